```python
import jax
import jax.numpy as jnp
from jax import lax
import numpy as np

D_MODEL = 1024
BATCH = 4
SEQ = 4096
DEPTH = 2
DEC_BATCH = 32
DEC_SEQ = 4
PAST_LEN = 16384
PAGE_SIZE = 128

N_MIXERS = 2
HEAD_DIM = 64
MEM_LEN = 256
MEM_HEADS = 4
MEM_W = MEM_HEADS * HEAD_DIM
MIX_W = D_MODEL - MEM_W
FOX_HEADS = MIX_W // HEAD_DIM
POOL_WINDOWS = (2, 4, 8, 16)
POOL_GROUPS = len(POOL_WINDOWS)
POOL_GW = MIX_W // POOL_GROUPS
POOL_STATE = max(POOL_WINDOWS) - 1
D_FF = 2816
N_EXPERTS = 8
TOP_K = 2
Q_BLOCK = 128
EPS = 1e-6
FORGET_BIAS_LO = 4.0
FORGET_BIAS_HI = 10.0
N_EVEN = (DEPTH + 1) // 2
N_ODD = DEPTH // 2

kernel_name = 'pool_fox_memory_hybrid_step'


def rmsnorm(x, g):
    x32 = x.astype(jnp.float32)
    y = x32 * lax.rsqrt(jnp.mean(x32 * x32, axis=-1, keepdims=True) + EPS)
    return y.astype(x.dtype) * g


def to_heads(t, n):
    return t.reshape(t.shape[:-1] + (n, HEAD_DIM))


def pool_mix(u_prev, u, start_pos, w_grp, scale):
    b, l, _ = u.shape
    ext = jnp.concatenate([u_prev, u], axis=1).astype(jnp.float32)
    cs = jnp.concatenate([jnp.zeros_like(ext[:, :1]), jnp.cumsum(ext, axis=1)], axis=1)
    end = cs[:, POOL_STATE + 1:POOL_STATE + 1 + l]
    pos = start_pos + jnp.arange(l)
    groups = []
    for g, w in enumerate(POOL_WINDOWS):
        sl = slice(g * POOL_GW, (g + 1) * POOL_GW)
        start = cs[:, POOL_STATE + 1 - w:POOL_STATE + 1 - w + l, sl]
        cnt = jnp.minimum(w, pos + 1).astype(jnp.float32)[None, :, None]
        groups.append((end[..., sl] - start) / cnt)
    pooled = jnp.stack(groups, axis=2)
    diff = (pooled - u.astype(jnp.float32).reshape(b, l, POOL_GROUPS, POOL_GW)).astype(u.dtype)
    y = jnp.einsum('blgc,gcd->blgd', diff, w_grp).reshape(b, l, MIX_W)
    return y * scale


def fox_project(h, w_in, b_f, qn_g, kn_g):
    z = h @ w_in
    q, k, v, fl, qm = jnp.split(z, [MIX_W, 2 * MIX_W, 3 * MIX_W, 3 * MIX_W + FOX_HEADS], axis=-1)
    q = rmsnorm(to_heads(q, FOX_HEADS), qn_g)
    k = rmsnorm(to_heads(k, FOX_HEADS), kn_g)
    v = to_heads(v, FOX_HEADS)
    logf = jax.nn.log_sigmoid((fl + b_f).astype(jnp.float32))
    return q, k, v, logf, qm


def fox_prompt(q, k, v, logf):
    b, s, h, _ = q.shape
    scale = HEAD_DIM ** -0.5
    c = jnp.cumsum(logf, axis=1)
    c_t = c.transpose(0, 2, 1)
    nb = s // Q_BLOCK
    qb = q.reshape(b, nb, Q_BLOCK, h, HEAD_DIM).transpose(1, 0, 2, 3, 4)
    cb = c.reshape(b, nb, Q_BLOCK, h).transpose(1, 0, 3, 2)
    kpos = jnp.arange(s)

    def block(args):
        i, qi, ci = args
        sc = jnp.einsum('bqhd,bkhd->bhqk', qi, k, preferred_element_type=jnp.float32) * scale
        bias = ci[..., :, None] - c_t[:, :, None, :]
        qpos = i * Q_BLOCK + jnp.arange(Q_BLOCK)
        mask = kpos[None, :] <= qpos[:, None]
        p = jax.nn.softmax(jnp.where(mask, sc + bias, -jnp.inf), axis=-1)
        return jnp.einsum('bhqk,bkhd->bqhd', p.astype(v.dtype), v)

    out = lax.map(block, (jnp.arange(nb), qb, cb))
    return out.transpose(1, 0, 2, 3, 4).reshape(b, s, MIX_W)


def gather_pages(pool, j, page_table):
    g = pool[j][page_table]
    return g.reshape((page_table.shape[0], -1) + pool.shape[3:])


def fox_sample(q, k_new, v_new, logf_new, k_past, v_past, logf_past):
    scale = HEAD_DIM ** -0.5
    lp = logf_past.astype(jnp.float32)
    d_past = (lax.cumsum(lp, axis=1, reverse=True) - lp).transpose(0, 2, 1)
    c_new = jnp.cumsum(logf_new, axis=1).transpose(0, 2, 1)
    s_past = (jnp.einsum('blhd,bphd->bhlp', q, k_past, preferred_element_type=jnp.float32) * scale
              + c_new[..., :, None] + d_past[:, :, None, :])
    s_new = (jnp.einsum('blhd,bmhd->bhlm', q, k_new, preferred_element_type=jnp.float32) * scale
             + c_new[..., :, None] - c_new[..., None, :])
    l = q.shape[1]
    s_new = jnp.where(jnp.tril(jnp.ones((l, l), dtype=bool)), s_new, -jnp.inf)
    p = jax.nn.softmax(jnp.concatenate([s_past, s_new], axis=-1), axis=-1).astype(v_new.dtype)
    n_past = k_past.shape[1]
    out = (jnp.einsum('bhlp,bphd->blhd', p[..., :n_past], v_past)
           + jnp.einsum('bhlm,bmhd->blhd', p[..., n_past:], v_new))
    return out.reshape(q.shape[0], l, MIX_W)


def mem_kv(mem, g_norm, w_kv, kn_g):
    k, v = jnp.split(rmsnorm(mem, g_norm) @ w_kv, 2, axis=-1)
    return rmsnorm(to_heads(k, MEM_HEADS), kn_g), to_heads(v, MEM_HEADS)


def mem_attend(qm, qn_g, mk, mv):
    q = rmsnorm(to_heads(qm, MEM_HEADS), qn_g)
    sc = jnp.einsum('blhd,bmhd->bhlm', q, mk, preferred_element_type=jnp.float32) * (HEAD_DIM ** -0.5)
    p = jax.nn.softmax(sc, axis=-1).astype(mv.dtype)
    out = jnp.einsum('bhlm,bmhd->blhd', p, mv)
    return out.reshape(q.shape[0], q.shape[1], MEM_W)


def swiglu(h, w_gu, w_down):
    g, u = jnp.split(h @ w_gu, 2, axis=-1)
    return (jax.nn.silu(g) * u) @ w_down


def moe_ffn(h, w_router, w_gu, w_down):
    logits = jnp.einsum('bld,de->ble', h, w_router, preferred_element_type=jnp.float32)
    top_v, top_i = lax.top_k(logits, TOP_K)
    gate = jnp.sum(jax.nn.one_hot(top_i, N_EXPERTS, dtype=jnp.float32)
                   * jax.nn.softmax(top_v, axis=-1)[..., None], axis=-2).astype(h.dtype)
    y = jnp.zeros_like(h)
    for e in range(N_EXPERTS):
        y = y + gate[..., e:e + 1] * swiglu(h, w_gu[e], w_down[e])
    return y


def setup_inputs(seed: int = 0) -> dict:
    key = jax.random.key(seed)
    ks = iter(jax.random.split(key, 40))

    def nrm(shape, scale):
        return jax.random.normal(next(ks), shape, jnp.float32) * scale

    def gain(shape):
        return 1.0 + nrm(shape, 0.02)

    n_pages = PAST_LEN // PAGE_SIZE
    n_used = DEC_BATCH * n_pages
    n_phys = n_used + (n_used + 3) // 4
    d = D_MODEL
    head_bias = jnp.linspace(FORGET_BIAS_LO, FORGET_BIAS_HI, FOX_HEADS)
    inputs = {}
    inputs['x_prompt'] = nrm((BATCH, SEQ, d), 1.0)
    inputs['x_sample'] = nrm((DEC_BATCH, DEC_SEQ, d), 1.0)
    inputs['cache_mem_k'] = nrm((DEPTH, DEC_BATCH, MEM_LEN, MEM_HEADS, HEAD_DIM), 1.0)
    inputs['cache_mem_v'] = nrm((DEPTH, DEC_BATCH, MEM_LEN, MEM_HEADS, HEAD_DIM), 1.0)
    inputs['state_pool'] = nrm((N_EVEN, DEC_BATCH, POOL_STATE, MIX_W), 1.0)
    inputs['cache_fox_k'] = nrm((N_ODD, n_phys, PAGE_SIZE, FOX_HEADS, HEAD_DIM), 1.0)
    inputs['cache_fox_v'] = nrm((N_ODD, n_phys, PAGE_SIZE, FOX_HEADS, HEAD_DIM), 1.0)
    inputs['cache_fox_logf'] = jax.nn.log_sigmoid(head_bias + nrm((N_ODD, n_phys, PAGE_SIZE, FOX_HEADS), 0.5))
    inputs['page_table'] = jax.random.permutation(next(ks), n_phys)[:n_used].reshape(DEC_BATCH, n_pages).astype(jnp.int32)
    inputs['mem_prompt'] = nrm((BATCH, MEM_LEN, d), 1.0)
    inputs['norm_mix_g'] = gain((DEPTH, d))
    inputs['norm_ffn_g'] = gain((DEPTH, d))
    inputs['norm_mem_g'] = gain((DEPTH, d))
    inputs['w_mem_kv'] = nrm((DEPTH, d, 2 * MEM_W), d ** -0.5)
    inputs['mem_q_norm_g'] = gain((DEPTH, HEAD_DIM))
    inputs['mem_k_norm_g'] = gain((DEPTH, HEAD_DIM))
    inputs['w_out'] = nrm((DEPTH, d, d), d ** -0.5)
    inputs['w_in_pool'] = nrm((N_EVEN, d, MIX_W + MEM_W), d ** -0.5)
    inputs['w_pool_group'] = nrm((N_EVEN, POOL_GROUPS, POOL_GW, POOL_GW), POOL_GW ** -0.5)
    inputs['pool_scale'] = gain((N_EVEN, MIX_W))
    inputs['w_in_fox'] = nrm((N_ODD, d, 3 * MIX_W + FOX_HEADS + MEM_W), d ** -0.5)
    inputs['b_forget'] = head_bias + nrm((N_ODD, FOX_HEADS), 0.3)
    inputs['fox_q_norm_g'] = gain((N_ODD, HEAD_DIM))
    inputs['fox_k_norm_g'] = gain((N_ODD, HEAD_DIM))
    inputs['w_ffn_gu'] = nrm((N_EVEN, d, 2 * D_FF), d ** -0.5)
    inputs['w_ffn_down'] = nrm((N_EVEN, D_FF, d), D_FF ** -0.5)
    inputs['w_router'] = nrm((N_ODD, d, N_EXPERTS), d ** -0.5)
    inputs['w_exp_gu'] = nrm((N_ODD, N_EXPERTS, d, 2 * D_FF), d ** -0.5)
    inputs['w_exp_down'] = nrm((N_ODD, N_EXPERTS, D_FF, d), D_FF ** -0.5)
    return inputs


def reference(x_prompt, x_sample, cache_mem_k, cache_mem_v, state_pool, cache_fox_k, cache_fox_v,
              cache_fox_logf, page_table, mem_prompt, norm_mix_g, norm_ffn_g, norm_mem_g, w_mem_kv,
              mem_q_norm_g, mem_k_norm_g, w_out, w_in_pool, w_pool_group, pool_scale, w_in_fox, b_forget,
              fox_q_norm_g, fox_k_norm_g, w_ffn_gu, w_ffn_down, w_router, w_exp_gu, w_exp_down):
    past_len = page_table.shape[1] * PAGE_SIZE
    b = x_prompt.shape[0]
    xp, xs = x_prompt, x_sample
    mem_k_p, mem_v_p, pool_p, pool_s = [], [], [], []
    fk_p, fv_p, fl_p, fk_s, fv_s, fl_s = [], [], [], [], [], []
    for i in range(DEPTH):
        j = i // N_MIXERS
        hp = rmsnorm(xp, norm_mix_g[i])
        hs = rmsnorm(xs, norm_mix_g[i])
        mkp, mvp = mem_kv(mem_prompt, norm_mem_g[i], w_mem_kv[i], mem_k_norm_g[i])
        mem_k_p.append(mkp)
        mem_v_p.append(mvp)
        if i % N_MIXERS == 0:
            up, qmp = jnp.split(hp @ w_in_pool[j], [MIX_W], axis=-1)
            us, qms = jnp.split(hs @ w_in_pool[j], [MIX_W], axis=-1)
            zeros_prev = jnp.zeros((b, POOL_STATE, MIX_W), up.dtype)
            mix_p = pool_mix(zeros_prev, up, 0, w_pool_group[j], pool_scale[j])
            mix_s = pool_mix(state_pool[j], us, past_len, w_pool_group[j], pool_scale[j])
            pool_p.append(up[:, -POOL_STATE:])
            pool_s.append(jnp.concatenate([state_pool[j], us], axis=1)[:, -POOL_STATE:])
        else:
            qp, kp, vp, lfp, qmp = fox_project(hp, w_in_fox[j], b_forget[j], fox_q_norm_g[j], fox_k_norm_g[j])
            qs, ksn, vsn, lfs, qms = fox_project(hs, w_in_fox[j], b_forget[j], fox_q_norm_g[j], fox_k_norm_g[j])
            mix_p = fox_prompt(qp, kp, vp, lfp)
            mix_s = fox_sample(qs, ksn, vsn, lfs,
                               gather_pages(cache_fox_k, j, page_table),
                               gather_pages(cache_fox_v, j, page_table),
                               gather_pages(cache_fox_logf, j, page_table))
            fk_p.append(kp)
            fv_p.append(vp)
            fl_p.append(lfp)
            fk_s.append(ksn)
            fv_s.append(vsn)
            fl_s.append(lfs)
        mem_out_p = mem_attend(qmp, mem_q_norm_g[i], mkp, mvp)
        mem_out_s = mem_attend(qms, mem_q_norm_g[i], cache_mem_k[i], cache_mem_v[i])
        xp = xp + jnp.concatenate([mix_p, mem_out_p], axis=-1) @ w_out[i]
        xs = xs + jnp.concatenate([mix_s, mem_out_s], axis=-1) @ w_out[i]
        hp = rmsnorm(xp, norm_ffn_g[i])
        hs = rmsnorm(xs, norm_ffn_g[i])
        if i % 2 == 0:
            xp = xp + swiglu(hp, w_ffn_gu[j], w_ffn_down[j])
            xs = xs + swiglu(hs, w_ffn_gu[j], w_ffn_down[j])
        else:
            xp = xp + moe_ffn(hp, w_router[j], w_exp_gu[j], w_exp_down[j])
            xs = xs + moe_ffn(hs, w_router[j], w_exp_gu[j], w_exp_down[j])
    return (xp, xs, jnp.stack(mem_k_p), jnp.stack(mem_v_p), jnp.stack(pool_p), jnp.stack(pool_s),
            jnp.stack(fk_p), jnp.stack(fv_p), jnp.stack(fl_p), jnp.stack(fk_s), jnp.stack(fv_s), jnp.stack(fl_s))
```

```python
import functools

import jax
import jax.numpy as jnp
from jax import lax
from jax.experimental import pallas as pl
from jax.experimental.pallas import tpu as pltpu

F32 = jnp.float32
BF16 = jnp.bfloat16

D_MODEL = 1024
HEAD_DIM = 64
MEM_LEN = 256
MEM_HEADS = 4
MEM_W = MEM_HEADS * HEAD_DIM
MIX_W = D_MODEL - MEM_W
FOX_HEADS = MIX_W // HEAD_DIM
POOL_WINDOWS = (2, 4, 8, 16)
POOL_GW = MIX_W // len(POOL_WINDOWS)
POOL_STATE = max(POOL_WINDOWS) - 1
D_FF = 2816
N_EXPERTS = 8
PAGE_SIZE = 128
EPS = 1e-6
SCALE = HEAD_DIM ** -0.5

LANES = 128
SUBLANES = 8
MXU_N = 256
VMEM_LIMIT = 56 * 1024 * 1024

FF_CHUNK = MXU_N
N_FF_CHUNKS = D_FF // FF_CHUNK
MOE_TM = 768
PAGES_PER_STEP = 8
SAMPLE_ROWS = 8


def _cparams(*sem, vmem=None):
    return pltpu.CompilerParams(dimension_semantics=sem, vmem_limit_bytes=vmem)


def _dot(a, b):
    return jnp.dot(a, b, preferred_element_type=F32)


def _dot_nt(a, b):
    return lax.dot_general(a, b, (((1,), (1,)), ((), ())), preferred_element_type=F32)


def _rms_rows(x, g):
    return x * lax.rsqrt(jnp.mean(x * x, axis=-1, keepdims=True) + EPS) * g


def _split3(x):
    hi = x.astype(BF16)
    r = x - hi.astype(F32)
    mid = r.astype(BF16)
    lo = (r - mid.astype(F32)).astype(BF16)
    return hi, mid, lo


def _head_sumsq_lanes(x):
    r = lax.broadcasted_iota(jnp.int32, (LANES, LANES), 0) // HEAD_DIM
    c = lax.broadcasted_iota(jnp.int32, (LANES, LANES), 1) // HEAD_DIM
    ones_bd = (r == c).astype(BF16)
    xx = x * x
    hi = xx.astype(BF16)
    lo = (xx - hi.astype(F32)).astype(BF16)
    parts = []
    for j in range(x.shape[1] // LANES):
        sl = slice(j * LANES, (j + 1) * LANES)
        parts.append(_dot(hi[:, sl], ones_bd) + _dot(lo[:, sl], ones_bd))
    return jnp.concatenate(parts, axis=1)


def _head_rms_lanes(x, g_tiled):
    ssq = _head_sumsq_lanes(x)
    return x * lax.rsqrt(ssq * (1.0 / HEAD_DIM) + EPS) * g_tiled


def _head_rms_rows_t(xt, g_col, n_heads):
    outs = []
    for h in range(n_heads):
        blk = xt[h * HEAD_DIM:(h + 1) * HEAD_DIM, :]
        ms = jnp.mean(blk * blk, axis=0, keepdims=True)
        outs.append(blk * lax.rsqrt(ms + EPS) * g_col)
    return outs


def _log_sigmoid(x):
    return jnp.minimum(x, 0.0) - jnp.log1p(jnp.exp(-jnp.abs(x)))


def _lane_head(shape, axis):
    return lax.broadcasted_iota(jnp.int32, shape, axis) // HEAD_DIM


def _memkv_body(mem_ref, gm_ref, w_ref, kg_ref, kt_ref, vt_ref):
    h = _rms_rows(mem_ref[0], gm_ref[0])
    z = _dot(h, w_ref[0])
    kt = z[:, :MEM_W].T
    vt_ref[0, 0] = z[:, MEM_W:].T
    pieces = _head_rms_rows_t(kt, kg_ref[0], MEM_HEADS)
    for h_i, p in enumerate(pieces):
        kt_ref[0, 0, h_i * HEAD_DIM:(h_i + 1) * HEAD_DIM, :] = p


def _memkv(mem, g_mem, w_kv, kn_g):
    depth, batch = w_kv.shape[0], mem.shape[0]
    out = jax.ShapeDtypeStruct((depth, batch, MEM_W, MEM_LEN), F32)
    return pl.pallas_call(
        _memkv_body,
        grid=(depth, batch),
        in_specs=[
            pl.BlockSpec((1, MEM_LEN, D_MODEL), lambda i, b: (b, 0, 0)),
            pl.BlockSpec((1, 1, D_MODEL), lambda i, b: (i, 0, 0)),
            pl.BlockSpec((1, D_MODEL, 2 * MEM_W), lambda i, b: (i, 0, 0)),
            pl.BlockSpec((1, HEAD_DIM, 1), lambda i, b: (i, 0, 0)),
        ],
        out_specs=[pl.BlockSpec((1, 1, MEM_W, MEM_LEN), lambda i, b: (i, b, 0, 0))] * 2,
        out_shape=[out, out],
        compiler_params=_cparams("arbitrary", "arbitrary"),
        name="memkv",
    )(mem, g_mem[:, None, :], w_kv, kn_g[:, :, None])


def _pool_proj_body(x_ref, g_ref, w_ref, u_ref, qm_ref):
    h = _rms_rows(x_ref[...], g_ref[...])
    u_ref[...] = _dot(h, w_ref[:, :MIX_W])
    qm_ref[...] = _dot(h, w_ref[:, MIX_W:])


def _pool_proj(x, g, w, tm):
    t = x.shape[0]
    return pl.pallas_call(
        _pool_proj_body,
        grid=(t // tm,),
        in_specs=[
            pl.BlockSpec((tm, D_MODEL), lambda i: (i, 0)),
            pl.BlockSpec((1, D_MODEL), lambda i: (0, 0)),
            pl.BlockSpec((D_MODEL, D_MODEL), lambda i: (0, 0)),
        ],
        out_specs=[pl.BlockSpec((tm, MIX_W), lambda i: (i, 0)),
                   pl.BlockSpec((tm, MEM_W), lambda i: (i, 0))],
        out_shape=[jax.ShapeDtypeStruct((t, MIX_W), F32), jax.ShapeDtypeStruct((t, MEM_W), F32)],
        compiler_params=_cparams("arbitrary", vmem=VMEM_LIMIT),
        name="pool_proj",
    )(x, g, w)


def _window_of_lane(shape):
    lane = lax.broadcasted_iota(jnp.int32, shape, len(shape) - 1)
    return jnp.where(lane < POOL_GW, POOL_WINDOWS[0],
                     jnp.where(lane < 2 * POOL_GW, POOL_WINDOWS[1],
                               jnp.where(lane < 3 * POOL_GW, POOL_WINDOWS[2], POOL_WINDOWS[3])))


def _pool_select(s2, s4, s8, s16, shape):
    lane = lax.broadcasted_iota(jnp.int32, shape, len(shape) - 1)
    return jnp.where(lane < POOL_GW, s2,
                     jnp.where(lane < 2 * POOL_GW, s4, jnp.where(lane < 3 * POOL_GW, s8, s16)))


_POOL_PAD = SUBLANES
_POOL_HALO = 2 * SUBLANES
_POOL_BASE = _POOL_PAD + _POOL_HALO


def _pool_prompt_body(u_ref, w_ref, sc_ref, o_ref, e_ref, s2_ref, s4_ref, s8_ref, *, tl):
    li = pl.program_id(1)
    n = _POOL_HALO + tl

    @pl.when(li == 0)
    def _():
        e_ref[0:_POOL_BASE, :] = jnp.zeros((_POOL_BASE, MIX_W), F32)
        s2_ref[0:_POOL_PAD, :] = jnp.zeros((_POOL_PAD, MIX_W), F32)
        s4_ref[0:_POOL_PAD, :] = jnp.zeros((_POOL_PAD, MIX_W), F32)
        s8_ref[0:_POOL_PAD, :] = jnp.zeros((_POOL_PAD, MIX_W), F32)

    u = u_ref[0]
    e_ref[_POOL_BASE:_POOL_BASE + tl, :] = u
    s2 = e_ref[_POOL_PAD:_POOL_PAD + n, :] + e_ref[_POOL_PAD - 1:_POOL_PAD - 1 + n, :]
    s2_ref[_POOL_PAD:_POOL_PAD + n, :] = s2
    s4 = s2 + s2_ref[_POOL_PAD - 2:_POOL_PAD - 2 + n, :]
    s4_ref[_POOL_PAD:_POOL_PAD + n, :] = s4
    s8 = s4 + s4_ref[_POOL_PAD - 4:_POOL_PAD - 4 + n, :]
    s8_ref[_POOL_PAD:_POOL_PAD + n, :] = s8
    s16 = s8[_POOL_HALO:, :] + s8_ref[_POOL_BASE - 8:_POOL_BASE - 8 + tl, :]
    shape = (tl, MIX_W)
    ssel = _pool_select(s2[_POOL_HALO:, :], s4[_POOL_HALO:, :], s8[_POOL_HALO:, :], s16, shape)
    pos = li * tl + lax.broadcasted_iota(jnp.int32, shape, 0)
    cnt = jnp.minimum(_window_of_lane(shape), pos + 1).astype(F32)
    diff = ssel / cnt - u
    o_ref[0] = _dot(diff, w_ref[...]) * sc_ref[...]
    e_ref[_POOL_PAD:_POOL_BASE, :] = e_ref[_POOL_PAD + tl:_POOL_BASE + tl, :]


def _pool_prompt(u, w_bd, scale, tl):
    b, s, _ = u.shape
    rows = _POOL_BASE + tl
    return pl.pallas_call(
        functools.partial(_pool_prompt_body, tl=tl),
        grid=(b, s // tl),
        in_specs=[
            pl.BlockSpec((1, tl, MIX_W), lambda bi, li: (bi, li, 0)),
            pl.BlockSpec((MIX_W, MIX_W), lambda bi, li: (0, 0)),
            pl.BlockSpec((1, MIX_W), lambda bi, li: (0, 0)),
        ],
        out_specs=pl.BlockSpec((1, tl, MIX_W), lambda bi, li: (bi, li, 0)),
        out_shape=jax.ShapeDtypeStruct((b, s, MIX_W), F32),
        scratch_shapes=[pltpu.VMEM((rows, MIX_W), F32)] * 4,
        compiler_params=_cparams("arbitrary", "arbitrary", vmem=VMEM_LIMIT),
        name="pool_prompt",
    )(u, w_bd, scale)


def _pool_sample_body(st_ref, u_ref, w_ref, sc_ref, o_ref, ns_ref, *, n_new):
    def ext(j):
        return st_ref[j] if j < POOL_STATE else u_ref[j - POOL_STATE]

    for l in range(n_new):
        r = POOL_STATE + l
        s2 = ext(r) + ext(r - 1)
        s4 = s2 + ext(r - 2) + ext(r - 3)
        s8 = s4
        for j in range(4, 8):
            s8 = s8 + ext(r - j)
        s16 = s8
        for j in range(8, 16):
            s16 = s16 + ext(r - j)
        shape = s2.shape
        cnt = _window_of_lane(shape).astype(F32)
        diff = _pool_select(s2, s4, s8, s16, shape) / cnt - ext(r)
        o_ref[l] = _dot(diff, w_ref[...]) * sc_ref[...]
    for j in range(POOL_STATE):
        ns_ref[j] = ext(j + n_new)


def _pool_sample(state_t, u_t, w_bd, scale):
    n_new, b, _ = u_t.shape
    return pl.pallas_call(
        functools.partial(_pool_sample_body, n_new=n_new),
        out_shape=[jax.ShapeDtypeStruct((n_new, b, MIX_W), F32),
                   jax.ShapeDtypeStruct((POOL_STATE, b, MIX_W), F32)],
        name="pool_sample",
    )(state_t, u_t, w_bd, scale)


def _mem_attend_body(q_ref, kt_ref, vt_ref, g_ref, o_ref):
    q = _head_rms_lanes(q_ref[0], g_ref[...]) * SCALE
    kt = kt_ref[0]
    vt = vt_ref[0]
    lh = _lane_head(q.shape, 1)
    out = jnp.zeros(q.shape, F32)
    for h in range(MEM_HEADS):
        s = _dot(jnp.where(lh == h, q, 0.0), kt)
        e = jnp.exp(s - jnp.max(s, axis=-1, keepdims=True))
        p = e / jnp.sum(e, axis=-1, keepdims=True)
        out = jnp.where(lh == h, _dot_nt(p, vt), out)
    o_ref[0] = out


def _mem_attend(qm, kt, vt, qn_g, tl):
    b, l, _ = qm.shape
    return pl.pallas_call(
        _mem_attend_body,
        grid=(b, l // tl),
        in_specs=[
            pl.BlockSpec((1, tl, MEM_W), lambda bi, li: (bi, li, 0)),
            pl.BlockSpec((1, MEM_W, MEM_LEN), lambda bi, li: (bi, 0, 0)),
            pl.BlockSpec((1, MEM_W, MEM_LEN), lambda bi, li: (bi, 0, 0)),
            pl.BlockSpec((1, MEM_W), lambda bi, li: (0, 0)),
        ],
        out_specs=pl.BlockSpec((1, tl, MEM_W), lambda bi, li: (bi, li, 0)),
        out_shape=jax.ShapeDtypeStruct((b, l, MEM_W), F32),
        compiler_params=_cparams("arbitrary", "arbitrary"),
        name="mem_attend",
    )(qm, kt, vt, jnp.tile(qn_g, MEM_HEADS)[None, :])


def _out_proj_body(x_ref, mix_ref, mem_ref, w_ref, o_ref):
    o_ref[...] = (x_ref[...] + _dot(mix_ref[...], w_ref[:MIX_W, :])
                  + _dot(mem_ref[...], w_ref[MIX_W:, :]))


def _out_proj(x, mix, mem, w, tm):
    t = x.shape[0]
    return pl.pallas_call(
        _out_proj_body,
        grid=(t // tm,),
        in_specs=[
            pl.BlockSpec((tm, D_MODEL), lambda i: (i, 0)),
            pl.BlockSpec((tm, MIX_W), lambda i: (i, 0)),
            pl.BlockSpec((tm, MEM_W), lambda i: (i, 0)),
            pl.BlockSpec((D_MODEL, D_MODEL), lambda i: (0, 0)),
        ],
        out_specs=pl.BlockSpec((tm, D_MODEL), lambda i: (i, 0)),
        out_shape=jax.ShapeDtypeStruct((t, D_MODEL), F32),
        compiler_params=_cparams("arbitrary", vmem=VMEM_LIMIT),
        name="out_proj",
    )(x, mix, mem, w)


def _silu(g):
    return g / (1.0 + jnp.exp(-g))


def _ffn_body(x_ref, g_ref, wg_ref, wu_ref, wd_ref, o_ref, h_ref):
    c = pl.program_id(1)

    @pl.when(c == 0)
    def _():
        x = x_ref[...]
        h_ref[...] = _rms_rows(x, g_ref[...])
        o_ref[...] = x

    h = h_ref[...]
    a = _silu(_dot(h, wg_ref[...])) * _dot(h, wu_ref[...])
    o_ref[...] += _dot(a, wd_ref[...])


def _ffn(x, g, w_gu, w_down, tm):
    t = x.shape[0]
    return pl.pallas_call(
        _ffn_body,
        grid=(t // tm, N_FF_CHUNKS),
        in_specs=[
            pl.BlockSpec((tm, D_MODEL), lambda i, c: (i, 0)),
            pl.BlockSpec((1, D_MODEL), lambda i, c: (0, 0)),
            pl.BlockSpec((D_MODEL, FF_CHUNK), lambda i, c: (0, c)),
            pl.BlockSpec((D_MODEL, FF_CHUNK), lambda i, c: (0, N_FF_CHUNKS + c)),
            pl.BlockSpec((FF_CHUNK, D_MODEL), lambda i, c: (c, 0)),
        ],
        out_specs=pl.BlockSpec((tm, D_MODEL), lambda i, c: (i, 0)),
        out_shape=jax.ShapeDtypeStruct((t, D_MODEL), F32),
        scratch_shapes=[pltpu.VMEM((tm, D_MODEL), F32)],
        compiler_params=_cparams("arbitrary", "arbitrary", vmem=VMEM_LIMIT),
        name="ffn",
    )(x, g, w_gu, w_gu, w_down)


_FL_ROWS = 2 * SUBLANES


def _fox_proj_body(x_ref, g_ref, wq_ref, wqm_ref, wkt_ref, wvt_ref, wft_ref, bf_ref, qg_ref, kg_ref,
                   q_ref, kt_ref, vt_ref, lft_ref, qm_ref, *rows_refs):
    h = _rms_rows(x_ref[...], g_ref[...])
    q_ref[...] = _head_rms_lanes(_dot(h, wq_ref[...]), qg_ref[...])
    qm_ref[...] = _dot(h, wqm_ref[...])
    kt = _dot_nt(wkt_ref[...], h)
    pieces = _head_rms_rows_t(kt, kg_ref[...], FOX_HEADS)
    for h_i, p in enumerate(pieces):
        kt_ref[0, h_i * HEAD_DIM:(h_i + 1) * HEAD_DIM, :] = p
    vt = _dot_nt(wvt_ref[...], h)
    vt_ref[0] = vt
    fl = _dot_nt(wft_ref[...], h) + bf_ref[...]
    head = lax.broadcasted_iota(jnp.int32, fl.shape, 0)
    lft_ref[...] = jnp.where(head < FOX_HEADS, _log_sigmoid(fl), 0.0)
    if rows_refs:
        k_rows_ref, v_rows_ref = rows_refs
        k_rows_ref[...] = jnp.concatenate(pieces, axis=0).T
        v_rows_ref[...] = vt.T


def _fox_proj(x, g, wq, wqm, wkt, wvt, wft, bf_col, qg_t, kg_col, tm, n_seq, rows_out):
    t = x.shape[0]
    per_seq = t // n_seq // tm

    def const(shape):
        return pl.BlockSpec(shape, lambda i: tuple(0 for _ in shape))

    out_specs = [
        pl.BlockSpec((tm, MIX_W), lambda i: (i, 0)),
        pl.BlockSpec((1, MIX_W, tm), lambda i: (i // per_seq, 0, i % per_seq)),
        pl.BlockSpec((1, MIX_W, tm), lambda i: (i // per_seq, 0, i % per_seq)),
        pl.BlockSpec((_FL_ROWS, tm), lambda i: (0, i)),
        pl.BlockSpec((tm, MEM_W), lambda i: (i, 0)),
    ]
    out_shape = [
        jax.ShapeDtypeStruct((t, MIX_W), F32),
        jax.ShapeDtypeStruct((n_seq, MIX_W, t // n_seq), F32),
        jax.ShapeDtypeStruct((n_seq, MIX_W, t // n_seq), F32),
        jax.ShapeDtypeStruct((_FL_ROWS, t), F32),
        jax.ShapeDtypeStruct((t, MEM_W), F32),
    ]
    if rows_out:
        out_specs += [pl.BlockSpec((tm, MIX_W), lambda i: (i, 0))] * 2
        out_shape += [jax.ShapeDtypeStruct((t, MIX_W), F32)] * 2
    return pl.pallas_call(
        _fox_proj_body,
        grid=(t // tm,),
        in_specs=[
            pl.BlockSpec((tm, D_MODEL), lambda i: (i, 0)),
            const((1, D_MODEL)),
            const((D_MODEL, MIX_W)), const((D_MODEL, MEM_W)),
            const((MIX_W, D_MODEL)), const((MIX_W, D_MODEL)), const((_FL_ROWS, D_MODEL)),
            const((_FL_ROWS, 1)), const((1, MIX_W)), const((HEAD_DIM, 1)),
        ],
        out_specs=out_specs,
        out_shape=out_shape,
        compiler_params=_cparams("arbitrary", vmem=VMEM_LIMIT),
        name="fox_proj",
    )(x, g, wq, wqm, wkt, wvt, wft, bf_col, qg_t, kg_col)


def _cumsum_body(lf_ref, c_ref, carry_ref, *, tl):
    @pl.when(pl.program_id(1) == 0)
    def _():
        carry_ref[...] = jnp.zeros(carry_ref.shape, F32)

    r = lax.broadcasted_iota(jnp.int32, (tl, tl), 0)
    c = lax.broadcasted_iota(jnp.int32, (tl, tl), 1)
    upper = (r <= c).astype(BF16)
    hi, mid, lo = _split3(lf_ref[...])
    acc = _dot(hi, upper) + _dot(mid, upper) + _dot(lo, upper) + carry_ref[:, 0:1]
    c_ref[...] = acc
    carry_ref[...] = jnp.broadcast_to(acc[:, tl - 1:tl], carry_ref.shape)


def _cumsum_tokens(lft, n_seq, tl):
    heads, t = lft.shape
    per_seq = t // n_seq // tl
    return pl.pallas_call(
        functools.partial(_cumsum_body, tl=tl),
        grid=(n_seq, per_seq),
        in_specs=[pl.BlockSpec((heads, tl), lambda b, i: (0, b * per_seq + i))],
        out_specs=pl.BlockSpec((heads, tl), lambda b, i: (0, b * per_seq + i)),
        out_shape=jax.ShapeDtypeStruct((heads, t), F32),
        scratch_shapes=[pltpu.VMEM((_FL_ROWS, LANES), F32)],
        compiler_params=_cparams("arbitrary", "arbitrary"),
        name="cumsum_logf",
    )(lft)


def _fox_flash_body(q_ref, kt_ref, vt_ref, cc_ref, cr_ref, o_ref, m_ref, l_ref, acc_ref, *, bq, bk):
    qi = pl.program_id(2)
    ki = pl.program_id(3)

    @pl.when(ki == 0)
    def _():
        m_ref[...] = jnp.full(m_ref.shape, -jnp.inf, F32)
        l_ref[...] = jnp.zeros(l_ref.shape, F32)
        acc_ref[...] = jnp.zeros(acc_ref.shape, F32)

    @pl.when(ki <= qi)
    def _():
        q = q_ref[0] * SCALE
        kt = kt_ref[0]
        vt = vt_ref[0]
        lane = lax.broadcasted_iota(jnp.int32, q.shape, 1)
        row = qi * bq + lax.broadcasted_iota(jnp.int32, (bq, bk), 0)
        col = ki * bk + lax.broadcasted_iota(jnp.int32, (bq, bk), 1)
        causal = col <= row
        for hh in range(2):
            qh = jnp.where(lane // HEAD_DIM == hh, q, 0.0)
            s = _dot(qh, kt) + (cc_ref[0, 0, :, hh:hh + 1] - cr_ref[0, hh:hh + 1, :])
            s = jnp.where(causal, s, -jnp.inf)
            m_prev = m_ref[hh]
            m_new = jnp.maximum(m_prev, jnp.max(s, axis=-1, keepdims=True))
            alpha = jnp.exp(m_prev - m_new)
            p = jnp.exp(s - m_new)
            l_ref[hh] = alpha * l_ref[hh] + jnp.sum(p, axis=-1, keepdims=True)
            acc_ref[hh] = alpha * acc_ref[hh] + _dot_nt(p, vt)
            m_ref[hh] = m_new

    @pl.when(ki == qi)
    def _():
        lane = lax.broadcasted_iota(jnp.int32, (bq, LANES), 1)
        o_ref[0] = jnp.where(lane < HEAD_DIM, acc_ref[0] / l_ref[0], acc_ref[1] / l_ref[1])


def _fox_flash(q, kt, vt, c_col, c_row, bq, bk):
    b, s, _ = q.shape
    nq, nk = s // bq, s // bk
    pairs = FOX_HEADS // 2
    return pl.pallas_call(
        functools.partial(_fox_flash_body, bq=bq, bk=bk),
        grid=(b, pairs, nq, nk),
        in_specs=[
            pl.BlockSpec((1, bq, LANES), lambda bi, hp, qi, ki: (bi, qi, hp)),
            pl.BlockSpec((1, LANES, bk), lambda bi, hp, qi, ki: (bi, hp, jnp.minimum(ki, qi))),
            pl.BlockSpec((1, LANES, bk), lambda bi, hp, qi, ki: (bi, hp, jnp.minimum(ki, qi))),
            pl.BlockSpec((1, 1, bq, 2), lambda bi, hp, qi, ki: (bi, hp, qi, 0)),
            pl.BlockSpec((1, 2, bk), lambda bi, hp, qi, ki: (hp, 0, bi * nk + jnp.minimum(ki, qi))),
        ],
        out_specs=pl.BlockSpec((1, bq, LANES), lambda bi, hp, qi, ki: (bi, qi, hp)),
        out_shape=jax.ShapeDtypeStruct((b, s, MIX_W), F32),
        scratch_shapes=[pltpu.VMEM((2, bq, 1), F32), pltpu.VMEM((2, bq, 1), F32),
                        pltpu.VMEM((2, bq, LANES), F32)],
        compiler_params=_cparams("arbitrary", "arbitrary", "arbitrary", "arbitrary", vmem=VMEM_LIMIT),
        name="fox_flash",
    )(q, kt, vt, c_col, c_row)


_QROWS = FOX_HEADS * SAMPLE_ROWS


def _expand_heads(x):
    n = x.shape[1]
    return jnp.broadcast_to(x[:FOX_HEADS, None, :], (FOX_HEADS, SAMPLE_ROWS, n)).reshape(_QROWS, n)


def _fox_decode_body(pt_ref, *refs, n_new):
    del pt_ref
    np_ = PAGES_PER_STEP
    kt_refs = refs[0:np_]
    vt_refs = refs[np_:2 * np_]
    lf_refs = refs[2 * np_:3 * np_]
    q_ref, kn_ref, vn_ref, lfn_ref, o_ref, qbd_ref, m_ref, l_ref, acc_ref, carry_ref, crow_ref = refs[3 * np_:]
    j = pl.program_id(1)
    row_l = lax.broadcasted_iota(jnp.int32, (_QROWS, 1), 0) % SAMPLE_ROWS

    def online_update(s, v_apply):
        m_prev = m_ref[...]
        m_new = jnp.maximum(m_prev, jnp.max(s, axis=-1, keepdims=True))
        alpha = jnp.exp(m_prev - m_new)
        p = jnp.exp(s - m_new)
        l_ref[...] = alpha * l_ref[...] + jnp.sum(p, axis=-1, keepdims=True)
        acc_ref[...] = alpha * acc_ref[...] + v_apply(p)
        m_ref[...] = m_new

    @pl.when(j == 0)
    def _():
        q = q_ref[0] * SCALE
        lh = _lane_head(q.shape, 1)
        for h in range(FOX_HEADS):
            qbd_ref[h * SAMPLE_ROWS:(h + 1) * SAMPLE_ROWS, :] = jnp.where(lh == h, q, 0.0)
        lfn = lfn_ref[0]
        lane = lax.broadcasted_iota(jnp.int32, lfn.shape, 1)
        c = jnp.zeros(lfn.shape, F32)
        for m in range(n_new):
            cm = jnp.sum(jnp.where(lane <= m, lfn, 0.0), axis=1, keepdims=True)
            c = jnp.where(lane == m, cm, c)
        c_q = _expand_heads(c)
        col = lax.broadcasted_iota(jnp.int32, c_q.shape, 1)
        crow = jnp.sum(jnp.where(col == row_l, c_q, 0.0), axis=1, keepdims=True)
        crow_ref[...] = crow
        s = _dot_nt(qbd_ref[...], kn_ref[0]) + (crow - c_q)
        valid = (col < n_new) & ((col <= row_l) | (row_l >= n_new))
        s = jnp.where(valid, s, -jnp.inf)
        m0 = jnp.max(s, axis=-1, keepdims=True)
        p = jnp.exp(s - m0)
        m_ref[...] = m0
        l_ref[...] = jnp.sum(p, axis=-1, keepdims=True)
        acc_ref[...] = _dot(p, vn_ref[0])
        carry_ref[...] = jnp.zeros(carry_ref.shape, F32)

    r = lax.broadcasted_iota(jnp.int32, (PAGE_SIZE, PAGE_SIZE), 0)
    cc = lax.broadcasted_iota(jnp.int32, (PAGE_SIZE, PAGE_SIZE), 1)
    later = (r > cc).astype(BF16)
    qbd = qbd_ref[...]
    crow = crow_ref[...]
    for i in range(np_):
        lf = lf_refs[i][0]
        hi, mid, lo = _split3(lf)
        d = _dot(hi, later) + _dot(mid, later) + _dot(lo, later) + carry_ref[:, 0:1]
        carry_ref[...] = carry_ref[...] + jnp.sum(lf, axis=1, keepdims=True)
        kt = kt_refs[i][0].reshape(MIX_W, PAGE_SIZE)
        vt = vt_refs[i][0].reshape(MIX_W, PAGE_SIZE)
        s = _dot(qbd, kt) + crow + _expand_heads(d)
        online_update(s, lambda p: _dot_nt(p, vt))

    @pl.when(j == pl.num_programs(1) - 1)
    def _():
        res = acc_ref[...] / l_ref[...]
        lh = _lane_head((SAMPLE_ROWS, MIX_W), 1)
        out = jnp.zeros((SAMPLE_ROWS, MIX_W), F32)
        for h in range(FOX_HEADS):
            out = jnp.where(lh == h, res[h * SAMPLE_ROWS:(h + 1) * SAMPLE_ROWS, :], out)
        o_ref[0] = out


def _fox_decode(page_table, kt_pages, vt_pages, lf_pages, q8, kn8, vn8, lfn, n_new):
    b, n_pages = page_table.shape
    np_ = PAGES_PER_STEP
    steps = n_pages // np_

    def page_map(i):
        return lambda bi, j, pt: (pt[bi, n_pages - 1 - (j * np_ + i)], 0, 0, 0)

    def lf_map(i):
        return lambda bi, j, pt: (pt[bi, n_pages - 1 - (j * np_ + i)], 0, 0)

    kv_specs = [pl.BlockSpec((1, FOX_HEADS, HEAD_DIM, PAGE_SIZE), page_map(i)) for i in range(np_)]
    lf_specs = [pl.BlockSpec((1, _FL_ROWS, PAGE_SIZE), lf_map(i)) for i in range(np_)]
    per_b = lambda bi, j, pt: (bi, 0, 0)
    grid_spec = pltpu.PrefetchScalarGridSpec(
        num_scalar_prefetch=1,
        grid=(b, steps),
        in_specs=kv_specs + kv_specs + lf_specs + [
            pl.BlockSpec((1, SAMPLE_ROWS, MIX_W), per_b),
            pl.BlockSpec((1, SAMPLE_ROWS, MIX_W), per_b),
            pl.BlockSpec((1, SAMPLE_ROWS, MIX_W), per_b),
            pl.BlockSpec((1, _FL_ROWS, SAMPLE_ROWS), per_b),
        ],
        out_specs=pl.BlockSpec((1, SAMPLE_ROWS, MIX_W), per_b),
        scratch_shapes=[
            pltpu.VMEM((_QROWS, MIX_W), F32),
            pltpu.VMEM((_QROWS, 1), F32), pltpu.VMEM((_QROWS, 1), F32),
            pltpu.VMEM((_QROWS, MIX_W), F32),
            pltpu.VMEM((_FL_ROWS, LANES), F32),
            pltpu.VMEM((_QROWS, 1), F32),
        ],
    )
    return pl.pallas_call(
        functools.partial(_fox_decode_body, n_new=n_new),
        grid_spec=grid_spec,
        out_shape=jax.ShapeDtypeStruct((b, SAMPLE_ROWS, MIX_W), F32),
        compiler_params=_cparams("arbitrary", "arbitrary", vmem=VMEM_LIMIT),
        name="fox_decode",
    )(page_table, *([kt_pages] * np_), *([vt_pages] * np_), *([lf_pages] * np_), q8, kn8, vn8, lfn)


def _router_body(x_ref, g_ref, wrt_ref, h_ref, r_ref, cnt_ref, carry_ref, *, tm):
    @pl.when(pl.program_id(0) == 0)
    def _():
        carry_ref[...] = jnp.zeros(carry_ref.shape, F32)

    h = _rms_rows(x_ref[...], g_ref[...])
    h_ref[...] = h
    h_hi = h.astype(BF16)
    h_lo = (h - h_hi.astype(F32)).astype(BF16)
    w = wrt_ref[...]
    w_hi = w.astype(BF16)
    w_lo = (w - w_hi.astype(F32)).astype(BF16)
    lg = _dot_nt(w_hi, h_hi) + _dot_nt(w_hi, h_lo) + _dot_nt(w_lo, h_hi)
    idx = lax.broadcasted_iota(jnp.int32, lg.shape, 0)
    m1 = jnp.max(lg, axis=0, keepdims=True)
    i1 = jnp.min(jnp.where(lg == m1, idx, N_EXPERTS), axis=0, keepdims=True)
    sel1 = idx == i1
    lg2 = jnp.where(sel1, -jnp.inf, lg)
    m2 = jnp.max(lg2, axis=0, keepdims=True)
    i2 = jnp.min(jnp.where(lg2 == m2, idx, N_EXPERTS), axis=0, keepdims=True)
    sel2 = idx == i2
    e = jnp.exp(m2 - m1)
    g1 = 1.0 / (1.0 + e)
    g2 = e / (1.0 + e)
    assign = jnp.where(sel1 | sel2, 1.0, 0.0)
    r = lax.broadcasted_iota(jnp.int32, (tm, tm), 0)
    c = lax.broadcasted_iota(jnp.int32, (tm, tm), 1)
    before = (r < c).astype(BF16)
    rank = _dot(assign.astype(BF16), before) + carry_ref[:, 0:1]
    r1 = jnp.sum(jnp.where(sel1, rank, 0.0), axis=0, keepdims=True)
    r2 = jnp.sum(jnp.where(sel2, rank, 0.0), axis=0, keepdims=True)
    carry = carry_ref[...] + jnp.sum(assign, axis=1, keepdims=True)
    carry_ref[...] = carry
    cnt_ref[...] = carry
    rows = [i1.astype(F32), i2.astype(F32), r1, r2, g1, g2]
    out = jnp.zeros(lg.shape, F32)
    for k, v in enumerate(rows):
        out = jnp.where(idx == k, v, out)
    r_ref[...] = out


def _router(x, g, wrt, tm):
    t = x.shape[0]
    return pl.pallas_call(
        functools.partial(_router_body, tm=tm),
        grid=(t // tm,),
        in_specs=[
            pl.BlockSpec((tm, D_MODEL), lambda i: (i, 0)),
            pl.BlockSpec((1, D_MODEL), lambda i: (0, 0)),
            pl.BlockSpec((N_EXPERTS, D_MODEL), lambda i: (0, 0)),
        ],
        out_specs=[
            pl.BlockSpec((tm, D_MODEL), lambda i: (i, 0)),
            pl.BlockSpec((N_EXPERTS, tm), lambda i: (0, i)),
            pl.BlockSpec((N_EXPERTS, LANES), lambda i: (0, 0)),
        ],
        out_shape=[
            jax.ShapeDtypeStruct((t, D_MODEL), F32),
            jax.ShapeDtypeStruct((N_EXPERTS, t), F32),
            jax.ShapeDtypeStruct((N_EXPERTS, LANES), F32),
        ],
        scratch_shapes=[pltpu.VMEM((N_EXPERTS, LANES), F32)],
        compiler_params=_cparams("arbitrary", vmem=VMEM_LIMIT),
        name="router",
    )(x, g, wrt)


_SCATTER_TM = 128


def _row_copy(src_ref, src_row, dst_ref, dst_row, sem):
    return pltpu.make_async_copy(src_ref.at[pl.ds(src_row, 1)], dst_ref.at[pl.ds(dst_row, 1)], sem)


def _scatter_body(d1_ref, d2_ref, hp_ref, hs_ref, xs_ref, sem, *, n_prompt_tiles):
    i = pl.program_id(0)

    def run(src_ref):
        def start(r, carry):
            _row_copy(src_ref, r, xs_ref, d1_ref[0, 0, r], sem.at[0]).start()
            _row_copy(src_ref, r, xs_ref, d2_ref[0, 0, r], sem.at[1]).start()
            return carry

        def wait(r, carry):
            _row_copy(src_ref, r, xs_ref, d1_ref[0, 0, r], sem.at[0]).wait()
            _row_copy(src_ref, r, xs_ref, d2_ref[0, 0, r], sem.at[1]).wait()
            return carry

        lax.fori_loop(0, _SCATTER_TM, start, 0)
        lax.fori_loop(0, _SCATTER_TM, wait, 0)

    @pl.when(i < n_prompt_tiles)
    def _():
        run(hp_ref)

    @pl.when(i >= n_prompt_tiles)
    def _():
        run(hs_ref)


def _scatter_rows(d1, d2, h_p, h_s, n_sorted):
    tm = _SCATTER_TM
    npt = h_p.shape[0] // tm
    nst = h_s.shape[0] // tm
    smem = lambda: pl.BlockSpec((1, 1, tm), lambda i: (i, 0, 0), memory_space=pltpu.SMEM)
    return pl.pallas_call(
        functools.partial(_scatter_body, n_prompt_tiles=npt),
        grid=(npt + nst,),
        in_specs=[
            smem(), smem(),
            pl.BlockSpec((tm, D_MODEL), lambda i: (jnp.minimum(i, npt - 1), 0)),
            pl.BlockSpec((tm, D_MODEL), lambda i: (jnp.maximum(i - npt, 0), 0)),
        ],
        out_specs=pl.BlockSpec(memory_space=pl.ANY),
        out_shape=jax.ShapeDtypeStruct((n_sorted, D_MODEL), F32),
        scratch_shapes=[pltpu.SemaphoreType.DMA((2,))],
        compiler_params=_cparams("arbitrary", vmem=VMEM_LIMIT),
        name="moe_scatter",
    )(d1.reshape(-1, 1, tm), d2.reshape(-1, 1, tm), h_p, h_s)


def _gmm_body(tile_ref, exp_ref, lo_ref, hi_ref, first_ref, x_ref, wg_ref, wu_ref, wd_ref, o_ref):
    del tile_ref, exp_ref
    v = pl.program_id(0)
    c = pl.program_id(1)

    @pl.when((first_ref[v] == 1) & (c == 0))
    def _():
        o_ref[...] = jnp.zeros(o_ref.shape, F32)

    lo = lo_ref[v]
    hi = hi_ref[v]

    @pl.when(hi > lo)
    def _():
        x = x_ref[...]
        a = _silu(_dot(x, wg_ref[0])) * _dot(x, wu_ref[0])
        y = _dot(a, wd_ref[0])
        rows = lax.broadcasted_iota(jnp.int32, (MOE_TM, 1), 0)
        o_ref[...] += jnp.where((rows >= lo) & (rows < hi), y, 0.0)


def _gmm(meta, xs, w_gu, w_down):
    tile, expert, lo, hi, first = meta
    n_visits = tile.shape[0]
    grid_spec = pltpu.PrefetchScalarGridSpec(
        num_scalar_prefetch=5,
        grid=(n_visits, N_FF_CHUNKS),
        in_specs=[
            pl.BlockSpec((MOE_TM, D_MODEL), lambda v, c, t, e, *_: (t[v], 0)),
            pl.BlockSpec((1, D_MODEL, FF_CHUNK), lambda v, c, t, e, *_: (e[v], 0, c)),
            pl.BlockSpec((1, D_MODEL, FF_CHUNK), lambda v, c, t, e, *_: (e[v], 0, N_FF_CHUNKS + c)),
            pl.BlockSpec((1, FF_CHUNK, D_MODEL), lambda v, c, t, e, *_: (e[v], c, 0)),
        ],
        out_specs=pl.BlockSpec((MOE_TM, D_MODEL), lambda v, c, t, e, *_: (t[v], 0)),
    )
    return pl.pallas_call(
        _gmm_body,
        grid_spec=grid_spec,
        out_shape=jax.ShapeDtypeStruct(xs.shape, F32),
        compiler_params=_cparams("arbitrary", "arbitrary", vmem=VMEM_LIMIT),
        name="moe_gmm",
    )(tile, expert, lo, hi, first, xs, w_gu, w_gu, w_down)


def _gmm_meta(counts, n_sorted):
    n_tiles = n_sorted // MOE_TM
    n_visits = n_tiles + N_EXPERTS - 1
    ends = jnp.cumsum(counts)
    starts = ends - counts
    first_tile = starts // MOE_TM
    last_tile = jnp.maximum(ends - 1, 0) // MOE_TM
    nv = jnp.where(counts > 0, last_tile - first_tile + 1, 0)
    cv = jnp.cumsum(nv)
    v = jnp.arange(n_visits, dtype=jnp.int32)
    total = cv[-1]
    valid = v < total
    vc = jnp.minimum(v, total - 1)
    expert = jnp.sum((cv[None, :] <= vc[:, None]).astype(jnp.int32), axis=1)
    tile = first_tile[expert] + (vc - (cv[expert] - nv[expert]))
    lo = jnp.maximum(starts[expert], tile * MOE_TM) - tile * MOE_TM
    hi = jnp.minimum(ends[expert], (tile + 1) * MOE_TM) - tile * MOE_TM
    lo = jnp.where(valid, lo, 0)
    hi = jnp.where(valid, hi, 0)
    prev_tile = jnp.concatenate([jnp.full((1,), -1, jnp.int32), tile[:-1]])
    first = (valid & (tile != prev_tile)).astype(jnp.int32)
    as_i32 = lambda a: a.astype(jnp.int32)
    return as_i32(tile), as_i32(expert), as_i32(lo), as_i32(hi), first


def _combine_body(d1_ref, d2_ref, x_ref, g_ref, os_ref, y_ref, buf_ref, sem, *, tm, n_tiles):
    i = pl.program_id(0)
    slot = i % 2

    def issue(tile, slot_):
        def body(r, carry):
            _row_copy(os_ref, d1_ref[tile, r], buf_ref.at[slot_, 0], r, sem.at[slot_, 0]).start()
            _row_copy(os_ref, d2_ref[tile, r], buf_ref.at[slot_, 1], r, sem.at[slot_, 1]).start()
            return carry
        lax.fori_loop(0, tm, body, 0)

    @pl.when(i == 0)
    def _():
        issue(0, 0)

    @pl.when(i + 1 < n_tiles)
    def _():
        issue(i + 1, 1 - slot)

    def wait(r, carry):
        _row_copy(os_ref, d1_ref[i, r], buf_ref.at[slot, 0], r, sem.at[slot, 0]).wait()
        _row_copy(os_ref, d2_ref[i, r], buf_ref.at[slot, 1], r, sem.at[slot, 1]).wait()
        return carry
    lax.fori_loop(0, tm, wait, 0)

    g = g_ref[...]
    y_ref[...] = x_ref[...] + g[:, 0:1] * buf_ref[slot, 0] + g[:, 1:2] * buf_ref[slot, 1]


def _combine(d1, d2, x, gates, o_sorted, tm):
    t = x.shape[0]
    n_tiles = t // tm
    grid_spec = pltpu.PrefetchScalarGridSpec(
        num_scalar_prefetch=2,
        grid=(n_tiles,),
        in_specs=[
            pl.BlockSpec((tm, D_MODEL), lambda i, *_: (i, 0)),
            pl.BlockSpec((tm, 2), lambda i, *_: (i, 0)),
            pl.BlockSpec(memory_space=pl.ANY),
        ],
        out_specs=pl.BlockSpec((tm, D_MODEL), lambda i, *_: (i, 0)),
        scratch_shapes=[pltpu.VMEM((2, 2, tm, D_MODEL), F32), pltpu.SemaphoreType.DMA((2, 2))],
    )
    return pl.pallas_call(
        functools.partial(_combine_body, tm=tm, n_tiles=n_tiles),
        grid_spec=grid_spec,
        out_shape=jax.ShapeDtypeStruct((t, D_MODEL), F32),
        compiler_params=_cparams("arbitrary", vmem=VMEM_LIMIT),
        name="moe_combine",
    )(d1.reshape(n_tiles, tm), d2.reshape(n_tiles, tm), x, gates, o_sorted)


def _block_diag(w_grp):
    g, n, _ = w_grp.shape
    out = jnp.zeros((g * n, g * n), w_grp.dtype)
    for i in range(g):
        out = out.at[i * n:(i + 1) * n, i * n:(i + 1) * n].set(w_grp[i])
    return out


def kernel(x_prompt, x_sample, cache_mem_k, cache_mem_v, state_pool, cache_fox_k, cache_fox_v, cache_fox_logf, page_table, mem_prompt, norm_mix_g, norm_ffn_g, norm_mem_g, w_mem_kv, mem_q_norm_g, mem_k_norm_g, w_out, w_in_pool, w_pool_group, pool_scale, w_in_fox, b_forget, fox_q_norm_g, fox_k_norm_g, w_ffn_gu, w_ffn_down, w_router, w_exp_gu, w_exp_down):
    b, s, d = x_prompt.shape
    bs, ls, _ = x_sample.shape
    tp, ts = b * s, bs * ls
    tm_p, tm_s = 512, ts

    xp = x_prompt.reshape(tp, d)
    xs = x_sample.reshape(ts, d)

    mem_kt_p, mem_vt_p = _memkv(mem_prompt, norm_mem_g, w_mem_kv, mem_k_norm_g)
    mem_kt_s = cache_mem_k.transpose(0, 1, 3, 4, 2).reshape(2, bs, MEM_W, MEM_LEN)
    mem_vt_s = cache_mem_v.transpose(0, 1, 3, 4, 2).reshape(2, bs, MEM_W, MEM_LEN)

    def mem_attend_both(qm_p, qm_s, layer):
        mo_p = _mem_attend(qm_p.reshape(b, s, MEM_W), mem_kt_p[layer], mem_vt_p[layer],
                           mem_q_norm_g[layer], 512).reshape(tp, MEM_W)
        qs8 = jnp.pad(qm_s.reshape(bs, ls, MEM_W), ((0, 0), (0, SAMPLE_ROWS - ls), (0, 0)))
        mo_s = _mem_attend(qs8, mem_kt_s[layer], mem_vt_s[layer], mem_q_norm_g[layer], SAMPLE_ROWS)
        return mo_p, mo_s[:, :ls].reshape(ts, MEM_W)

    g_mix0 = norm_mix_g[0][None, :]
    up, qmp = _pool_proj(xp, g_mix0, w_in_pool[0], tm_p)
    us, qms = _pool_proj(xs, g_mix0, w_in_pool[0], tm_s)
    w_bd = _block_diag(w_pool_group[0])
    pscale = pool_scale[0][None, :]
    mix_p = _pool_prompt(up.reshape(b, s, MIX_W), w_bd, pscale, 512).reshape(tp, MIX_W)
    us_t = us.reshape(bs, ls, MIX_W).transpose(1, 0, 2)
    mix_s_t, new_state_t = _pool_sample(state_pool[0].transpose(1, 0, 2), us_t, w_bd, pscale)
    mix_s = mix_s_t.transpose(1, 0, 2).reshape(ts, MIX_W)
    pool_p = up.reshape(b, s, MIX_W)[:, s - POOL_STATE:][None]
    pool_s = new_state_t.transpose(1, 0, 2)[None]

    mo_p, mo_s = mem_attend_both(qmp, qms, 0)
    xp = _out_proj(xp, mix_p, mo_p, w_out[0], tm_p)
    xs = _out_proj(xs, mix_s, mo_s, w_out[0], tm_s)
    g_ffn0 = norm_ffn_g[0][None, :]
    xp = _ffn(xp, g_ffn0, w_ffn_gu[0], w_ffn_down[0], 1024)
    xs = _ffn(xs, g_ffn0, w_ffn_gu[0], w_ffn_down[0], tm_s)

    w_in = w_in_fox[0]
    wq = w_in[:, :MIX_W]
    wkt = w_in[:, MIX_W:2 * MIX_W].T
    wvt = w_in[:, 2 * MIX_W:3 * MIX_W].T
    wft = jnp.pad(w_in[:, 3 * MIX_W:3 * MIX_W + FOX_HEADS].T, ((0, _FL_ROWS - FOX_HEADS), (0, 0)))
    wqm = w_in[:, 3 * MIX_W + FOX_HEADS:]
    bf_col = jnp.pad(b_forget[0], (0, _FL_ROWS - FOX_HEADS))[:, None]
    qg_t = jnp.tile(fox_q_norm_g[0], FOX_HEADS)[None, :]
    kg_col = fox_k_norm_g[0][:, None]
    g_mix1 = norm_mix_g[1][None, :]
    proj = functools.partial(_fox_proj, g=g_mix1, wq=wq, wqm=wqm, wkt=wkt, wvt=wvt, wft=wft,
                             bf_col=bf_col, qg_t=qg_t, kg_col=kg_col)
    qp, ktp, vtp, lftp, qmp = proj(xp, tm=tm_p, n_seq=b, rows_out=False)
    qs, _, _, lfts, qms, kns, vns = proj(xs, tm=tm_s, n_seq=1, rows_out=True)

    ct = _cumsum_tokens(lftp, b, 512)[:FOX_HEADS]
    c_col = ct.reshape(FOX_HEADS // 2, 2, b, s).transpose(2, 0, 3, 1)
    c_row = ct.reshape(FOX_HEADS // 2, 2, tp)
    mix_p = _fox_flash(qp.reshape(b, s, MIX_W), ktp, vtp, c_col, c_row, 512, 512).reshape(tp, MIX_W)

    pad_rows = ((0, 0), (0, SAMPLE_ROWS - ls), (0, 0))
    q8 = jnp.pad(qs.reshape(bs, ls, MIX_W), pad_rows)
    kn8 = jnp.pad(kns.reshape(bs, ls, MIX_W), pad_rows)
    vn8 = jnp.pad(vns.reshape(bs, ls, MIX_W), pad_rows)
    lfn = jnp.pad(lfts.reshape(_FL_ROWS, bs, ls).transpose(1, 0, 2),
                  ((0, 0), (0, 0), (0, SAMPLE_ROWS - ls)))
    kt_pages = cache_fox_k[0].transpose(0, 2, 3, 1)
    vt_pages = cache_fox_v[0].transpose(0, 2, 3, 1)
    lf_pages = jnp.pad(cache_fox_logf[0].transpose(0, 2, 1), ((0, 0), (0, _FL_ROWS - FOX_HEADS), (0, 0)))
    mix_s = _fox_decode(page_table, kt_pages, vt_pages, lf_pages, q8, kn8, vn8, lfn, ls)
    mix_s = mix_s[:, :ls].reshape(ts, MIX_W)

    mo_p, mo_s = mem_attend_both(qmp, qms, 1)
    xp = _out_proj(xp, mix_p, mo_p, w_out[1], tm_p)
    xs = _out_proj(xs, mix_s, mo_s, w_out[1], tm_s)

    g_ffn1 = norm_ffn_g[1][None, :]
    wrt = w_router[0].T
    hp, rp, cnt_p = _router(xp, g_ffn1, wrt, tm_p)
    hs, rs, cnt_s = _router(xs, g_ffn1, wrt, tm_s)
    cnt_p = cnt_p[:, 0].astype(jnp.int32)
    cnt_s = cnt_s[:, 0].astype(jnp.int32)
    counts = cnt_p + cnt_s
    offsets = jnp.cumsum(counts) - counts
    n_sorted = 2 * (tp + ts)

    def dests(r, base):
        i1, i2 = r[0].astype(jnp.int32), r[1].astype(jnp.int32)
        return base[i1] + r[2].astype(jnp.int32), base[i2] + r[3].astype(jnp.int32)

    d1p, d2p = dests(rp, offsets)
    d1s, d2s = dests(rs, offsets + cnt_p)
    x_sorted = _scatter_rows(jnp.concatenate([d1p, d1s]), jnp.concatenate([d2p, d2s]), hp, hs, n_sorted)
    o_sorted = _gmm(_gmm_meta(counts, n_sorted), x_sorted, w_exp_gu[0], w_exp_down[0])
    yp = _combine(d1p, d2p, xp, rp[4:6].T, o_sorted, 256)
    ys = _combine(d1s, d2s, xs, rs[4:6].T, o_sorted, ts)

    def heads_t(a_t, n_b, n_h):
        return a_t.reshape(n_b, n_h, HEAD_DIM, a_t.shape[-1]).transpose(0, 3, 1, 2)

    mem_k_p = jnp.stack([heads_t(mem_kt_p[i], b, MEM_HEADS) for i in range(2)])
    mem_v_p = jnp.stack([heads_t(mem_vt_p[i], b, MEM_HEADS) for i in range(2)])
    fk_p = heads_t(ktp, b, FOX_HEADS)[None]
    fv_p = heads_t(vtp, b, FOX_HEADS)[None]
    fl_p = lftp[:FOX_HEADS].reshape(FOX_HEADS, b, s).transpose(1, 2, 0)[None]
    fk_s = kns.reshape(1, bs, ls, FOX_HEADS, HEAD_DIM)
    fv_s = vns.reshape(1, bs, ls, FOX_HEADS, HEAD_DIM)
    fl_s = lfts[:FOX_HEADS].reshape(FOX_HEADS, bs, ls).transpose(1, 2, 0)[None]
    return (yp.reshape(b, s, d), ys.reshape(bs, ls, d), mem_k_p, mem_v_p, pool_p, pool_s,
            fk_p, fv_p, fl_p, fk_s, fv_s, fl_s)
```

```python
import functools
import math

import jax
import jax.numpy as jnp
from jax import lax
from jax.experimental import pallas as pl
from jax.experimental.pallas import tpu as pltpu

F32 = jnp.float32
BF16 = jnp.bfloat16

D_MODEL = 1024
HEAD_DIM = 64
MEM_LEN = 256
MEM_HEADS = 4
MEM_W = MEM_HEADS * HEAD_DIM
MIX_W = D_MODEL - MEM_W
FOX_HEADS = MIX_W // HEAD_DIM
POOL_WINDOWS = (2, 4, 8, 16)
POOL_GW = MIX_W // len(POOL_WINDOWS)
POOL_STATE = max(POOL_WINDOWS) - 1
D_FF = 2816
N_EXPERTS = 8
PAGE_SIZE = 128
EPS = 1e-6
SCALE = HEAD_DIM ** -0.5
LOG2E = math.log2(math.e)

LANES = 128
SUBLANES = 8
MXU_N = 256
VMEM_LIMIT = 56 * 1024 * 1024

FF_CHUNK = MXU_N
N_FF_CHUNKS = D_FF // FF_CHUNK
MOE_TM = 768
PAGES_PER_STEP = 8
SAMPLE_ROWS = 8
DMA_UNROLL = 8


def _cparams(*sem, vmem=None):
    return pltpu.CompilerParams(dimension_semantics=sem, vmem_limit_bytes=vmem)


def _dot(a, b):
    return jnp.dot(a, b, preferred_element_type=F32)


def _dot_nt(a, b):
    return lax.dot_general(a, b, (((1,), (1,)), ((), ())), preferred_element_type=F32)


def _rms_rows(x, g):
    return x * lax.rsqrt(jnp.mean(x * x, axis=-1, keepdims=True) + EPS) * g


def _split3(x):
    hi = x.astype(BF16)
    r = x - hi.astype(F32)
    mid = r.astype(BF16)
    lo = (r - mid.astype(F32)).astype(BF16)
    return hi, mid, lo


def _head_sumsq_lanes(x):
    r = lax.broadcasted_iota(jnp.int32, (LANES, LANES), 0) // HEAD_DIM
    c = lax.broadcasted_iota(jnp.int32, (LANES, LANES), 1) // HEAD_DIM
    ones_bd = (r == c).astype(BF16)
    xx = x * x
    hi = xx.astype(BF16)
    lo = (xx - hi.astype(F32)).astype(BF16)
    parts = []
    for j in range(x.shape[1] // LANES):
        sl = slice(j * LANES, (j + 1) * LANES)
        parts.append(_dot(hi[:, sl], ones_bd) + _dot(lo[:, sl], ones_bd))
    return jnp.concatenate(parts, axis=1)


def _head_rms_lanes(x, g_tiled):
    ssq = _head_sumsq_lanes(x)
    return x * lax.rsqrt(ssq * (1.0 / HEAD_DIM) + EPS) * g_tiled


def _head_rms_rows_t(xt, g_col, n_heads):
    outs = []
    for h in range(n_heads):
        blk = xt[h * HEAD_DIM:(h + 1) * HEAD_DIM, :]
        ms = jnp.mean(blk * blk, axis=0, keepdims=True)
        outs.append(blk * lax.rsqrt(ms + EPS) * g_col)
    return outs


def _log_sigmoid(x):
    return jnp.minimum(x, 0.0) - jnp.log1p(jnp.exp(-jnp.abs(x)))


def _lane_head(shape, axis):
    return lax.broadcasted_iota(jnp.int32, shape, axis) // HEAD_DIM


def _memkv_body(mem_ref, gm_ref, w_ref, kg_ref, kt_ref, vt_ref):
    h = _rms_rows(mem_ref[0], gm_ref[0])
    z = _dot(h, w_ref[0])
    kt = z[:, :MEM_W].T
    vt_ref[0, 0] = z[:, MEM_W:].T
    pieces = _head_rms_rows_t(kt, kg_ref[0], MEM_HEADS)
    for h_i, p in enumerate(pieces):
        kt_ref[0, 0, h_i * HEAD_DIM:(h_i + 1) * HEAD_DIM, :] = p


def _memkv(mem, g_mem, w_kv, kn_g):
    depth, batch = w_kv.shape[0], mem.shape[0]
    out = jax.ShapeDtypeStruct((depth, batch, MEM_W, MEM_LEN), F32)
    return pl.pallas_call(
        _memkv_body,
        grid=(depth, batch),
        in_specs=[
            pl.BlockSpec((1, MEM_LEN, D_MODEL), lambda i, b: (b, 0, 0)),
            pl.BlockSpec((1, 1, D_MODEL), lambda i, b: (i, 0, 0)),
            pl.BlockSpec((1, D_MODEL, 2 * MEM_W), lambda i, b: (i, 0, 0)),
            pl.BlockSpec((1, HEAD_DIM, 1), lambda i, b: (i, 0, 0)),
        ],
        out_specs=[pl.BlockSpec((1, 1, MEM_W, MEM_LEN), lambda i, b: (i, b, 0, 0))] * 2,
        out_shape=[out, out],
        compiler_params=_cparams("arbitrary", "arbitrary"),
        name="memkv",
    )(mem, g_mem[:, None, :], w_kv, kn_g[:, :, None])


def _pool_proj_body(x_ref, g_ref, w_ref, u_ref, qm_ref):
    h = _rms_rows(x_ref[...], g_ref[...])
    u_ref[...] = _dot(h, w_ref[:, :MIX_W])
    qm_ref[...] = _dot(h, w_ref[:, MIX_W:])


def _pool_proj(x, g, w, tm):
    t = x.shape[0]
    return pl.pallas_call(
        _pool_proj_body,
        grid=(t // tm,),
        in_specs=[
            pl.BlockSpec((tm, D_MODEL), lambda i: (i, 0)),
            pl.BlockSpec((1, D_MODEL), lambda i: (0, 0)),
            pl.BlockSpec((D_MODEL, D_MODEL), lambda i: (0, 0)),
        ],
        out_specs=[pl.BlockSpec((tm, MIX_W), lambda i: (i, 0)),
                   pl.BlockSpec((tm, MEM_W), lambda i: (i, 0))],
        out_shape=[jax.ShapeDtypeStruct((t, MIX_W), F32), jax.ShapeDtypeStruct((t, MEM_W), F32)],
        compiler_params=_cparams("arbitrary", vmem=VMEM_LIMIT),
        name="pool_proj",
    )(x, g, w)


def _window_of_lane(shape):
    lane = lax.broadcasted_iota(jnp.int32, shape, len(shape) - 1)
    return jnp.where(lane < POOL_GW, POOL_WINDOWS[0],
                     jnp.where(lane < 2 * POOL_GW, POOL_WINDOWS[1],
                               jnp.where(lane < 3 * POOL_GW, POOL_WINDOWS[2], POOL_WINDOWS[3])))


def _pool_select(s2, s4, s8, s16, shape):
    lane = lax.broadcasted_iota(jnp.int32, shape, len(shape) - 1)
    return jnp.where(lane < POOL_GW, s2,
                     jnp.where(lane < 2 * POOL_GW, s4, jnp.where(lane < 3 * POOL_GW, s8, s16)))


_POOL_PAD = SUBLANES
_POOL_HALO = 2 * SUBLANES
_POOL_BASE = _POOL_PAD + _POOL_HALO


def _pool_prompt_body(u_ref, w_ref, sc_ref, o_ref, e_ref, s2_ref, s4_ref, s8_ref, *, tl):
    li = pl.program_id(1)
    n = _POOL_HALO + tl

    @pl.when(li == 0)
    def _():
        e_ref[0:_POOL_BASE, :] = jnp.zeros((_POOL_BASE, MIX_W), F32)
        s2_ref[0:_POOL_PAD, :] = jnp.zeros((_POOL_PAD, MIX_W), F32)
        s4_ref[0:_POOL_PAD, :] = jnp.zeros((_POOL_PAD, MIX_W), F32)
        s8_ref[0:_POOL_PAD, :] = jnp.zeros((_POOL_PAD, MIX_W), F32)

    u = u_ref[0]
    e_ref[_POOL_BASE:_POOL_BASE + tl, :] = u
    s2 = e_ref[_POOL_PAD:_POOL_PAD + n, :] + e_ref[_POOL_PAD - 1:_POOL_PAD - 1 + n, :]
    s2_ref[_POOL_PAD:_POOL_PAD + n, :] = s2
    s4 = s2 + s2_ref[_POOL_PAD - 2:_POOL_PAD - 2 + n, :]
    s4_ref[_POOL_PAD:_POOL_PAD + n, :] = s4
    s8 = s4 + s4_ref[_POOL_PAD - 4:_POOL_PAD - 4 + n, :]
    s8_ref[_POOL_PAD:_POOL_PAD + n, :] = s8
    s16 = s8[_POOL_HALO:, :] + s8_ref[_POOL_BASE - 8:_POOL_BASE - 8 + tl, :]
    shape = (tl, MIX_W)
    ssel = _pool_select(s2[_POOL_HALO:, :], s4[_POOL_HALO:, :], s8[_POOL_HALO:, :], s16, shape)
    pos = li * tl + lax.broadcasted_iota(jnp.int32, shape, 0)
    cnt = jnp.minimum(_window_of_lane(shape), pos + 1).astype(F32)
    diff = ssel / cnt - u
    o_ref[0] = _dot(diff, w_ref[...]) * sc_ref[...]
    e_ref[_POOL_PAD:_POOL_BASE, :] = e_ref[_POOL_PAD + tl:_POOL_BASE + tl, :]


def _pool_prompt(u, w_bd, scale, tl):
    b, s, _ = u.shape
    rows = _POOL_BASE + tl
    return pl.pallas_call(
        functools.partial(_pool_prompt_body, tl=tl),
        grid=(b, s // tl),
        in_specs=[
            pl.BlockSpec((1, tl, MIX_W), lambda bi, li: (bi, li, 0)),
            pl.BlockSpec((MIX_W, MIX_W), lambda bi, li: (0, 0)),
            pl.BlockSpec((1, MIX_W), lambda bi, li: (0, 0)),
        ],
        out_specs=pl.BlockSpec((1, tl, MIX_W), lambda bi, li: (bi, li, 0)),
        out_shape=jax.ShapeDtypeStruct((b, s, MIX_W), F32),
        scratch_shapes=[pltpu.VMEM((rows, MIX_W), F32)] * 4,
        compiler_params=_cparams("arbitrary", "arbitrary", vmem=VMEM_LIMIT),
        name="pool_prompt",
    )(u, w_bd, scale)


def _pool_sample_body(st_ref, u_ref, w_ref, sc_ref, o_ref, ns_ref, *, n_new):
    def ext(j):
        return st_ref[j] if j < POOL_STATE else u_ref[j - POOL_STATE]

    for l in range(n_new):
        r = POOL_STATE + l
        s2 = ext(r) + ext(r - 1)
        s4 = s2 + ext(r - 2) + ext(r - 3)
        s8 = s4
        for j in range(4, 8):
            s8 = s8 + ext(r - j)
        s16 = s8
        for j in range(8, 16):
            s16 = s16 + ext(r - j)
        shape = s2.shape
        cnt = _window_of_lane(shape).astype(F32)
        diff = _pool_select(s2, s4, s8, s16, shape) / cnt - ext(r)
        o_ref[l] = _dot(diff, w_ref[...]) * sc_ref[...]
    for j in range(POOL_STATE):
        ns_ref[j] = ext(j + n_new)


def _pool_sample(state_t, u_t, w_bd, scale):
    n_new, b, _ = u_t.shape
    return pl.pallas_call(
        functools.partial(_pool_sample_body, n_new=n_new),
        out_shape=[jax.ShapeDtypeStruct((n_new, b, MIX_W), F32),
                   jax.ShapeDtypeStruct((POOL_STATE, b, MIX_W), F32)],
        name="pool_sample",
    )(state_t, u_t, w_bd, scale)


def _mem_attend_body(q_ref, kt_ref, vt_ref, g_ref, o_ref):
    q = _head_rms_lanes(q_ref[0], g_ref[...]) * SCALE
    kt = kt_ref[0]
    vt = vt_ref[0]
    lh = _lane_head(q.shape, 1)
    out = jnp.zeros(q.shape, F32)
    for h in range(MEM_HEADS):
        s = _dot(jnp.where(lh == h, q, 0.0), kt)
        e = jnp.exp(s - jnp.max(s, axis=-1, keepdims=True))
        p = e / jnp.sum(e, axis=-1, keepdims=True)
        out = jnp.where(lh == h, _dot_nt(p, vt), out)
    o_ref[0] = out


def _mem_attend(qm, kt, vt, qn_g, tl):
    b, l, _ = qm.shape
    return pl.pallas_call(
        _mem_attend_body,
        grid=(b, l // tl),
        in_specs=[
            pl.BlockSpec((1, tl, MEM_W), lambda bi, li: (bi, li, 0)),
            pl.BlockSpec((1, MEM_W, MEM_LEN), lambda bi, li: (bi, 0, 0)),
            pl.BlockSpec((1, MEM_W, MEM_LEN), lambda bi, li: (bi, 0, 0)),
            pl.BlockSpec((1, MEM_W), lambda bi, li: (0, 0)),
        ],
        out_specs=pl.BlockSpec((1, tl, MEM_W), lambda bi, li: (bi, li, 0)),
        out_shape=jax.ShapeDtypeStruct((b, l, MEM_W), F32),
        compiler_params=_cparams("arbitrary", "arbitrary"),
        name="mem_attend",
    )(qm, kt, vt, jnp.tile(qn_g, MEM_HEADS)[None, :])


def _out_proj_body(x_ref, mix_ref, mem_ref, w_ref, o_ref, *, mix_transposed):
    mix = mix_ref[0].T if mix_transposed else mix_ref[...]
    o_ref[...] = x_ref[...] + _dot(mix, w_ref[:MIX_W, :]) + _dot(mem_ref[...], w_ref[MIX_W:, :])


def _out_proj(x, mix, mem, w, tm, mix_transposed=False):
    t = x.shape[0]
    if mix_transposed:
        per_seq = mix.shape[2] // tm
        mix_spec = pl.BlockSpec((1, MIX_W, tm), lambda i: (i // per_seq, 0, i % per_seq))
    else:
        mix_spec = pl.BlockSpec((tm, MIX_W), lambda i: (i, 0))
    return pl.pallas_call(
        functools.partial(_out_proj_body, mix_transposed=mix_transposed),
        grid=(t // tm,),
        in_specs=[
            pl.BlockSpec((tm, D_MODEL), lambda i: (i, 0)),
            mix_spec,
            pl.BlockSpec((tm, MEM_W), lambda i: (i, 0)),
            pl.BlockSpec((D_MODEL, D_MODEL), lambda i: (0, 0)),
        ],
        out_specs=pl.BlockSpec((tm, D_MODEL), lambda i: (i, 0)),
        out_shape=jax.ShapeDtypeStruct((t, D_MODEL), F32),
        compiler_params=_cparams("arbitrary", vmem=VMEM_LIMIT),
        name="out_proj",
    )(x, mix, mem, w)


def _silu(g):
    return g / (1.0 + jnp.exp(-g))


def _ffn_body(x_ref, g_ref, wg_ref, wu_ref, wd_ref, o_ref, h_ref):
    c = pl.program_id(1)

    @pl.when(c == 0)
    def _():
        x = x_ref[...]
        h_ref[...] = _rms_rows(x, g_ref[...])
        o_ref[...] = x

    h = h_ref[...]
    a = _silu(_dot(h, wg_ref[...])) * _dot(h, wu_ref[...])
    o_ref[...] += _dot(a, wd_ref[...])


def _ffn(x, g, w_gu, w_down, tm):
    t = x.shape[0]
    return pl.pallas_call(
        _ffn_body,
        grid=(t // tm, N_FF_CHUNKS),
        in_specs=[
            pl.BlockSpec((tm, D_MODEL), lambda i, c: (i, 0)),
            pl.BlockSpec((1, D_MODEL), lambda i, c: (0, 0)),
            pl.BlockSpec((D_MODEL, FF_CHUNK), lambda i, c: (0, c)),
            pl.BlockSpec((D_MODEL, FF_CHUNK), lambda i, c: (0, N_FF_CHUNKS + c)),
            pl.BlockSpec((FF_CHUNK, D_MODEL), lambda i, c: (c, 0)),
        ],
        out_specs=pl.BlockSpec((tm, D_MODEL), lambda i, c: (i, 0)),
        out_shape=jax.ShapeDtypeStruct((t, D_MODEL), F32),
        scratch_shapes=[pltpu.VMEM((tm, D_MODEL), F32)],
        compiler_params=_cparams("arbitrary", "arbitrary", vmem=VMEM_LIMIT),
        name="ffn",
    )(x, g, w_gu, w_gu, w_down)


_FL_ROWS = 2 * SUBLANES
_AUG0 = HEAD_DIM


def _aug_pieces(c_col):
    hi, mid, lo = _split3(c_col)
    bc = lambda a: jnp.broadcast_to(a.astype(F32), (c_col.shape[0], LANES))
    return bc(hi), bc(mid), bc(lo)


def _fox_proj_prompt_body(x_ref, g_ref, wq_ref, wk_ref, wv_ref, wqm_ref, wf_ref, bf_ref, qg_ref, kg_ref,
                          qp_ref, kp_ref, kt_ref, vt_ref, lft_ref, qm_ref, carry_ref, *, tm, per_seq):
    @pl.when(pl.program_id(0) % per_seq == 0)
    def _():
        carry_ref[...] = jnp.zeros(carry_ref.shape, F32)

    h = _rms_rows(x_ref[...], g_ref[...])
    q = _head_rms_lanes(_dot(h, wq_ref[...]), qg_ref[...]) * (SCALE * LOG2E)
    k = _head_rms_lanes(_dot(h, wk_ref[...]), kg_ref[...])
    v = _dot(h, wv_ref[...])
    qm_ref[...] = _dot(h, wqm_ref[...])
    kt_ref[0] = k.T
    vt_ref[0] = v.T
    lane = lax.broadcasted_iota(jnp.int32, (tm, LANES), 1)
    lf = jnp.where(lane < FOX_HEADS, _log_sigmoid(_dot(h, wf_ref[...]) + bf_ref[...]), 0.0)
    lft_ref[...] = lf.T[:_FL_ROWS, :]
    r = lax.broadcasted_iota(jnp.int32, (tm, tm), 0)
    c = lax.broadcasted_iota(jnp.int32, (tm, tm), 1)
    lower = (c <= r).astype(BF16)
    hi, mid, lo = _split3(lf)
    csum = _dot(lower, hi) + _dot(lower, mid) + _dot(lower, lo) + carry_ref[0:1, :]
    carry_ref[...] = jnp.broadcast_to(csum[tm - 1:tm, :], carry_ref.shape)
    c2 = csum * LOG2E
    one = jnp.ones((tm, LANES), F32)
    zero = jnp.zeros((tm, LANES), F32)
    for hd in range(FOX_HEADS):
        chi, cmid, clo = _aug_pieces(c2[:, hd:hd + 1])
        aug_q = jnp.where(lane == _AUG0, chi, jnp.where(lane == _AUG0 + 1, cmid, jnp.where(
            lane == _AUG0 + 2, clo, jnp.where(lane < _AUG0 + 6, one, zero))))
        aug_k = jnp.where(lane < _AUG0 + 3, one, jnp.where(lane == _AUG0 + 3, -chi, jnp.where(
            lane == _AUG0 + 4, -cmid, jnp.where(lane == _AUG0 + 5, -clo, zero))))
        col = slice((hd // 2) * LANES, (hd // 2 + 1) * LANES)
        qc, kc = q[:, col], k[:, col]
        if hd % 2:
            qc = pltpu.roll(qc, HEAD_DIM, 1)
            kc = pltpu.roll(kc, HEAD_DIM, 1)
        qp_ref[0, hd] = jnp.where(lane < HEAD_DIM, qc, aug_q)
        kp_ref[0, hd] = jnp.where(lane < HEAD_DIM, kc, aug_k)


def _fox_proj_prompt(x, g, wq, wk, wv, wqm, wf, bf_row, qg_t, kg_t, tm, n_seq):
    t = x.shape[0]
    seq = t // n_seq
    per_seq = seq // tm

    def const(shape):
        return pl.BlockSpec(shape, lambda i: tuple(0 for _ in shape))

    heads_spec = pl.BlockSpec((1, FOX_HEADS, tm, LANES), lambda i: (i // per_seq, 0, i % per_seq, 0))
    t_spec = pl.BlockSpec((1, MIX_W, tm), lambda i: (i // per_seq, 0, i % per_seq))
    return pl.pallas_call(
        functools.partial(_fox_proj_prompt_body, tm=tm, per_seq=per_seq),
        grid=(t // tm,),
        in_specs=[
            pl.BlockSpec((tm, D_MODEL), lambda i: (i, 0)),
            const((1, D_MODEL)),
            const((D_MODEL, MIX_W)), const((D_MODEL, MIX_W)), const((D_MODEL, MIX_W)),
            const((D_MODEL, MEM_W)), const((D_MODEL, LANES)),
            const((1, LANES)), const((1, MIX_W)), const((1, MIX_W)),
        ],
        out_specs=[heads_spec, heads_spec, t_spec, t_spec,
                   pl.BlockSpec((_FL_ROWS, tm), lambda i: (0, i)),
                   pl.BlockSpec((tm, MEM_W), lambda i: (i, 0))],
        out_shape=[
            jax.ShapeDtypeStruct((n_seq, FOX_HEADS, seq, LANES), F32),
            jax.ShapeDtypeStruct((n_seq, FOX_HEADS, seq, LANES), F32),
            jax.ShapeDtypeStruct((n_seq, MIX_W, seq), F32),
            jax.ShapeDtypeStruct((n_seq, MIX_W, seq), F32),
            jax.ShapeDtypeStruct((_FL_ROWS, t), F32),
            jax.ShapeDtypeStruct((t, MEM_W), F32),
        ],
        scratch_shapes=[pltpu.VMEM((SUBLANES, LANES), F32)],
        compiler_params=_cparams("arbitrary", vmem=VMEM_LIMIT),
        name="fox_proj_prompt",
    )(x, g, wq, wk, wv, wqm, wf, bf_row, qg_t, kg_t)


def _fox_proj_sample_body(x_ref, g_ref, wq_ref, wk_ref, wv_ref, wqm_ref, wf_ref, bf_ref, qg_ref, kg_ref,
                          q_ref, k_ref, v_ref, lft_ref, qm_ref):
    h = _rms_rows(x_ref[...], g_ref[...])
    q_ref[...] = _head_rms_lanes(_dot(h, wq_ref[...]), qg_ref[...])
    k_ref[...] = _head_rms_lanes(_dot(h, wk_ref[...]), kg_ref[...])
    v_ref[...] = _dot(h, wv_ref[...])
    qm_ref[...] = _dot(h, wqm_ref[...])
    lane = lax.broadcasted_iota(jnp.int32, (x_ref.shape[0], LANES), 1)
    lf = jnp.where(lane < FOX_HEADS, _log_sigmoid(_dot(h, wf_ref[...]) + bf_ref[...]), 0.0)
    lft_ref[...] = lf.T[:_FL_ROWS, :]


def _fox_proj_sample(x, g, wq, wk, wv, wqm, wf, bf_row, qg_t, kg_t):
    t = x.shape[0]
    rows = jax.ShapeDtypeStruct((t, MIX_W), F32)
    return pl.pallas_call(
        _fox_proj_sample_body,
        out_shape=[rows, rows, rows, jax.ShapeDtypeStruct((_FL_ROWS, t), F32),
                   jax.ShapeDtypeStruct((t, MEM_W), F32)],
        compiler_params=_cparams(vmem=VMEM_LIMIT),
        name="fox_proj_sample",
    )(x, g, wq, wk, wv, wqm, wf, bf_row, qg_t, kg_t)


def _fox_flash_body(qi_ref, ki_ref, qp_ref, kp_ref, vt_ref, o_ref, m_ref, l_ref, acc_ref, *, blk):
    p = pl.program_id(2)
    qi = qi_ref[p]
    ki = ki_ref[p]

    @pl.when(ki == 0)
    def _():
        m_ref[...] = jnp.full(m_ref.shape, -jnp.inf, F32)
        l_ref[...] = jnp.zeros(l_ref.shape, F32)
        acc_ref[...] = jnp.zeros(acc_ref.shape, F32)

    def step(diagonal):
        vt = vt_ref[0]
        for hh in range(2):
            st = _dot_nt(kp_ref[0, hh], qp_ref[0, hh])
            if diagonal:
                key = lax.broadcasted_iota(jnp.int32, st.shape, 0)
                qry = lax.broadcasted_iota(jnp.int32, st.shape, 1)
                st = jnp.where(key <= qry, st, -jnp.inf)
            m_prev = m_ref[hh]
            m_new = jnp.maximum(m_prev, jnp.max(st, axis=0, keepdims=True))
            alpha = jnp.exp2(m_prev - m_new)
            pt = jnp.exp2(st - m_new)
            l_ref[hh] = alpha * l_ref[hh] + jnp.sum(pt, axis=0, keepdims=True)
            acc_ref[hh] = alpha * acc_ref[hh] + _dot(vt, pt)
            m_ref[hh] = m_new

    @pl.when(ki < qi)
    def _():
        step(False)

    @pl.when(ki == qi)
    def _():
        step(True)
        row = lax.broadcasted_iota(jnp.int32, (LANES, blk), 0)
        o_ref[0] = jnp.where(row < HEAD_DIM, acc_ref[0] / l_ref[0], acc_ref[1] / l_ref[1])


def _fox_flash(qp, kp, vt, blk):
    b, _, s, _ = qp.shape
    n = s // blk
    pairs = [(qi, ki) for qi in range(n) for ki in range(qi + 1)]
    qi_tab = jnp.asarray([p[0] for p in pairs], jnp.int32)
    ki_tab = jnp.asarray([p[1] for p in pairs], jnp.int32)
    grid_spec = pltpu.PrefetchScalarGridSpec(
        num_scalar_prefetch=2,
        grid=(b, FOX_HEADS // 2, len(pairs)),
        in_specs=[
            pl.BlockSpec((1, 2, blk, LANES), lambda bi, hp, p, qt, kt: (bi, hp, qt[p], 0)),
            pl.BlockSpec((1, 2, blk, LANES), lambda bi, hp, p, qt, kt: (bi, hp, kt[p], 0)),
            pl.BlockSpec((1, LANES, blk), lambda bi, hp, p, qt, kt: (bi, hp, kt[p])),
        ],
        out_specs=pl.BlockSpec((1, LANES, blk), lambda bi, hp, p, qt, kt: (bi, hp, qt[p])),
        scratch_shapes=[pltpu.VMEM((2, 1, blk), F32), pltpu.VMEM((2, 1, blk), F32),
                        pltpu.VMEM((2, LANES, blk), F32)],
    )
    return pl.pallas_call(
        functools.partial(_fox_flash_body, blk=blk),
        grid_spec=grid_spec,
        out_shape=jax.ShapeDtypeStruct((b, MIX_W, s), F32),
        compiler_params=_cparams("arbitrary", "arbitrary", "arbitrary", vmem=VMEM_LIMIT),
        name="fox_flash",
    )(qi_tab, ki_tab, qp, kp, vt)


_QROWS = FOX_HEADS * SAMPLE_ROWS
_CHUNK = PAGES_PER_STEP * PAGE_SIZE


def _expand_heads(x):
    n = x.shape[1]
    return jnp.broadcast_to(x[:FOX_HEADS, None, :], (FOX_HEADS, SAMPLE_ROWS, n)).reshape(_QROWS, n)


def _page_copies(pt_ref, kt_hbm, vt_hbm, lf_hbm, kt_buf, vt_buf, lf_buf, sem, g, slot, steps, n_pages):
    bi = g // steps
    first = n_pages - (g % steps + 1) * PAGES_PER_STEP
    copies = []
    for i in range(PAGES_PER_STEP):
        page = pt_ref[bi, first + i]
        lanes = pl.ds(i * PAGE_SIZE, PAGE_SIZE)
        copies.append(pltpu.make_async_copy(kt_hbm.at[page], kt_buf.at[slot, :, lanes], sem.at[slot, 0]))
        copies.append(pltpu.make_async_copy(vt_hbm.at[page], vt_buf.at[slot, :, lanes], sem.at[slot, 1]))
        copies.append(pltpu.make_async_copy(lf_hbm.at[:, page], lf_buf.at[slot, i, pl.ds(0, FOX_HEADS)],
                                            sem.at[slot, 2]))
    return copies


def _fox_decode_body(pt_ref, kt_hbm, vt_hbm, lf_hbm, q_ref, kn_ref, vn_ref, lfn_ref, o_ref,
                     kt_buf, vt_buf, lf_buf, sem, qbd_ref, m_ref, l_ref, acc_ref, carry_ref, crow_ref,
                     *, n_new, steps, n_pages):
    g = pl.program_id(0)
    n_steps = pl.num_programs(0)
    j = g % steps
    slot = g % 2
    copies = functools.partial(_page_copies, pt_ref, kt_hbm, vt_hbm, lf_hbm, kt_buf, vt_buf, lf_buf, sem,
                               steps=steps, n_pages=n_pages)
    row_l = lax.broadcasted_iota(jnp.int32, (_QROWS, 1), 0) % SAMPLE_ROWS

    @pl.when(g == 0)
    def _():
        lf_buf[...] = jnp.zeros(lf_buf.shape, F32)
        for cp in copies(g=g, slot=slot):
            cp.start()

    @pl.when(g + 1 < n_steps)
    def _():
        for cp in copies(g=g + 1, slot=1 - slot):
            cp.start()

    @pl.when(j == 0)
    def _():
        q = q_ref[0] * SCALE
        lh = _lane_head(q.shape, 1)
        for h in range(FOX_HEADS):
            qbd_ref[h * SAMPLE_ROWS:(h + 1) * SAMPLE_ROWS, :] = jnp.where(lh == h, q, 0.0)
        lfn = lfn_ref[0]
        lane = lax.broadcasted_iota(jnp.int32, lfn.shape, 1)
        c = jnp.zeros(lfn.shape, F32)
        for m in range(n_new):
            cm = jnp.sum(jnp.where(lane <= m, lfn, 0.0), axis=1, keepdims=True)
            c = jnp.where(lane == m, cm, c)
        c_q = _expand_heads(c)
        col = lax.broadcasted_iota(jnp.int32, c_q.shape, 1)
        crow = jnp.sum(jnp.where(col == row_l, c_q, 0.0), axis=1, keepdims=True)
        crow_ref[...] = crow
        s = _dot_nt(qbd_ref[...], kn_ref[0]) + (crow - c_q)
        valid = (col < n_new) & ((col <= row_l) | (row_l >= n_new))
        s = jnp.where(valid, s, -jnp.inf)
        m0 = jnp.max(s, axis=-1, keepdims=True)
        p = jnp.exp(s - m0)
        m_ref[...] = m0
        l_ref[...] = jnp.sum(p, axis=-1, keepdims=True)
        acc_ref[...] = _dot(p, vn_ref[0])
        carry_ref[...] = jnp.zeros(carry_ref.shape, F32)

    for cp in copies(g=g, slot=slot):
        cp.wait()

    r = lax.broadcasted_iota(jnp.int32, (PAGE_SIZE, PAGE_SIZE), 0)
    cc = lax.broadcasted_iota(jnp.int32, (PAGE_SIZE, PAGE_SIZE), 1)
    later = (r > cc).astype(BF16)
    lf3 = lf_buf[slot]
    tot = jnp.sum(lf3, axis=2, keepdims=True)
    hi, mid, lo = _split3(lf3.reshape(PAGES_PER_STEP * _FL_ROWS, PAGE_SIZE))
    d_in = (_dot(hi, later) + _dot(mid, later) + _dot(lo, later)).reshape(PAGES_PER_STEP, _FL_ROWS, PAGE_SIZE)
    after = carry_ref[:, 0:1]
    d_pages = [None] * PAGES_PER_STEP
    for i in reversed(range(PAGES_PER_STEP)):
        d_pages[i] = _expand_heads(d_in[i] + after)
        after = after + tot[i]
    carry_ref[...] = jnp.broadcast_to(after, carry_ref.shape)
    d = jnp.concatenate(d_pages, axis=1)

    vt = vt_buf[slot]
    s = _dot(qbd_ref[...], kt_buf[slot]) + crow_ref[...] + d
    m_prev = m_ref[...]
    m_new = jnp.maximum(m_prev, jnp.max(s, axis=-1, keepdims=True))
    alpha = jnp.exp(m_prev - m_new)
    p = jnp.exp(s - m_new)
    l_ref[...] = alpha * l_ref[...] + jnp.sum(p, axis=-1, keepdims=True)
    acc_ref[...] = alpha * acc_ref[...] + _dot_nt(p, vt)
    m_ref[...] = m_new

    @pl.when(j == steps - 1)
    def _():
        res = acc_ref[...] / l_ref[...]
        lh = _lane_head((SAMPLE_ROWS, MIX_W), 1)
        out = jnp.zeros((SAMPLE_ROWS, MIX_W), F32)
        for h in range(FOX_HEADS):
            out = jnp.where(lh == h, res[h * SAMPLE_ROWS:(h + 1) * SAMPLE_ROWS, :], out)
        o_ref[0] = out


def _fox_decode(page_table, kt_pages, vt_pages, lf_heads, q8, kn8, vn8, lfn, n_new):
    b, n_pages = page_table.shape
    steps = n_pages // PAGES_PER_STEP
    per_b = lambda g, pt: (g // steps, 0, 0)
    any_spec = pl.BlockSpec(memory_space=pl.ANY)
    grid_spec = pltpu.PrefetchScalarGridSpec(
        num_scalar_prefetch=1,
        grid=(b * steps,),
        in_specs=[any_spec, any_spec, any_spec,
                  pl.BlockSpec((1, SAMPLE_ROWS, MIX_W), per_b),
                  pl.BlockSpec((1, SAMPLE_ROWS, MIX_W), per_b),
                  pl.BlockSpec((1, SAMPLE_ROWS, MIX_W), per_b),
                  pl.BlockSpec((1, _FL_ROWS, SAMPLE_ROWS), per_b)],
        out_specs=pl.BlockSpec((1, SAMPLE_ROWS, MIX_W), per_b),
        scratch_shapes=[
            pltpu.VMEM((2, MIX_W, _CHUNK), F32),
            pltpu.VMEM((2, MIX_W, _CHUNK), F32),
            pltpu.VMEM((2, PAGES_PER_STEP, _FL_ROWS, PAGE_SIZE), F32),
            pltpu.SemaphoreType.DMA((2, 3)),
            pltpu.VMEM((_QROWS, MIX_W), F32),
            pltpu.VMEM((_QROWS, 1), F32), pltpu.VMEM((_QROWS, 1), F32),
            pltpu.VMEM((_QROWS, MIX_W), F32),
            pltpu.VMEM((_FL_ROWS, LANES), F32),
            pltpu.VMEM((_QROWS, 1), F32),
        ],
    )
    return pl.pallas_call(
        functools.partial(_fox_decode_body, n_new=n_new, steps=steps, n_pages=n_pages),
        grid_spec=grid_spec,
        out_shape=jax.ShapeDtypeStruct((b, SAMPLE_ROWS, MIX_W), F32),
        compiler_params=_cparams("arbitrary", vmem=VMEM_LIMIT),
        name="fox_decode",
    )(page_table, kt_pages, vt_pages, lf_heads, q8, kn8, vn8, lfn)


def _router_body(x_ref, g_ref, wrt_ref, h_ref, r_ref, cnt_ref, carry_ref, *, tm):
    @pl.when(pl.program_id(0) == 0)
    def _():
        carry_ref[...] = jnp.zeros(carry_ref.shape, F32)

    h = _rms_rows(x_ref[...], g_ref[...])
    h_ref[...] = h
    h_hi = h.astype(BF16)
    h_lo = (h - h_hi.astype(F32)).astype(BF16)
    w = wrt_ref[...]
    w_hi = w.astype(BF16)
    w_lo = (w - w_hi.astype(F32)).astype(BF16)
    lg = _dot_nt(w_hi, h_hi) + _dot_nt(w_hi, h_lo) + _dot_nt(w_lo, h_hi)
    idx = lax.broadcasted_iota(jnp.int32, lg.shape, 0)
    m1 = jnp.max(lg, axis=0, keepdims=True)
    i1 = jnp.min(jnp.where(lg == m1, idx, N_EXPERTS), axis=0, keepdims=True)
    sel1 = idx == i1
    lg2 = jnp.where(sel1, -jnp.inf, lg)
    m2 = jnp.max(lg2, axis=0, keepdims=True)
    i2 = jnp.min(jnp.where(lg2 == m2, idx, N_EXPERTS), axis=0, keepdims=True)
    sel2 = idx == i2
    e = jnp.exp(m2 - m1)
    g1 = 1.0 / (1.0 + e)
    g2 = e / (1.0 + e)
    assign = jnp.where(sel1 | sel2, 1.0, 0.0)
    r = lax.broadcasted_iota(jnp.int32, (tm, tm), 0)
    c = lax.broadcasted_iota(jnp.int32, (tm, tm), 1)
    before = (r < c).astype(BF16)
    rank = _dot(assign.astype(BF16), before) + carry_ref[:, 0:1]
    r1 = jnp.sum(jnp.where(sel1, rank, 0.0), axis=0, keepdims=True)
    r2 = jnp.sum(jnp.where(sel2, rank, 0.0), axis=0, keepdims=True)
    carry = carry_ref[...] + jnp.sum(assign, axis=1, keepdims=True)
    carry_ref[...] = carry
    cnt_ref[...] = carry
    rows = [i1.astype(F32), i2.astype(F32), r1, r2, g1, g2]
    out = jnp.zeros(lg.shape, F32)
    for k, v in enumerate(rows):
        out = jnp.where(idx == k, v, out)
    r_ref[...] = out


def _router(x, g, wrt, tm):
    t = x.shape[0]
    return pl.pallas_call(
        functools.partial(_router_body, tm=tm),
        grid=(t // tm,),
        in_specs=[
            pl.BlockSpec((tm, D_MODEL), lambda i: (i, 0)),
            pl.BlockSpec((1, D_MODEL), lambda i: (0, 0)),
            pl.BlockSpec((N_EXPERTS, D_MODEL), lambda i: (0, 0)),
        ],
        out_specs=[
            pl.BlockSpec((tm, D_MODEL), lambda i: (i, 0)),
            pl.BlockSpec((N_EXPERTS, tm), lambda i: (0, i)),
            pl.BlockSpec((N_EXPERTS, LANES), lambda i: (0, 0)),
        ],
        out_shape=[
            jax.ShapeDtypeStruct((t, D_MODEL), F32),
            jax.ShapeDtypeStruct((N_EXPERTS, t), F32),
            jax.ShapeDtypeStruct((N_EXPERTS, LANES), F32),
        ],
        scratch_shapes=[pltpu.VMEM((N_EXPERTS, LANES), F32)],
        compiler_params=_cparams("arbitrary", vmem=VMEM_LIMIT),
        name="router",
    )(x, g, wrt)


_SCATTER_TM = 128


def _row_copy(src_ref, src_row, dst_ref, dst_row, sem):
    return pltpu.make_async_copy(src_ref.at[pl.ds(src_row, 1)], dst_ref.at[pl.ds(dst_row, 1)], sem)


def _scatter_body(d1_ref, d2_ref, hp_ref, hs_ref, xs_ref, sem, *, n_prompt_tiles):
    i = pl.program_id(0)

    def run(src_ref):
        def start(r, carry):
            _row_copy(src_ref, r, xs_ref, d1_ref[0, 0, r], sem.at[0]).start()
            _row_copy(src_ref, r, xs_ref, d2_ref[0, 0, r], sem.at[1]).start()
            return carry

        def wait(r, carry):
            _row_copy(src_ref, r, xs_ref, d1_ref[0, 0, r], sem.at[0]).wait()
            _row_copy(src_ref, r, xs_ref, d2_ref[0, 0, r], sem.at[1]).wait()
            return carry

        lax.fori_loop(0, _SCATTER_TM, start, 0, unroll=DMA_UNROLL)
        lax.fori_loop(0, _SCATTER_TM, wait, 0, unroll=DMA_UNROLL)

    @pl.when(i < n_prompt_tiles)
    def _():
        run(hp_ref)

    @pl.when(i >= n_prompt_tiles)
    def _():
        run(hs_ref)


def _scatter_rows(d1, d2, h_p, h_s, n_sorted):
    tm = _SCATTER_TM
    npt = h_p.shape[0] // tm
    nst = h_s.shape[0] // tm
    smem = lambda: pl.BlockSpec((1, 1, tm), lambda i: (i, 0, 0), memory_space=pltpu.SMEM)
    return pl.pallas_call(
        functools.partial(_scatter_body, n_prompt_tiles=npt),
        grid=(npt + nst,),
        in_specs=[
            smem(), smem(),
            pl.BlockSpec((tm, D_MODEL), lambda i: (jnp.minimum(i, npt - 1), 0)),
            pl.BlockSpec((tm, D_MODEL), lambda i: (jnp.maximum(i - npt, 0), 0)),
        ],
        out_specs=pl.BlockSpec(memory_space=pl.ANY),
        out_shape=jax.ShapeDtypeStruct((n_sorted, D_MODEL), F32),
        scratch_shapes=[pltpu.SemaphoreType.DMA((2,))],
        compiler_params=_cparams("arbitrary", vmem=VMEM_LIMIT),
        name="moe_scatter",
    )(d1.reshape(-1, 1, tm), d2.reshape(-1, 1, tm), h_p, h_s)


def _gmm_body(tile_ref, exp_ref, lo_ref, hi_ref, first_ref, x_ref, wg_ref, wu_ref, wd_ref, o_ref):
    del tile_ref, exp_ref
    v = pl.program_id(0)
    c = pl.program_id(1)

    @pl.when((first_ref[v] == 1) & (c == 0))
    def _():
        o_ref[...] = jnp.zeros(o_ref.shape, F32)

    lo = lo_ref[v]
    hi = hi_ref[v]

    @pl.when(hi > lo)
    def _():
        x = x_ref[...]
        a = _silu(_dot(x, wg_ref[0])) * _dot(x, wu_ref[0])
        y = _dot(a, wd_ref[0])
        rows = lax.broadcasted_iota(jnp.int32, (MOE_TM, 1), 0)
        o_ref[...] += jnp.where((rows >= lo) & (rows < hi), y, 0.0)


def _gmm(meta, xs, w_gu, w_down):
    tile, expert, lo, hi, first = meta
    n_visits = tile.shape[0]
    grid_spec = pltpu.PrefetchScalarGridSpec(
        num_scalar_prefetch=5,
        grid=(n_visits, N_FF_CHUNKS),
        in_specs=[
            pl.BlockSpec((MOE_TM, D_MODEL), lambda v, c, t, e, *_: (t[v], 0)),
            pl.BlockSpec((1, D_MODEL, FF_CHUNK), lambda v, c, t, e, *_: (e[v], 0, c)),
            pl.BlockSpec((1, D_MODEL, FF_CHUNK), lambda v, c, t, e, *_: (e[v], 0, N_FF_CHUNKS + c)),
            pl.BlockSpec((1, FF_CHUNK, D_MODEL), lambda v, c, t, e, *_: (e[v], c, 0)),
        ],
        out_specs=pl.BlockSpec((MOE_TM, D_MODEL), lambda v, c, t, e, *_: (t[v], 0)),
    )
    return pl.pallas_call(
        _gmm_body,
        grid_spec=grid_spec,
        out_shape=jax.ShapeDtypeStruct(xs.shape, F32),
        compiler_params=_cparams("arbitrary", "arbitrary", vmem=VMEM_LIMIT),
        name="moe_gmm",
    )(tile, expert, lo, hi, first, xs, w_gu, w_gu, w_down)


def _gmm_meta(counts, n_sorted):
    n_tiles = n_sorted // MOE_TM
    n_visits = n_tiles + N_EXPERTS - 1
    ends = jnp.cumsum(counts)
    starts = ends - counts
    first_tile = starts // MOE_TM
    last_tile = jnp.maximum(ends - 1, 0) // MOE_TM
    nv = jnp.where(counts > 0, last_tile - first_tile + 1, 0)
    cv = jnp.cumsum(nv)
    v = jnp.arange(n_visits, dtype=jnp.int32)
    total = cv[-1]
    valid = v < total
    vc = jnp.minimum(v, total - 1)
    expert = jnp.sum((cv[None, :] <= vc[:, None]).astype(jnp.int32), axis=1)
    tile = first_tile[expert] + (vc - (cv[expert] - nv[expert]))
    lo = jnp.maximum(starts[expert], tile * MOE_TM) - tile * MOE_TM
    hi = jnp.minimum(ends[expert], (tile + 1) * MOE_TM) - tile * MOE_TM
    lo = jnp.where(valid, lo, 0)
    hi = jnp.where(valid, hi, 0)
    prev_tile = jnp.concatenate([jnp.full((1,), -1, jnp.int32), tile[:-1]])
    first = (valid & (tile != prev_tile)).astype(jnp.int32)
    as_i32 = lambda a: a.astype(jnp.int32)
    return as_i32(tile), as_i32(expert), as_i32(lo), as_i32(hi), first


def _combine_body(d1_ref, d2_ref, x_ref, g_ref, os_ref, y_ref, buf_ref, sem, *, tm, n_tiles):
    i = pl.program_id(0)
    slot = i % 2

    def issue(tile, slot_):
        def body(r, carry):
            _row_copy(os_ref, d1_ref[tile, r], buf_ref.at[slot_, 0], r, sem.at[slot_, 0]).start()
            _row_copy(os_ref, d2_ref[tile, r], buf_ref.at[slot_, 1], r, sem.at[slot_, 1]).start()
            return carry
        lax.fori_loop(0, tm, body, 0, unroll=DMA_UNROLL)

    @pl.when(i == 0)
    def _():
        issue(0, 0)

    @pl.when(i + 1 < n_tiles)
    def _():
        issue(i + 1, 1 - slot)

    def wait(r, carry):
        _row_copy(os_ref, d1_ref[i, r], buf_ref.at[slot, 0], r, sem.at[slot, 0]).wait()
        _row_copy(os_ref, d2_ref[i, r], buf_ref.at[slot, 1], r, sem.at[slot, 1]).wait()
        return carry
    lax.fori_loop(0, tm, wait, 0, unroll=DMA_UNROLL)

    g = g_ref[...]
    y_ref[...] = x_ref[...] + g[:, 0:1] * buf_ref[slot, 0] + g[:, 1:2] * buf_ref[slot, 1]


def _combine(d1, d2, x, gates, o_sorted, tm):
    t = x.shape[0]
    n_tiles = t // tm
    grid_spec = pltpu.PrefetchScalarGridSpec(
        num_scalar_prefetch=2,
        grid=(n_tiles,),
        in_specs=[
            pl.BlockSpec((tm, D_MODEL), lambda i, *_: (i, 0)),
            pl.BlockSpec((tm, 2), lambda i, *_: (i, 0)),
            pl.BlockSpec(memory_space=pl.ANY),
        ],
        out_specs=pl.BlockSpec((tm, D_MODEL), lambda i, *_: (i, 0)),
        scratch_shapes=[pltpu.VMEM((2, 2, tm, D_MODEL), F32), pltpu.SemaphoreType.DMA((2, 2))],
    )
    return pl.pallas_call(
        functools.partial(_combine_body, tm=tm, n_tiles=n_tiles),
        grid_spec=grid_spec,
        out_shape=jax.ShapeDtypeStruct((t, D_MODEL), F32),
        compiler_params=_cparams("arbitrary", vmem=VMEM_LIMIT),
        name="moe_combine",
    )(d1.reshape(n_tiles, tm), d2.reshape(n_tiles, tm), x, gates, o_sorted)


def _block_diag(w_grp):
    g, n, _ = w_grp.shape
    out = jnp.zeros((g * n, g * n), w_grp.dtype)
    for i in range(g):
        out = out.at[i * n:(i + 1) * n, i * n:(i + 1) * n].set(w_grp[i])
    return out


def kernel(x_prompt, x_sample, cache_mem_k, cache_mem_v, state_pool, cache_fox_k, cache_fox_v, cache_fox_logf, page_table, mem_prompt, norm_mix_g, norm_ffn_g, norm_mem_g, w_mem_kv, mem_q_norm_g, mem_k_norm_g, w_out, w_in_pool, w_pool_group, pool_scale, w_in_fox, b_forget, fox_q_norm_g, fox_k_norm_g, w_ffn_gu, w_ffn_down, w_router, w_exp_gu, w_exp_down):
    b, s, d = x_prompt.shape
    bs, ls, _ = x_sample.shape
    tp, ts = b * s, bs * ls
    tm_p, tm_s = 512, ts

    xp = x_prompt.reshape(tp, d)
    xs = x_sample.reshape(ts, d)

    mem_kt_p, mem_vt_p = _memkv(mem_prompt, norm_mem_g, w_mem_kv, mem_k_norm_g)
    mem_kt_s = cache_mem_k.transpose(0, 1, 3, 4, 2).reshape(2, bs, MEM_W, MEM_LEN)
    mem_vt_s = cache_mem_v.transpose(0, 1, 3, 4, 2).reshape(2, bs, MEM_W, MEM_LEN)

    def mem_attend_both(qm_p, qm_s, layer):
        mo_p = _mem_attend(qm_p.reshape(b, s, MEM_W), mem_kt_p[layer], mem_vt_p[layer],
                           mem_q_norm_g[layer], 512).reshape(tp, MEM_W)
        qs8 = jnp.pad(qm_s.reshape(bs, ls, MEM_W), ((0, 0), (0, SAMPLE_ROWS - ls), (0, 0)))
        mo_s = _mem_attend(qs8, mem_kt_s[layer], mem_vt_s[layer], mem_q_norm_g[layer], SAMPLE_ROWS)
        return mo_p, mo_s[:, :ls].reshape(ts, MEM_W)

    g_mix0 = norm_mix_g[0][None, :]
    up, qmp = _pool_proj(xp, g_mix0, w_in_pool[0], tm_p)
    us, qms = _pool_proj(xs, g_mix0, w_in_pool[0], tm_s)
    w_bd = _block_diag(w_pool_group[0])
    pscale = pool_scale[0][None, :]
    mix_p = _pool_prompt(up.reshape(b, s, MIX_W), w_bd, pscale, 512).reshape(tp, MIX_W)
    us_t = us.reshape(bs, ls, MIX_W).transpose(1, 0, 2)
    mix_s_t, new_state_t = _pool_sample(state_pool[0].transpose(1, 0, 2), us_t, w_bd, pscale)
    mix_s = mix_s_t.transpose(1, 0, 2).reshape(ts, MIX_W)
    pool_p = up.reshape(b, s, MIX_W)[:, s - POOL_STATE:][None]
    pool_s = new_state_t.transpose(1, 0, 2)[None]

    mo_p, mo_s = mem_attend_both(qmp, qms, 0)
    xp = _out_proj(xp, mix_p, mo_p, w_out[0], tm_p)
    xs = _out_proj(xs, mix_s, mo_s, w_out[0], tm_s)
    g_ffn0 = norm_ffn_g[0][None, :]
    xp = _ffn(xp, g_ffn0, w_ffn_gu[0], w_ffn_down[0], 1024)
    xs = _ffn(xs, g_ffn0, w_ffn_gu[0], w_ffn_down[0], tm_s)

    w_in = w_in_fox[0]
    wq = w_in[:, :MIX_W]
    wk = w_in[:, MIX_W:2 * MIX_W]
    wv = w_in[:, 2 * MIX_W:3 * MIX_W]
    wf = jnp.pad(w_in[:, 3 * MIX_W:3 * MIX_W + FOX_HEADS], ((0, 0), (0, LANES - FOX_HEADS)))
    wqm = w_in[:, 3 * MIX_W + FOX_HEADS:]
    bf_row = jnp.pad(b_forget[0], (0, LANES - FOX_HEADS))[None, :]
    qg_t = jnp.tile(fox_q_norm_g[0], FOX_HEADS)[None, :]
    kg_t = jnp.tile(fox_k_norm_g[0], FOX_HEADS)[None, :]
    g_mix1 = norm_mix_g[1][None, :]
    proj_w = (g_mix1, wq, wk, wv, wqm, wf, bf_row, qg_t, kg_t)
    qpp, kpp, ktp, vtp, lftp, qmp = _fox_proj_prompt(xp, *proj_w, tm=tm_p, n_seq=b)
    qs, kns, vns, lfts, qms = _fox_proj_sample(xs, *proj_w)

    mix_pt = _fox_flash(qpp, kpp, vtp, 512)

    pad_rows = ((0, 0), (0, SAMPLE_ROWS - ls), (0, 0))
    q8 = jnp.pad(qs.reshape(bs, ls, MIX_W), pad_rows)
    kn8 = jnp.pad(kns.reshape(bs, ls, MIX_W), pad_rows)
    vn8 = jnp.pad(vns.reshape(bs, ls, MIX_W), pad_rows)
    lfn = jnp.pad(lfts.reshape(_FL_ROWS, bs, ls).transpose(1, 0, 2),
                  ((0, 0), (0, 0), (0, SAMPLE_ROWS - ls)))
    n_phys = cache_fox_k.shape[1]
    kt_pages = cache_fox_k[0].transpose(0, 2, 3, 1).reshape(n_phys, MIX_W, PAGE_SIZE)
    vt_pages = cache_fox_v[0].transpose(0, 2, 3, 1).reshape(n_phys, MIX_W, PAGE_SIZE)
    lf_heads = cache_fox_logf[0].transpose(2, 0, 1)
    mix_s = _fox_decode(page_table, kt_pages, vt_pages, lf_heads, q8, kn8, vn8, lfn, ls)
    mix_s = mix_s[:, :ls].reshape(ts, MIX_W)

    mo_p, mo_s = mem_attend_both(qmp, qms, 1)
    xp = _out_proj(xp, mix_pt, mo_p, w_out[1], tm_p, mix_transposed=True)
    xs = _out_proj(xs, mix_s, mo_s, w_out[1], tm_s)

    g_ffn1 = norm_ffn_g[1][None, :]
    wrt = w_router[0].T
    hp, rp, cnt_p = _router(xp, g_ffn1, wrt, tm_p)
    hs, rs, cnt_s = _router(xs, g_ffn1, wrt, tm_s)
    cnt_p = cnt_p[:, 0].astype(jnp.int32)
    cnt_s = cnt_s[:, 0].astype(jnp.int32)
    counts = cnt_p + cnt_s
    offsets = jnp.cumsum(counts) - counts
    n_sorted = 2 * (tp + ts)

    def dests(r, base):
        i1, i2 = r[0].astype(jnp.int32), r[1].astype(jnp.int32)
        return base[i1] + r[2].astype(jnp.int32), base[i2] + r[3].astype(jnp.int32)

    d1p, d2p = dests(rp, offsets)
    d1s, d2s = dests(rs, offsets + cnt_p)
    x_sorted = _scatter_rows(jnp.concatenate([d1p, d1s]), jnp.concatenate([d2p, d2s]), hp, hs, n_sorted)
    o_sorted = _gmm(_gmm_meta(counts, n_sorted), x_sorted, w_exp_gu[0], w_exp_down[0])
    yp = _combine(d1p, d2p, xp, rp[4:6].T, o_sorted, 256)
    ys = _combine(d1s, d2s, xs, rs[4:6].T, o_sorted, ts)

    def heads_t(a_t, n_b, n_h):
        return a_t.reshape(n_b, n_h, HEAD_DIM, a_t.shape[-1]).transpose(0, 3, 1, 2)

    mem_k_p = jnp.stack([heads_t(mem_kt_p[i], b, MEM_HEADS) for i in range(2)])
    mem_v_p = jnp.stack([heads_t(mem_vt_p[i], b, MEM_HEADS) for i in range(2)])
    fk_p = heads_t(ktp, b, FOX_HEADS)[None]
    fv_p = heads_t(vtp, b, FOX_HEADS)[None]
    fl_p = lftp[:FOX_HEADS].reshape(FOX_HEADS, b, s).transpose(1, 2, 0)[None]
    fk_s = kns.reshape(1, bs, ls, FOX_HEADS, HEAD_DIM)
    fv_s = vns.reshape(1, bs, ls, FOX_HEADS, HEAD_DIM)
    fl_s = lfts[:FOX_HEADS].reshape(FOX_HEADS, bs, ls).transpose(1, 2, 0)[None]
    return (yp.reshape(b, s, d), ys.reshape(bs, ls, d), mem_k_p, mem_v_p, pool_p, pool_s,
            fk_p, fv_p, fl_p, fk_s, fv_s, fl_s)
```

```python
import functools
import math

import jax
import jax.numpy as jnp
from jax import lax
from jax.experimental import pallas as pl
from jax.experimental.pallas import tpu as pltpu

F32 = jnp.float32
BF16 = jnp.bfloat16

D_MODEL = 1024
HEAD_DIM = 64
MEM_LEN = 256
MEM_HEADS = 4
MEM_W = MEM_HEADS * HEAD_DIM
MIX_W = D_MODEL - MEM_W
FOX_HEADS = MIX_W // HEAD_DIM
POOL_WINDOWS = (2, 4, 8, 16)
POOL_GW = MIX_W // len(POOL_WINDOWS)
POOL_STATE = max(POOL_WINDOWS) - 1
D_FF = 2816
N_EXPERTS = 8
PAGE_SIZE = 128
EPS = 1e-6
SCALE = HEAD_DIM ** -0.5
LOG2E = math.log2(math.e)

LANES = 128
SUBLANES = 8
MXU_N = 256
VMEM_LIMIT = 56 * 1024 * 1024

FF_CHUNK = MXU_N
N_FF_CHUNKS = D_FF // FF_CHUNK
MOE_TM = 768
PAGES_PER_STEP = 8
SAMPLE_ROWS = 8
DMA_UNROLL = 8
FLASH_HEADS = 6
MOE_FF_SPLIT = 2
MOE_SUB_ROWS = 256


def _cparams(*sem, vmem=None):
    return pltpu.CompilerParams(dimension_semantics=sem, vmem_limit_bytes=vmem)


def _dot(a, b):
    return jnp.dot(a, b, preferred_element_type=F32)


def _dot_nt(a, b):
    return lax.dot_general(a, b, (((1,), (1,)), ((), ())), preferred_element_type=F32)


def _rms_rows(x, g):
    return x * lax.rsqrt(jnp.mean(x * x, axis=-1, keepdims=True) + EPS) * g


def _split3(x):
    hi = x.astype(BF16)
    r = x - hi.astype(F32)
    mid = r.astype(BF16)
    lo = (r - mid.astype(F32)).astype(BF16)
    return hi, mid, lo


def _head_sumsq_lanes(x):
    r = lax.broadcasted_iota(jnp.int32, (LANES, LANES), 0) // HEAD_DIM
    c = lax.broadcasted_iota(jnp.int32, (LANES, LANES), 1) // HEAD_DIM
    ones_bd = (r == c).astype(BF16)
    xx = x * x
    hi = xx.astype(BF16)
    lo = (xx - hi.astype(F32)).astype(BF16)
    parts = []
    for j in range(x.shape[1] // LANES):
        sl = slice(j * LANES, (j + 1) * LANES)
        parts.append(_dot(hi[:, sl], ones_bd) + _dot(lo[:, sl], ones_bd))
    return jnp.concatenate(parts, axis=1)


def _head_rms_lanes(x, g_tiled):
    ssq = _head_sumsq_lanes(x)
    return x * lax.rsqrt(ssq * (1.0 / HEAD_DIM) + EPS) * g_tiled


def _head_rms_rows_t(xt, g_col, n_heads):
    outs = []
    for h in range(n_heads):
        blk = xt[h * HEAD_DIM:(h + 1) * HEAD_DIM, :]
        ms = jnp.mean(blk * blk, axis=0, keepdims=True)
        outs.append(blk * lax.rsqrt(ms + EPS) * g_col)
    return outs


def _log_sigmoid(x):
    return jnp.minimum(x, 0.0) - jnp.log1p(jnp.exp(-jnp.abs(x)))


def _lane_head(shape, axis):
    return lax.broadcasted_iota(jnp.int32, shape, axis) // HEAD_DIM


def _memkv_body(mem_ref, gm_ref, w_ref, kg_ref, kt_ref, vt_ref):
    h = _rms_rows(mem_ref[0], gm_ref[0])
    z = _dot(h, w_ref[0])
    kt = z[:, :MEM_W].T
    vt_ref[0, 0] = z[:, MEM_W:].T
    pieces = _head_rms_rows_t(kt, kg_ref[0], MEM_HEADS)
    for h_i, p in enumerate(pieces):
        kt_ref[0, 0, h_i * HEAD_DIM:(h_i + 1) * HEAD_DIM, :] = p


def _memkv(mem, g_mem, w_kv, kn_g):
    depth, batch = w_kv.shape[0], mem.shape[0]
    out = jax.ShapeDtypeStruct((depth, batch, MEM_W, MEM_LEN), F32)
    return pl.pallas_call(
        _memkv_body,
        grid=(depth, batch),
        in_specs=[
            pl.BlockSpec((1, MEM_LEN, D_MODEL), lambda i, b: (b, 0, 0)),
            pl.BlockSpec((1, 1, D_MODEL), lambda i, b: (i, 0, 0)),
            pl.BlockSpec((1, D_MODEL, 2 * MEM_W), lambda i, b: (i, 0, 0)),
            pl.BlockSpec((1, HEAD_DIM, 1), lambda i, b: (i, 0, 0)),
        ],
        out_specs=[pl.BlockSpec((1, 1, MEM_W, MEM_LEN), lambda i, b: (i, b, 0, 0))] * 2,
        out_shape=[out, out],
        compiler_params=_cparams("arbitrary", "arbitrary"),
        name="memkv",
    )(mem, g_mem[:, None, :], w_kv, kn_g[:, :, None])


def _pool_proj_body(x_ref, g_ref, w_ref, u_ref, qm_ref):
    h = _rms_rows(x_ref[...], g_ref[...])
    u_ref[...] = _dot(h, w_ref[:, :MIX_W])
    qm_ref[...] = _dot(h, w_ref[:, MIX_W:])


def _pool_proj(x, g, w, tm):
    t = x.shape[0]
    return pl.pallas_call(
        _pool_proj_body,
        grid=(t // tm,),
        in_specs=[
            pl.BlockSpec((tm, D_MODEL), lambda i: (i, 0)),
            pl.BlockSpec((1, D_MODEL), lambda i: (0, 0)),
            pl.BlockSpec((D_MODEL, D_MODEL), lambda i: (0, 0)),
        ],
        out_specs=[pl.BlockSpec((tm, MIX_W), lambda i: (i, 0)),
                   pl.BlockSpec((tm, MEM_W), lambda i: (i, 0))],
        out_shape=[jax.ShapeDtypeStruct((t, MIX_W), F32), jax.ShapeDtypeStruct((t, MEM_W), F32)],
        compiler_params=_cparams("arbitrary", vmem=VMEM_LIMIT),
        name="pool_proj",
    )(x, g, w)


def _window_of_lane(shape):
    lane = lax.broadcasted_iota(jnp.int32, shape, len(shape) - 1)
    return jnp.where(lane < POOL_GW, POOL_WINDOWS[0],
                     jnp.where(lane < 2 * POOL_GW, POOL_WINDOWS[1],
                               jnp.where(lane < 3 * POOL_GW, POOL_WINDOWS[2], POOL_WINDOWS[3])))


def _pool_select(s2, s4, s8, s16, shape):
    lane = lax.broadcasted_iota(jnp.int32, shape, len(shape) - 1)
    return jnp.where(lane < POOL_GW, s2,
                     jnp.where(lane < 2 * POOL_GW, s4, jnp.where(lane < 3 * POOL_GW, s8, s16)))


_POOL_PAD = SUBLANES
_POOL_HALO = 2 * SUBLANES
_POOL_BASE = _POOL_PAD + _POOL_HALO


def _pool_prompt_body(u_ref, w_ref, sc_ref, o_ref, e_ref, s2_ref, s4_ref, s8_ref, *, tl):
    li = pl.program_id(1)
    n = _POOL_HALO + tl

    @pl.when(li == 0)
    def _():
        e_ref[0:_POOL_BASE, :] = jnp.zeros((_POOL_BASE, MIX_W), F32)
        s2_ref[0:_POOL_PAD, :] = jnp.zeros((_POOL_PAD, MIX_W), F32)
        s4_ref[0:_POOL_PAD, :] = jnp.zeros((_POOL_PAD, MIX_W), F32)
        s8_ref[0:_POOL_PAD, :] = jnp.zeros((_POOL_PAD, MIX_W), F32)

    u = u_ref[0]
    e_ref[_POOL_BASE:_POOL_BASE + tl, :] = u
    s2 = e_ref[_POOL_PAD:_POOL_PAD + n, :] + e_ref[_POOL_PAD - 1:_POOL_PAD - 1 + n, :]
    s2_ref[_POOL_PAD:_POOL_PAD + n, :] = s2
    s4 = s2 + s2_ref[_POOL_PAD - 2:_POOL_PAD - 2 + n, :]
    s4_ref[_POOL_PAD:_POOL_PAD + n, :] = s4
    s8 = s4 + s4_ref[_POOL_PAD - 4:_POOL_PAD - 4 + n, :]
    s8_ref[_POOL_PAD:_POOL_PAD + n, :] = s8
    s16 = s8[_POOL_HALO:, :] + s8_ref[_POOL_BASE - 8:_POOL_BASE - 8 + tl, :]
    shape = (tl, MIX_W)
    ssel = _pool_select(s2[_POOL_HALO:, :], s4[_POOL_HALO:, :], s8[_POOL_HALO:, :], s16, shape)
    pos = li * tl + lax.broadcasted_iota(jnp.int32, shape, 0)
    cnt = jnp.minimum(_window_of_lane(shape), pos + 1).astype(F32)
    diff = ssel / cnt - u
    o_ref[0] = _dot(diff, w_ref[...]) * sc_ref[...]
    e_ref[_POOL_PAD:_POOL_BASE, :] = e_ref[_POOL_PAD + tl:_POOL_BASE + tl, :]


def _pool_prompt(u, w_bd, scale, tl):
    b, s, _ = u.shape
    rows = _POOL_BASE + tl
    return pl.pallas_call(
        functools.partial(_pool_prompt_body, tl=tl),
        grid=(b, s // tl),
        in_specs=[
            pl.BlockSpec((1, tl, MIX_W), lambda bi, li: (bi, li, 0)),
            pl.BlockSpec((MIX_W, MIX_W), lambda bi, li: (0, 0)),
            pl.BlockSpec((1, MIX_W), lambda bi, li: (0, 0)),
        ],
        out_specs=pl.BlockSpec((1, tl, MIX_W), lambda bi, li: (bi, li, 0)),
        out_shape=jax.ShapeDtypeStruct((b, s, MIX_W), F32),
        scratch_shapes=[pltpu.VMEM((rows, MIX_W), F32)] * 4,
        compiler_params=_cparams("arbitrary", "arbitrary", vmem=VMEM_LIMIT),
        name="pool_prompt",
    )(u, w_bd, scale)


def _pool_sample_body(st_ref, u_ref, w_ref, sc_ref, o_ref, ns_ref, *, n_new):
    def ext(j):
        return st_ref[j] if j < POOL_STATE else u_ref[j - POOL_STATE]

    for l in range(n_new):
        r = POOL_STATE + l
        s2 = ext(r) + ext(r - 1)
        s4 = s2 + ext(r - 2) + ext(r - 3)
        s8 = s4
        for j in range(4, 8):
            s8 = s8 + ext(r - j)
        s16 = s8
        for j in range(8, 16):
            s16 = s16 + ext(r - j)
        shape = s2.shape
        cnt = _window_of_lane(shape).astype(F32)
        diff = _pool_select(s2, s4, s8, s16, shape) / cnt - ext(r)
        o_ref[l] = _dot(diff, w_ref[...]) * sc_ref[...]
    for j in range(POOL_STATE):
        ns_ref[j] = ext(j + n_new)


def _pool_sample(state_t, u_t, w_bd, scale):
    n_new, b, _ = u_t.shape
    return pl.pallas_call(
        functools.partial(_pool_sample_body, n_new=n_new),
        out_shape=[jax.ShapeDtypeStruct((n_new, b, MIX_W), F32),
                   jax.ShapeDtypeStruct((POOL_STATE, b, MIX_W), F32)],
        name="pool_sample",
    )(state_t, u_t, w_bd, scale)


def _mem_attend_body(q_ref, kt_ref, vt_ref, g_ref, o_ref):
    q = _head_rms_lanes(q_ref[0], g_ref[...]) * SCALE
    kt = kt_ref[0]
    vt = vt_ref[0]
    lh = _lane_head(q.shape, 1)
    out = jnp.zeros(q.shape, F32)
    for h in range(MEM_HEADS):
        s = _dot(jnp.where(lh == h, q, 0.0), kt)
        e = jnp.exp(s - jnp.max(s, axis=-1, keepdims=True))
        p = e / jnp.sum(e, axis=-1, keepdims=True)
        out = jnp.where(lh == h, _dot_nt(p, vt), out)
    o_ref[0] = out


def _mem_attend(qm, kt, vt, qn_g, tl):
    b, l, _ = qm.shape
    return pl.pallas_call(
        _mem_attend_body,
        grid=(b, l // tl),
        in_specs=[
            pl.BlockSpec((1, tl, MEM_W), lambda bi, li: (bi, li, 0)),
            pl.BlockSpec((1, MEM_W, MEM_LEN), lambda bi, li: (bi, 0, 0)),
            pl.BlockSpec((1, MEM_W, MEM_LEN), lambda bi, li: (bi, 0, 0)),
            pl.BlockSpec((1, MEM_W), lambda bi, li: (0, 0)),
        ],
        out_specs=pl.BlockSpec((1, tl, MEM_W), lambda bi, li: (bi, li, 0)),
        out_shape=jax.ShapeDtypeStruct((b, l, MEM_W), F32),
        compiler_params=_cparams("arbitrary", "arbitrary"),
        name="mem_attend",
    )(qm, kt, vt, jnp.tile(qn_g, MEM_HEADS)[None, :])


def _out_proj_body(x_ref, mix_ref, mem_ref, w_ref, o_ref, *, mix_transposed):
    mix = mix_ref[0].T if mix_transposed else mix_ref[...]
    o_ref[...] = x_ref[...] + _dot(mix, w_ref[:MIX_W, :]) + _dot(mem_ref[...], w_ref[MIX_W:, :])


def _out_proj(x, mix, mem, w, tm, mix_transposed=False):
    t = x.shape[0]
    if mix_transposed:
        per_seq = mix.shape[2] // tm
        mix_spec = pl.BlockSpec((1, MIX_W, tm), lambda i: (i // per_seq, 0, i % per_seq))
    else:
        mix_spec = pl.BlockSpec((tm, MIX_W), lambda i: (i, 0))
    return pl.pallas_call(
        functools.partial(_out_proj_body, mix_transposed=mix_transposed),
        grid=(t // tm,),
        in_specs=[
            pl.BlockSpec((tm, D_MODEL), lambda i: (i, 0)),
            mix_spec,
            pl.BlockSpec((tm, MEM_W), lambda i: (i, 0)),
            pl.BlockSpec((D_MODEL, D_MODEL), lambda i: (0, 0)),
        ],
        out_specs=pl.BlockSpec((tm, D_MODEL), lambda i: (i, 0)),
        out_shape=jax.ShapeDtypeStruct((t, D_MODEL), F32),
        compiler_params=_cparams("arbitrary", vmem=VMEM_LIMIT),
        name="out_proj",
    )(x, mix, mem, w)


def _silu(g):
    return g / (1.0 + jnp.exp(-g))


def _ffn_body(x_ref, g_ref, wg_ref, wu_ref, wd_ref, o_ref, h_ref):
    c = pl.program_id(1)

    @pl.when(c == 0)
    def _():
        x = x_ref[...]
        h_ref[...] = _rms_rows(x, g_ref[...])
        o_ref[...] = x

    h = h_ref[...]
    a = _silu(_dot(h, wg_ref[...])) * _dot(h, wu_ref[...])
    o_ref[...] += _dot(a, wd_ref[...])


def _ffn(x, g, w_gu, w_down, tm):
    t = x.shape[0]
    return pl.pallas_call(
        _ffn_body,
        grid=(t // tm, N_FF_CHUNKS),
        in_specs=[
            pl.BlockSpec((tm, D_MODEL), lambda i, c: (i, 0)),
            pl.BlockSpec((1, D_MODEL), lambda i, c: (0, 0)),
            pl.BlockSpec((D_MODEL, FF_CHUNK), lambda i, c: (0, c)),
            pl.BlockSpec((D_MODEL, FF_CHUNK), lambda i, c: (0, N_FF_CHUNKS + c)),
            pl.BlockSpec((FF_CHUNK, D_MODEL), lambda i, c: (c, 0)),
        ],
        out_specs=pl.BlockSpec((tm, D_MODEL), lambda i, c: (i, 0)),
        out_shape=jax.ShapeDtypeStruct((t, D_MODEL), F32),
        scratch_shapes=[pltpu.VMEM((tm, D_MODEL), F32)],
        compiler_params=_cparams("arbitrary", "arbitrary", vmem=VMEM_LIMIT),
        name="ffn",
    )(x, g, w_gu, w_gu, w_down)


_FL_ROWS = 2 * SUBLANES
_AUG0 = HEAD_DIM


def _aug_pieces(c_col):
    hi, mid, lo = _split3(c_col)
    bc = lambda a: jnp.broadcast_to(a.astype(F32), (c_col.shape[0], LANES))
    return bc(hi), bc(mid), bc(lo)


def _fox_proj_prompt_body(x_ref, g_ref, wq_ref, wk_ref, wv_ref, wqm_ref, wf_ref, bf_ref, qg_ref, kg_ref,
                          qp_ref, kp_ref, kt_ref, vt_ref, lft_ref, qm_ref, carry_ref, *, tm, per_seq):
    @pl.when(pl.program_id(0) % per_seq == 0)
    def _():
        carry_ref[...] = jnp.zeros(carry_ref.shape, F32)

    h = _rms_rows(x_ref[...], g_ref[...])
    q = _head_rms_lanes(_dot(h, wq_ref[...]), qg_ref[...]) * (SCALE * LOG2E)
    k = _head_rms_lanes(_dot(h, wk_ref[...]), kg_ref[...])
    v = _dot(h, wv_ref[...])
    qm_ref[...] = _dot(h, wqm_ref[...])
    kt_ref[0] = k.T
    vt_ref[0] = v.T
    lane = lax.broadcasted_iota(jnp.int32, (tm, LANES), 1)
    lf = jnp.where(lane < FOX_HEADS, _log_sigmoid(_dot(h, wf_ref[...]) + bf_ref[...]), 0.0)
    lft_ref[...] = lf.T[:_FL_ROWS, :]
    r = lax.broadcasted_iota(jnp.int32, (tm, tm), 0)
    c = lax.broadcasted_iota(jnp.int32, (tm, tm), 1)
    lower = (c <= r).astype(BF16)
    hi, mid, lo = _split3(lf)
    csum = _dot(lower, hi) + _dot(lower, mid) + _dot(lower, lo) + carry_ref[0:1, :]
    carry_ref[...] = jnp.broadcast_to(csum[tm - 1:tm, :], carry_ref.shape)
    c2 = csum * LOG2E
    one = jnp.ones((tm, LANES), F32)
    zero = jnp.zeros((tm, LANES), F32)
    for hd in range(FOX_HEADS):
        chi, cmid, clo = _aug_pieces(c2[:, hd:hd + 1])
        aug_q = jnp.where(lane == _AUG0, chi, jnp.where(lane == _AUG0 + 1, cmid, jnp.where(
            lane == _AUG0 + 2, clo, jnp.where(lane < _AUG0 + 6, one, zero))))
        aug_k = jnp.where(lane < _AUG0 + 3, one, jnp.where(lane == _AUG0 + 3, -chi, jnp.where(
            lane == _AUG0 + 4, -cmid, jnp.where(lane == _AUG0 + 5, -clo, zero))))
        col = slice((hd // 2) * LANES, (hd // 2 + 1) * LANES)
        qc, kc = q[:, col], k[:, col]
        if hd % 2:
            qc = pltpu.roll(qc, HEAD_DIM, 1)
            kc = pltpu.roll(kc, HEAD_DIM, 1)
        qp_ref[0, hd] = jnp.where(lane < HEAD_DIM, qc, aug_q)
        kp_ref[0, hd] = jnp.where(lane < HEAD_DIM, kc, aug_k)


def _fox_proj_prompt(x, g, wq, wk, wv, wqm, wf, bf_row, qg_t, kg_t, tm, n_seq):
    t = x.shape[0]
    seq = t // n_seq
    per_seq = seq // tm

    def const(shape):
        return pl.BlockSpec(shape, lambda i: tuple(0 for _ in shape))

    heads_spec = pl.BlockSpec((1, FOX_HEADS, tm, LANES), lambda i: (i // per_seq, 0, i % per_seq, 0))
    t_spec = pl.BlockSpec((1, MIX_W, tm), lambda i: (i // per_seq, 0, i % per_seq))
    return pl.pallas_call(
        functools.partial(_fox_proj_prompt_body, tm=tm, per_seq=per_seq),
        grid=(t // tm,),
        in_specs=[
            pl.BlockSpec((tm, D_MODEL), lambda i: (i, 0)),
            const((1, D_MODEL)),
            const((D_MODEL, MIX_W)), const((D_MODEL, MIX_W)), const((D_MODEL, MIX_W)),
            const((D_MODEL, MEM_W)), const((D_MODEL, LANES)),
            const((1, LANES)), const((1, MIX_W)), const((1, MIX_W)),
        ],
        out_specs=[heads_spec, heads_spec, t_spec, t_spec,
                   pl.BlockSpec((_FL_ROWS, tm), lambda i: (0, i)),
                   pl.BlockSpec((tm, MEM_W), lambda i: (i, 0))],
        out_shape=[
            jax.ShapeDtypeStruct((n_seq, FOX_HEADS, seq, LANES), F32),
            jax.ShapeDtypeStruct((n_seq, FOX_HEADS, seq, LANES), F32),
            jax.ShapeDtypeStruct((n_seq, MIX_W, seq), F32),
            jax.ShapeDtypeStruct((n_seq, MIX_W, seq), F32),
            jax.ShapeDtypeStruct((_FL_ROWS, t), F32),
            jax.ShapeDtypeStruct((t, MEM_W), F32),
        ],
        scratch_shapes=[pltpu.VMEM((SUBLANES, LANES), F32)],
        compiler_params=_cparams("arbitrary", vmem=VMEM_LIMIT),
        name="fox_proj_prompt",
    )(x, g, wq, wk, wv, wqm, wf, bf_row, qg_t, kg_t)


def _fox_proj_sample_body(x_ref, g_ref, wq_ref, wk_ref, wv_ref, wqm_ref, wf_ref, bf_ref, qg_ref, kg_ref,
                          q_ref, k_ref, v_ref, lft_ref, qm_ref):
    h = _rms_rows(x_ref[...], g_ref[...])
    q_ref[...] = _head_rms_lanes(_dot(h, wq_ref[...]), qg_ref[...])
    k_ref[...] = _head_rms_lanes(_dot(h, wk_ref[...]), kg_ref[...])
    v_ref[...] = _dot(h, wv_ref[...])
    qm_ref[...] = _dot(h, wqm_ref[...])
    lane = lax.broadcasted_iota(jnp.int32, (x_ref.shape[0], LANES), 1)
    lf = jnp.where(lane < FOX_HEADS, _log_sigmoid(_dot(h, wf_ref[...]) + bf_ref[...]), 0.0)
    lft_ref[...] = lf.T[:_FL_ROWS, :]


def _fox_proj_sample(x, g, wq, wk, wv, wqm, wf, bf_row, qg_t, kg_t):
    t = x.shape[0]
    rows = jax.ShapeDtypeStruct((t, MIX_W), F32)
    return pl.pallas_call(
        _fox_proj_sample_body,
        out_shape=[rows, rows, rows, jax.ShapeDtypeStruct((_FL_ROWS, t), F32),
                   jax.ShapeDtypeStruct((t, MEM_W), F32)],
        compiler_params=_cparams(vmem=VMEM_LIMIT),
        name="fox_proj_sample",
    )(x, g, wq, wk, wv, wqm, wf, bf_row, qg_t, kg_t)


def _fox_flash_body(qi_ref, ki_ref, qp_ref, kp_ref, vt_ref, o_ref, m_ref, l_ref, acc_ref, *, blk):
    p = pl.program_id(2)
    qi = qi_ref[p]
    ki = ki_ref[p]

    @pl.when(ki == 0)
    def _():
        m_ref[...] = jnp.full(m_ref.shape, -jnp.inf, F32)
        l_ref[...] = jnp.zeros(l_ref.shape, F32)
        acc_ref[...] = jnp.zeros(acc_ref.shape, F32)

    def step(diagonal):
        for hh in range(FLASH_HEADS):
            pair = hh // 2
            vt = vt_ref[0, pair * LANES:(pair + 1) * LANES, :]
            st = _dot_nt(kp_ref[0, hh], qp_ref[0, hh])
            if diagonal:
                key = lax.broadcasted_iota(jnp.int32, st.shape, 0)
                qry = lax.broadcasted_iota(jnp.int32, st.shape, 1)
                st = jnp.where(key <= qry, st, -jnp.inf)
            m_prev = m_ref[hh]
            m_new = jnp.maximum(m_prev, jnp.max(st, axis=0, keepdims=True))
            alpha = jnp.exp2(m_prev - m_new)
            pt = jnp.exp2(st - m_new)
            l_ref[hh] = alpha * l_ref[hh] + jnp.sum(pt, axis=0, keepdims=True)
            acc_ref[hh] = alpha * acc_ref[hh] + _dot(vt, pt)
            m_ref[hh] = m_new

    @pl.when(ki < qi)
    def _():
        step(False)

    @pl.when(ki == qi)
    def _():
        step(True)
        row = lax.broadcasted_iota(jnp.int32, (LANES, blk), 0)
        for pair in range(FLASH_HEADS // 2):
            a, c = 2 * pair, 2 * pair + 1
            o_ref[0, pair * LANES:(pair + 1) * LANES, :] = jnp.where(
                row < HEAD_DIM, acc_ref[a] / l_ref[a], acc_ref[c] / l_ref[c])


def _fox_flash(qp, kp, vt, blk):
    b, _, s, _ = qp.shape
    n = s // blk
    pairs = [(qi, ki) for qi in range(n) for ki in range(qi + 1)]
    qi_tab = jnp.asarray([p[0] for p in pairs], jnp.int32)
    ki_tab = jnp.asarray([p[1] for p in pairs], jnp.int32)
    fh = FLASH_HEADS
    ch = fh // 2 * LANES
    grid_spec = pltpu.PrefetchScalarGridSpec(
        num_scalar_prefetch=2,
        grid=(b, FOX_HEADS // fh, len(pairs)),
        in_specs=[
            pl.BlockSpec((1, fh, blk, LANES), lambda bi, hg, p, qt, kt: (bi, hg, qt[p], 0)),
            pl.BlockSpec((1, fh, blk, LANES), lambda bi, hg, p, qt, kt: (bi, hg, kt[p], 0)),
            pl.BlockSpec((1, ch, blk), lambda bi, hg, p, qt, kt: (bi, hg, kt[p])),
        ],
        out_specs=pl.BlockSpec((1, ch, blk), lambda bi, hg, p, qt, kt: (bi, hg, qt[p])),
        scratch_shapes=[pltpu.VMEM((fh, 1, blk), F32), pltpu.VMEM((fh, 1, blk), F32),
                        pltpu.VMEM((fh, LANES, blk), F32)],
    )
    return pl.pallas_call(
        functools.partial(_fox_flash_body, blk=blk),
        grid_spec=grid_spec,
        out_shape=jax.ShapeDtypeStruct((b, MIX_W, s), F32),
        compiler_params=_cparams("arbitrary", "arbitrary", "arbitrary", vmem=VMEM_LIMIT),
        name="fox_flash",
    )(qi_tab, ki_tab, qp, kp, vt)


_QROWS = FOX_HEADS * SAMPLE_ROWS
_CHUNK = PAGES_PER_STEP * PAGE_SIZE


def _expand_heads(x):
    n = x.shape[1]
    return jnp.broadcast_to(x[:FOX_HEADS, None, :], (FOX_HEADS, SAMPLE_ROWS, n)).reshape(_QROWS, n)


def _page_copies(pt_ref, kt_hbm, vt_hbm, lf_hbm, kt_buf, vt_buf, lf_buf, sem, g, slot, steps, n_pages):
    bi = g // steps
    first = n_pages - (g % steps + 1) * PAGES_PER_STEP
    copies = []
    for i in range(PAGES_PER_STEP):
        page = pt_ref[bi, first + i]
        lanes = pl.ds(i * PAGE_SIZE, PAGE_SIZE)
        copies.append(pltpu.make_async_copy(kt_hbm.at[page], kt_buf.at[slot, :, lanes], sem.at[slot, 0]))
        copies.append(pltpu.make_async_copy(vt_hbm.at[page], vt_buf.at[slot, :, lanes], sem.at[slot, 1]))
        copies.append(pltpu.make_async_copy(lf_hbm.at[:, page], lf_buf.at[slot, i, pl.ds(0, FOX_HEADS)],
                                            sem.at[slot, 2]))
    return copies


def _fox_decode_body(pt_ref, kt_hbm, vt_hbm, lf_hbm, q_ref, kn_ref, vn_ref, lfn_ref, o_ref,
                     kt_buf, vt_buf, lf_buf, sem, qbd_ref, m_ref, l_ref, acc_ref, carry_ref, crow_ref,
                     *, n_new, steps, n_pages):
    g = pl.program_id(0)
    n_steps = pl.num_programs(0)
    j = g % steps
    slot = g % 2
    copies = functools.partial(_page_copies, pt_ref, kt_hbm, vt_hbm, lf_hbm, kt_buf, vt_buf, lf_buf, sem,
                               steps=steps, n_pages=n_pages)
    row_l = lax.broadcasted_iota(jnp.int32, (_QROWS, 1), 0) % SAMPLE_ROWS

    @pl.when(g == 0)
    def _():
        lf_buf[...] = jnp.zeros(lf_buf.shape, F32)
        for cp in copies(g=g, slot=slot):
            cp.start()

    @pl.when(g + 1 < n_steps)
    def _():
        for cp in copies(g=g + 1, slot=1 - slot):
            cp.start()

    @pl.when(j == 0)
    def _():
        q = q_ref[0] * SCALE
        lh = _lane_head(q.shape, 1)
        for h in range(FOX_HEADS):
            qbd_ref[h * SAMPLE_ROWS:(h + 1) * SAMPLE_ROWS, :] = jnp.where(lh == h, q, 0.0)
        lfn = lfn_ref[0]
        lane = lax.broadcasted_iota(jnp.int32, lfn.shape, 1)
        c = jnp.zeros(lfn.shape, F32)
        for m in range(n_new):
            cm = jnp.sum(jnp.where(lane <= m, lfn, 0.0), axis=1, keepdims=True)
            c = jnp.where(lane == m, cm, c)
        c_q = _expand_heads(c)
        col = lax.broadcasted_iota(jnp.int32, c_q.shape, 1)
        crow = jnp.sum(jnp.where(col == row_l, c_q, 0.0), axis=1, keepdims=True)
        crow_ref[...] = crow
        s = _dot_nt(qbd_ref[...], kn_ref[0]) + (crow - c_q)
        valid = (col < n_new) & ((col <= row_l) | (row_l >= n_new))
        s = jnp.where(valid, s, -jnp.inf)
        m0 = jnp.max(s, axis=-1, keepdims=True)
        p = jnp.exp(s - m0)
        m_ref[...] = m0
        l_ref[...] = jnp.sum(p, axis=-1, keepdims=True)
        acc_ref[...] = _dot(p, vn_ref[0])
        carry_ref[...] = jnp.zeros(carry_ref.shape, F32)

    for cp in copies(g=g, slot=slot):
        cp.wait()

    r = lax.broadcasted_iota(jnp.int32, (PAGE_SIZE, PAGE_SIZE), 0)
    cc = lax.broadcasted_iota(jnp.int32, (PAGE_SIZE, PAGE_SIZE), 1)
    later = (r > cc).astype(BF16)
    lf3 = lf_buf[slot]
    tot = jnp.sum(lf3, axis=2, keepdims=True)
    hi, mid, lo = _split3(lf3.reshape(PAGES_PER_STEP * _FL_ROWS, PAGE_SIZE))
    d_in = (_dot(hi, later) + _dot(mid, later) + _dot(lo, later)).reshape(PAGES_PER_STEP, _FL_ROWS, PAGE_SIZE)
    after = carry_ref[:, 0:1]
    d_pages = [None] * PAGES_PER_STEP
    for i in reversed(range(PAGES_PER_STEP)):
        d_pages[i] = _expand_heads(d_in[i] + after)
        after = after + tot[i]
    carry_ref[...] = jnp.broadcast_to(after, carry_ref.shape)
    d = jnp.concatenate(d_pages, axis=1)

    vt = vt_buf[slot]
    s = _dot(qbd_ref[...], kt_buf[slot]) + crow_ref[...] + d
    m_prev = m_ref[...]
    m_new = jnp.maximum(m_prev, jnp.max(s, axis=-1, keepdims=True))
    alpha = jnp.exp(m_prev - m_new)
    p = jnp.exp(s - m_new)
    l_ref[...] = alpha * l_ref[...] + jnp.sum(p, axis=-1, keepdims=True)
    acc_ref[...] = alpha * acc_ref[...] + _dot_nt(p, vt)
    m_ref[...] = m_new

    @pl.when(j == steps - 1)
    def _():
        res = acc_ref[...] / l_ref[...]
        lh = _lane_head((SAMPLE_ROWS, MIX_W), 1)
        out = jnp.zeros((SAMPLE_ROWS, MIX_W), F32)
        for h in range(FOX_HEADS):
            out = jnp.where(lh == h, res[h * SAMPLE_ROWS:(h + 1) * SAMPLE_ROWS, :], out)
        o_ref[0] = out


def _fox_decode(page_table, kt_pages, vt_pages, lf_heads, q8, kn8, vn8, lfn, n_new):
    b, n_pages = page_table.shape
    steps = n_pages // PAGES_PER_STEP
    per_b = lambda g, pt: (g // steps, 0, 0)
    any_spec = pl.BlockSpec(memory_space=pl.ANY)
    grid_spec = pltpu.PrefetchScalarGridSpec(
        num_scalar_prefetch=1,
        grid=(b * steps,),
        in_specs=[any_spec, any_spec, any_spec,
                  pl.BlockSpec((1, SAMPLE_ROWS, MIX_W), per_b),
                  pl.BlockSpec((1, SAMPLE_ROWS, MIX_W), per_b),
                  pl.BlockSpec((1, SAMPLE_ROWS, MIX_W), per_b),
                  pl.BlockSpec((1, _FL_ROWS, SAMPLE_ROWS), per_b)],
        out_specs=pl.BlockSpec((1, SAMPLE_ROWS, MIX_W), per_b),
        scratch_shapes=[
            pltpu.VMEM((2, MIX_W, _CHUNK), F32),
            pltpu.VMEM((2, MIX_W, _CHUNK), F32),
            pltpu.VMEM((2, PAGES_PER_STEP, _FL_ROWS, PAGE_SIZE), F32),
            pltpu.SemaphoreType.DMA((2, 3)),
            pltpu.VMEM((_QROWS, MIX_W), F32),
            pltpu.VMEM((_QROWS, 1), F32), pltpu.VMEM((_QROWS, 1), F32),
            pltpu.VMEM((_QROWS, MIX_W), F32),
            pltpu.VMEM((_FL_ROWS, LANES), F32),
            pltpu.VMEM((_QROWS, 1), F32),
        ],
    )
    return pl.pallas_call(
        functools.partial(_fox_decode_body, n_new=n_new, steps=steps, n_pages=n_pages),
        grid_spec=grid_spec,
        out_shape=jax.ShapeDtypeStruct((b, SAMPLE_ROWS, MIX_W), F32),
        compiler_params=_cparams("arbitrary", vmem=VMEM_LIMIT),
        name="fox_decode",
    )(page_table, kt_pages, vt_pages, lf_heads, q8, kn8, vn8, lfn)


def _router_body(x_ref, g_ref, wrt_ref, h_ref, r_ref, cnt_ref, carry_ref, *, tm):
    @pl.when(pl.program_id(0) == 0)
    def _():
        carry_ref[...] = jnp.zeros(carry_ref.shape, F32)

    h = _rms_rows(x_ref[...], g_ref[...])
    h_ref[...] = h
    h_hi = h.astype(BF16)
    h_lo = (h - h_hi.astype(F32)).astype(BF16)
    w = wrt_ref[...]
    w_hi = w.astype(BF16)
    w_lo = (w - w_hi.astype(F32)).astype(BF16)
    lg = _dot_nt(w_hi, h_hi) + _dot_nt(w_hi, h_lo) + _dot_nt(w_lo, h_hi)
    idx = lax.broadcasted_iota(jnp.int32, lg.shape, 0)
    m1 = jnp.max(lg, axis=0, keepdims=True)
    i1 = jnp.min(jnp.where(lg == m1, idx, N_EXPERTS), axis=0, keepdims=True)
    sel1 = idx == i1
    lg2 = jnp.where(sel1, -jnp.inf, lg)
    m2 = jnp.max(lg2, axis=0, keepdims=True)
    i2 = jnp.min(jnp.where(lg2 == m2, idx, N_EXPERTS), axis=0, keepdims=True)
    sel2 = idx == i2
    e = jnp.exp(m2 - m1)
    g1 = 1.0 / (1.0 + e)
    g2 = e / (1.0 + e)
    assign = jnp.where(sel1 | sel2, 1.0, 0.0)
    r = lax.broadcasted_iota(jnp.int32, (tm, tm), 0)
    c = lax.broadcasted_iota(jnp.int32, (tm, tm), 1)
    before = (r < c).astype(BF16)
    rank = _dot(assign.astype(BF16), before) + carry_ref[:, 0:1]
    r1 = jnp.sum(jnp.where(sel1, rank, 0.0), axis=0, keepdims=True)
    r2 = jnp.sum(jnp.where(sel2, rank, 0.0), axis=0, keepdims=True)
    carry = carry_ref[...] + jnp.sum(assign, axis=1, keepdims=True)
    carry_ref[...] = carry
    cnt_ref[...] = carry
    rows = [i1.astype(F32), i2.astype(F32), r1, r2, g1, g2]
    out = jnp.zeros(lg.shape, F32)
    for k, v in enumerate(rows):
        out = jnp.where(idx == k, v, out)
    r_ref[...] = out


def _router(x, g, wrt, tm):
    t = x.shape[0]
    return pl.pallas_call(
        functools.partial(_router_body, tm=tm),
        grid=(t // tm,),
        in_specs=[
            pl.BlockSpec((tm, D_MODEL), lambda i: (i, 0)),
            pl.BlockSpec((1, D_MODEL), lambda i: (0, 0)),
            pl.BlockSpec((N_EXPERTS, D_MODEL), lambda i: (0, 0)),
        ],
        out_specs=[
            pl.BlockSpec((tm, D_MODEL), lambda i: (i, 0)),
            pl.BlockSpec((N_EXPERTS, tm), lambda i: (0, i)),
            pl.BlockSpec((N_EXPERTS, LANES), lambda i: (0, 0)),
        ],
        out_shape=[
            jax.ShapeDtypeStruct((t, D_MODEL), F32),
            jax.ShapeDtypeStruct((N_EXPERTS, t), F32),
            jax.ShapeDtypeStruct((N_EXPERTS, LANES), F32),
        ],
        scratch_shapes=[pltpu.VMEM((N_EXPERTS, LANES), F32)],
        compiler_params=_cparams("arbitrary", vmem=VMEM_LIMIT),
        name="router",
    )(x, g, wrt)


_SCATTER_TM = 128


def _row_copy(src_ref, src_row, dst_ref, dst_row, sem):
    return pltpu.make_async_copy(src_ref.at[pl.ds(src_row, 1)], dst_ref.at[pl.ds(dst_row, 1)], sem)


def _scatter_body(d1_ref, d2_ref, hp_ref, hs_ref, xs_ref, sem, *, n_prompt_tiles):
    i = pl.program_id(0)

    def run(src_ref):
        def start(r, carry):
            _row_copy(src_ref, r, xs_ref, d1_ref[0, 0, r], sem.at[0]).start(priority=0)
            _row_copy(src_ref, r, xs_ref, d2_ref[0, 0, r], sem.at[1]).start(priority=1)
            return carry

        lax.fori_loop(0, _SCATTER_TM, start, 0, unroll=DMA_UNROLL)
        for k in range(2):
            pltpu.make_async_copy(src_ref, xs_ref.at[pl.ds(0, _SCATTER_TM)], sem.at[k]).wait()

    @pl.when(i < n_prompt_tiles)
    def _():
        run(hp_ref)

    @pl.when(i >= n_prompt_tiles)
    def _():
        run(hs_ref)


def _scatter_rows(d1, d2, h_p, h_s, n_sorted):
    tm = _SCATTER_TM
    npt = h_p.shape[0] // tm
    nst = h_s.shape[0] // tm
    smem = lambda: pl.BlockSpec((1, 1, tm), lambda i: (i, 0, 0), memory_space=pltpu.SMEM)
    return pl.pallas_call(
        functools.partial(_scatter_body, n_prompt_tiles=npt),
        grid=(npt + nst,),
        in_specs=[
            smem(), smem(),
            pl.BlockSpec((tm, D_MODEL), lambda i: (jnp.minimum(i, npt - 1), 0)),
            pl.BlockSpec((tm, D_MODEL), lambda i: (jnp.maximum(i - npt, 0), 0)),
        ],
        out_specs=pl.BlockSpec(memory_space=pl.ANY),
        out_shape=jax.ShapeDtypeStruct((n_sorted, D_MODEL), F32),
        scratch_shapes=[pltpu.SemaphoreType.DMA((2,))],
        compiler_params=_cparams("arbitrary", vmem=VMEM_LIMIT),
        name="moe_scatter",
    )(d1.reshape(-1, 1, tm), d2.reshape(-1, 1, tm), h_p, h_s)


def _gmm_body(tile_ref, exp_ref, lo_ref, hi_ref, first_ref, x_ref, wg_ref, wu_ref, wd_ref, o_ref):
    del tile_ref, exp_ref
    v = pl.program_id(0)
    c = pl.program_id(1)

    @pl.when((first_ref[v] == 1) & (c == 0))
    def _():
        o_ref[...] = jnp.zeros(o_ref.shape, F32)

    lo = lo_ref[v]
    hi = hi_ref[v]
    whole = (lo == 0) & (hi == MOE_TM)

    def swiglu_into(rows, mask):
        x = x_ref[rows, :]
        for off, n in _MOE_SUBS:
            cols = slice(off, off + n)
            a = _silu(_dot(x, wg_ref[0, :, cols])) * _dot(x, wu_ref[0, :, cols])
            y = _dot(a, wd_ref[0, cols, :])
            o_ref[rows, :] += y if mask is None else jnp.where(mask, y, 0.0)

    @pl.when(whole)
    def _():
        swiglu_into(slice(None), None)

    @pl.when(jnp.logical_not(whole) & (hi > lo))
    def _():
        for r0 in range(0, MOE_TM, MOE_SUB_ROWS):
            @pl.when((lo < r0 + MOE_SUB_ROWS) & (hi > r0))
            def _():
                rows = r0 + lax.broadcasted_iota(jnp.int32, (MOE_SUB_ROWS, 1), 0)
                swiglu_into(slice(r0, r0 + MOE_SUB_ROWS), (rows >= lo) & (rows < hi))


_MOE_FF = D_FF // MOE_FF_SPLIT
_MOE_SUBS = tuple((off, min(MXU_N, _MOE_FF - off)) for off in range(0, _MOE_FF, MXU_N))


def _gmm(meta, xs, w_gu, w_down):
    tile, expert, lo, hi, first = meta
    n_visits = tile.shape[0]
    grid_spec = pltpu.PrefetchScalarGridSpec(
        num_scalar_prefetch=5,
        grid=(n_visits, MOE_FF_SPLIT),
        in_specs=[
            pl.BlockSpec((MOE_TM, D_MODEL), lambda v, c, t, e, *_: (t[v], 0)),
            pl.BlockSpec((1, D_MODEL, _MOE_FF), lambda v, c, t, e, *_: (e[v], 0, c)),
            pl.BlockSpec((1, D_MODEL, _MOE_FF), lambda v, c, t, e, *_: (e[v], 0, MOE_FF_SPLIT + c)),
            pl.BlockSpec((1, _MOE_FF, D_MODEL), lambda v, c, t, e, *_: (e[v], c, 0)),
        ],
        out_specs=pl.BlockSpec((MOE_TM, D_MODEL), lambda v, c, t, e, *_: (t[v], 0)),
    )
    return pl.pallas_call(
        _gmm_body,
        grid_spec=grid_spec,
        out_shape=jax.ShapeDtypeStruct(xs.shape, F32),
        compiler_params=_cparams("arbitrary", "arbitrary", vmem=VMEM_LIMIT),
        name="moe_gmm",
    )(tile, expert, lo, hi, first, xs, w_gu, w_gu, w_down)


def _gmm_meta(counts, n_sorted):
    n_tiles = n_sorted // MOE_TM
    n_visits = n_tiles + N_EXPERTS - 1
    ends = jnp.cumsum(counts)
    starts = ends - counts
    first_tile = starts // MOE_TM
    last_tile = jnp.maximum(ends - 1, 0) // MOE_TM
    nv = jnp.where(counts > 0, last_tile - first_tile + 1, 0)
    cv = jnp.cumsum(nv)
    v = jnp.arange(n_visits, dtype=jnp.int32)
    total = cv[-1]
    valid = v < total
    vc = jnp.minimum(v, total - 1)
    expert = jnp.sum((cv[None, :] <= vc[:, None]).astype(jnp.int32), axis=1)
    tile = first_tile[expert] + (vc - (cv[expert] - nv[expert]))
    lo = jnp.maximum(starts[expert], tile * MOE_TM) - tile * MOE_TM
    hi = jnp.minimum(ends[expert], (tile + 1) * MOE_TM) - tile * MOE_TM
    lo = jnp.where(valid, lo, 0)
    hi = jnp.where(valid, hi, 0)
    prev_tile = jnp.concatenate([jnp.full((1,), -1, jnp.int32), tile[:-1]])
    first = (valid & (tile != prev_tile)).astype(jnp.int32)
    as_i32 = lambda a: a.astype(jnp.int32)
    return as_i32(tile), as_i32(expert), as_i32(lo), as_i32(hi), first


def _combine_body(d1_ref, d2_ref, x_ref, g_ref, os_ref, y_ref, buf_ref, sem, *, tm, n_tiles):
    i = pl.program_id(0)
    slot = i % 2

    def issue(tile, slot_):
        def body(r, carry):
            _row_copy(os_ref, d1_ref[tile, r], buf_ref.at[slot_, 0], r, sem.at[slot_, 0]).start(priority=0)
            _row_copy(os_ref, d2_ref[tile, r], buf_ref.at[slot_, 1], r, sem.at[slot_, 1]).start(priority=1)
            return carry
        lax.fori_loop(0, tm, body, 0, unroll=DMA_UNROLL)

    @pl.when(i == 0)
    def _():
        issue(0, 0)

    @pl.when(i + 1 < n_tiles)
    def _():
        issue(i + 1, 1 - slot)

    for k in range(2):
        pltpu.make_async_copy(os_ref.at[pl.ds(0, tm)], buf_ref.at[slot, k], sem.at[slot, k]).wait()

    g = g_ref[...]
    y_ref[...] = x_ref[...] + g[:, 0:1] * buf_ref[slot, 0] + g[:, 1:2] * buf_ref[slot, 1]


def _combine(d1, d2, x, gates, o_sorted, tm):
    t = x.shape[0]
    n_tiles = t // tm
    grid_spec = pltpu.PrefetchScalarGridSpec(
        num_scalar_prefetch=2,
        grid=(n_tiles,),
        in_specs=[
            pl.BlockSpec((tm, D_MODEL), lambda i, *_: (i, 0)),
            pl.BlockSpec((tm, 2), lambda i, *_: (i, 0)),
            pl.BlockSpec(memory_space=pl.ANY),
        ],
        out_specs=pl.BlockSpec((tm, D_MODEL), lambda i, *_: (i, 0)),
        scratch_shapes=[pltpu.VMEM((2, 2, tm, D_MODEL), F32), pltpu.SemaphoreType.DMA((2, 2))],
    )
    return pl.pallas_call(
        functools.partial(_combine_body, tm=tm, n_tiles=n_tiles),
        grid_spec=grid_spec,
        out_shape=jax.ShapeDtypeStruct((t, D_MODEL), F32),
        compiler_params=_cparams("arbitrary", vmem=VMEM_LIMIT),
        name="moe_combine",
    )(d1.reshape(n_tiles, tm), d2.reshape(n_tiles, tm), x, gates, o_sorted)


def _block_diag(w_grp):
    g, n, _ = w_grp.shape
    out = jnp.zeros((g * n, g * n), w_grp.dtype)
    for i in range(g):
        out = out.at[i * n:(i + 1) * n, i * n:(i + 1) * n].set(w_grp[i])
    return out


def kernel(x_prompt, x_sample, cache_mem_k, cache_mem_v, state_pool, cache_fox_k, cache_fox_v, cache_fox_logf, page_table, mem_prompt, norm_mix_g, norm_ffn_g, norm_mem_g, w_mem_kv, mem_q_norm_g, mem_k_norm_g, w_out, w_in_pool, w_pool_group, pool_scale, w_in_fox, b_forget, fox_q_norm_g, fox_k_norm_g, w_ffn_gu, w_ffn_down, w_router, w_exp_gu, w_exp_down):
    b, s, d = x_prompt.shape
    bs, ls, _ = x_sample.shape
    tp, ts = b * s, bs * ls
    tm_p, tm_s = 512, ts

    xp = x_prompt.reshape(tp, d)
    xs = x_sample.reshape(ts, d)

    mem_kt_p, mem_vt_p = _memkv(mem_prompt, norm_mem_g, w_mem_kv, mem_k_norm_g)
    mem_kt_s = cache_mem_k.transpose(0, 1, 3, 4, 2).reshape(2, bs, MEM_W, MEM_LEN)
    mem_vt_s = cache_mem_v.transpose(0, 1, 3, 4, 2).reshape(2, bs, MEM_W, MEM_LEN)

    def mem_attend_both(qm_p, qm_s, layer):
        mo_p = _mem_attend(qm_p.reshape(b, s, MEM_W), mem_kt_p[layer], mem_vt_p[layer],
                           mem_q_norm_g[layer], 512).reshape(tp, MEM_W)
        qs8 = jnp.pad(qm_s.reshape(bs, ls, MEM_W), ((0, 0), (0, SAMPLE_ROWS - ls), (0, 0)))
        mo_s = _mem_attend(qs8, mem_kt_s[layer], mem_vt_s[layer], mem_q_norm_g[layer], SAMPLE_ROWS)
        return mo_p, mo_s[:, :ls].reshape(ts, MEM_W)

    g_mix0 = norm_mix_g[0][None, :]
    up, qmp = _pool_proj(xp, g_mix0, w_in_pool[0], tm_p)
    us, qms = _pool_proj(xs, g_mix0, w_in_pool[0], tm_s)
    w_bd = _block_diag(w_pool_group[0])
    pscale = pool_scale[0][None, :]
    mix_p = _pool_prompt(up.reshape(b, s, MIX_W), w_bd, pscale, 512).reshape(tp, MIX_W)
    us_t = us.reshape(bs, ls, MIX_W).transpose(1, 0, 2)
    mix_s_t, new_state_t = _pool_sample(state_pool[0].transpose(1, 0, 2), us_t, w_bd, pscale)
    mix_s = mix_s_t.transpose(1, 0, 2).reshape(ts, MIX_W)
    pool_p = up.reshape(b, s, MIX_W)[:, s - POOL_STATE:][None]
    pool_s = new_state_t.transpose(1, 0, 2)[None]

    mo_p, mo_s = mem_attend_both(qmp, qms, 0)
    xp = _out_proj(xp, mix_p, mo_p, w_out[0], tm_p)
    xs = _out_proj(xs, mix_s, mo_s, w_out[0], tm_s)
    g_ffn0 = norm_ffn_g[0][None, :]
    xp = _ffn(xp, g_ffn0, w_ffn_gu[0], w_ffn_down[0], 1024)
    xs = _ffn(xs, g_ffn0, w_ffn_gu[0], w_ffn_down[0], tm_s)

    w_in = w_in_fox[0]
    wq = w_in[:, :MIX_W]
    wk = w_in[:, MIX_W:2 * MIX_W]
    wv = w_in[:, 2 * MIX_W:3 * MIX_W]
    wf = jnp.pad(w_in[:, 3 * MIX_W:3 * MIX_W + FOX_HEADS], ((0, 0), (0, LANES - FOX_HEADS)))
    wqm = w_in[:, 3 * MIX_W + FOX_HEADS:]
    bf_row = jnp.pad(b_forget[0], (0, LANES - FOX_HEADS))[None, :]
    qg_t = jnp.tile(fox_q_norm_g[0], FOX_HEADS)[None, :]
    kg_t = jnp.tile(fox_k_norm_g[0], FOX_HEADS)[None, :]
    g_mix1 = norm_mix_g[1][None, :]
    proj_w = (g_mix1, wq, wk, wv, wqm, wf, bf_row, qg_t, kg_t)
    qpp, kpp, ktp, vtp, lftp, qmp = _fox_proj_prompt(xp, *proj_w, tm=tm_p, n_seq=b)
    qs, kns, vns, lfts, qms = _fox_proj_sample(xs, *proj_w)

    mix_pt = _fox_flash(qpp, kpp, vtp, 512)

    pad_rows = ((0, 0), (0, SAMPLE_ROWS - ls), (0, 0))
    q8 = jnp.pad(qs.reshape(bs, ls, MIX_W), pad_rows)
    kn8 = jnp.pad(kns.reshape(bs, ls, MIX_W), pad_rows)
    vn8 = jnp.pad(vns.reshape(bs, ls, MIX_W), pad_rows)
    lfn = jnp.pad(lfts.reshape(_FL_ROWS, bs, ls).transpose(1, 0, 2),
                  ((0, 0), (0, 0), (0, SAMPLE_ROWS - ls)))
    n_phys = cache_fox_k.shape[1]
    kt_pages = cache_fox_k[0].transpose(0, 2, 3, 1).reshape(n_phys, MIX_W, PAGE_SIZE)
    vt_pages = cache_fox_v[0].transpose(0, 2, 3, 1).reshape(n_phys, MIX_W, PAGE_SIZE)
    lf_heads = cache_fox_logf[0].transpose(2, 0, 1)
    mix_s = _fox_decode(page_table, kt_pages, vt_pages, lf_heads, q8, kn8, vn8, lfn, ls)
    mix_s = mix_s[:, :ls].reshape(ts, MIX_W)

    mo_p, mo_s = mem_attend_both(qmp, qms, 1)
    xp = _out_proj(xp, mix_pt, mo_p, w_out[1], tm_p, mix_transposed=True)
    xs = _out_proj(xs, mix_s, mo_s, w_out[1], tm_s)

    g_ffn1 = norm_ffn_g[1][None, :]
    wrt = w_router[0].T
    hp, rp, cnt_p = _router(xp, g_ffn1, wrt, tm_p)
    hs, rs, cnt_s = _router(xs, g_ffn1, wrt, tm_s)
    cnt_p = cnt_p[:, 0].astype(jnp.int32)
    cnt_s = cnt_s[:, 0].astype(jnp.int32)
    counts = cnt_p + cnt_s
    offsets = jnp.cumsum(counts) - counts
    n_sorted = 2 * (tp + ts)

    def dests(r, base):
        i1, i2 = r[0].astype(jnp.int32), r[1].astype(jnp.int32)
        return base[i1] + r[2].astype(jnp.int32), base[i2] + r[3].astype(jnp.int32)

    d1p, d2p = dests(rp, offsets)
    d1s, d2s = dests(rs, offsets + cnt_p)
    x_sorted = _scatter_rows(jnp.concatenate([d1p, d1s]), jnp.concatenate([d2p, d2s]), hp, hs, n_sorted)
    o_sorted = _gmm(_gmm_meta(counts, n_sorted), x_sorted, w_exp_gu[0], w_exp_down[0])
    yp = _combine(d1p, d2p, xp, rp[4:6].T, o_sorted, 256)
    ys = _combine(d1s, d2s, xs, rs[4:6].T, o_sorted, ts)

    def heads_t(a_t, n_b, n_h):
        return a_t.reshape(n_b, n_h, HEAD_DIM, a_t.shape[-1]).transpose(0, 3, 1, 2)

    mem_k_p = jnp.stack([heads_t(mem_kt_p[i], b, MEM_HEADS) for i in range(2)])
    mem_v_p = jnp.stack([heads_t(mem_vt_p[i], b, MEM_HEADS) for i in range(2)])
    fk_p = heads_t(ktp, b, FOX_HEADS)[None]
    fv_p = heads_t(vtp, b, FOX_HEADS)[None]
    fl_p = lftp[:FOX_HEADS].reshape(FOX_HEADS, b, s).transpose(1, 2, 0)[None]
    fk_s = kns.reshape(1, bs, ls, FOX_HEADS, HEAD_DIM)
    fv_s = vns.reshape(1, bs, ls, FOX_HEADS, HEAD_DIM)
    fl_s = lfts[:FOX_HEADS].reshape(FOX_HEADS, bs, ls).transpose(1, 2, 0)[None]
    return (yp.reshape(b, s, d), ys.reshape(bs, ls, d), mem_k_p, mem_v_p, pool_p, pool_s,
            fk_p, fv_p, fl_p, fk_s, fv_s, fl_s)
```

```python
import functools
import math

import jax
import jax.numpy as jnp
from jax import lax
from jax.experimental import pallas as pl
from jax.experimental.pallas import tpu as pltpu

F32 = jnp.float32
BF16 = jnp.bfloat16

D_MODEL = 1024
HEAD_DIM = 64
MEM_LEN = 256
MEM_HEADS = 4
MEM_W = MEM_HEADS * HEAD_DIM
MIX_W = D_MODEL - MEM_W
FOX_HEADS = MIX_W // HEAD_DIM
POOL_WINDOWS = (2, 4, 8, 16)
POOL_GW = MIX_W // len(POOL_WINDOWS)
POOL_STATE = max(POOL_WINDOWS) - 1
D_FF = 2816
N_EXPERTS = 8
PAGE_SIZE = 128
EPS = 1e-6
SCALE = HEAD_DIM ** -0.5
LOG2E = math.log2(math.e)

LANES = 128
SUBLANES = 8
MXU_N = 256
VMEM_LIMIT = 56 * 1024 * 1024

FF_CHUNK = MXU_N
N_FF_CHUNKS = D_FF // FF_CHUNK
MOE_TM = 768
PAGES_PER_STEP = 8
SAMPLE_ROWS = 8
DMA_UNROLL = 8
DECODE_SLOTS = 3
FLASH_HEADS = 6
MOE_FF_SPLIT = 2
MOE_SUB_ROWS = 256


def _cparams(*sem, vmem=None):
    return pltpu.CompilerParams(dimension_semantics=sem, vmem_limit_bytes=vmem)


def _dot(a, b):
    return jnp.dot(a, b, preferred_element_type=F32)


def _dot_nt(a, b):
    return lax.dot_general(a, b, (((1,), (1,)), ((), ())), preferred_element_type=F32)


def _rms_rows(x, g):
    return x * lax.rsqrt(jnp.mean(x * x, axis=-1, keepdims=True) + EPS) * g


def _split3(x):
    hi = x.astype(BF16)
    r = x - hi.astype(F32)
    mid = r.astype(BF16)
    lo = (r - mid.astype(F32)).astype(BF16)
    return hi, mid, lo


def _head_sumsq_lanes(x):
    r = lax.broadcasted_iota(jnp.int32, (LANES, LANES), 0) // HEAD_DIM
    c = lax.broadcasted_iota(jnp.int32, (LANES, LANES), 1) // HEAD_DIM
    ones_bd = (r == c).astype(BF16)
    xx = x * x
    hi = xx.astype(BF16)
    lo = (xx - hi.astype(F32)).astype(BF16)
    parts = []
    for j in range(x.shape[1] // LANES):
        sl = slice(j * LANES, (j + 1) * LANES)
        parts.append(_dot(hi[:, sl], ones_bd) + _dot(lo[:, sl], ones_bd))
    return jnp.concatenate(parts, axis=1)


def _head_rms_lanes(x, g_tiled):
    ssq = _head_sumsq_lanes(x)
    return x * lax.rsqrt(ssq * (1.0 / HEAD_DIM) + EPS) * g_tiled


def _head_rms_rows_t(xt, g_col, n_heads):
    outs = []
    for h in range(n_heads):
        blk = xt[h * HEAD_DIM:(h + 1) * HEAD_DIM, :]
        ms = jnp.mean(blk * blk, axis=0, keepdims=True)
        outs.append(blk * lax.rsqrt(ms + EPS) * g_col)
    return outs


def _log_sigmoid(x):
    return jnp.minimum(x, 0.0) - jnp.log1p(jnp.exp(-jnp.abs(x)))


def _lane_head(shape, axis):
    return lax.broadcasted_iota(jnp.int32, shape, axis) // HEAD_DIM


def _memkv_body(mem_ref, gm_ref, w_ref, kg_ref, kt_ref, vt_ref):
    h = _rms_rows(mem_ref[0], gm_ref[0])
    z = _dot(h, w_ref[0])
    kt = z[:, :MEM_W].T
    vt_ref[0, 0] = z[:, MEM_W:].T
    pieces = _head_rms_rows_t(kt, kg_ref[0], MEM_HEADS)
    for h_i, p in enumerate(pieces):
        kt_ref[0, 0, h_i * HEAD_DIM:(h_i + 1) * HEAD_DIM, :] = p


def _memkv(mem, g_mem, w_kv, kn_g):
    depth, batch = w_kv.shape[0], mem.shape[0]
    out = jax.ShapeDtypeStruct((depth, batch, MEM_W, MEM_LEN), F32)
    return pl.pallas_call(
        _memkv_body,
        grid=(depth, batch),
        in_specs=[
            pl.BlockSpec((1, MEM_LEN, D_MODEL), lambda i, b: (b, 0, 0)),
            pl.BlockSpec((1, 1, D_MODEL), lambda i, b: (i, 0, 0)),
            pl.BlockSpec((1, D_MODEL, 2 * MEM_W), lambda i, b: (i, 0, 0)),
            pl.BlockSpec((1, HEAD_DIM, 1), lambda i, b: (i, 0, 0)),
        ],
        out_specs=[pl.BlockSpec((1, 1, MEM_W, MEM_LEN), lambda i, b: (i, b, 0, 0))] * 2,
        out_shape=[out, out],
        compiler_params=_cparams("arbitrary", "arbitrary"),
        name="memkv",
    )(mem, g_mem[:, None, :], w_kv, kn_g[:, :, None])


def _pool_proj_body(x_ref, g_ref, w_ref, u_ref, qm_ref):
    h = _rms_rows(x_ref[...], g_ref[...])
    u_ref[...] = _dot(h, w_ref[:, :MIX_W])
    qm_ref[...] = _dot(h, w_ref[:, MIX_W:])


def _pool_proj(x, g, w, tm):
    t = x.shape[0]
    return pl.pallas_call(
        _pool_proj_body,
        grid=(t // tm,),
        in_specs=[
            pl.BlockSpec((tm, D_MODEL), lambda i: (i, 0)),
            pl.BlockSpec((1, D_MODEL), lambda i: (0, 0)),
            pl.BlockSpec((D_MODEL, D_MODEL), lambda i: (0, 0)),
        ],
        out_specs=[pl.BlockSpec((tm, MIX_W), lambda i: (i, 0)),
                   pl.BlockSpec((tm, MEM_W), lambda i: (i, 0))],
        out_shape=[jax.ShapeDtypeStruct((t, MIX_W), F32), jax.ShapeDtypeStruct((t, MEM_W), F32)],
        compiler_params=_cparams("arbitrary", vmem=VMEM_LIMIT),
        name="pool_proj",
    )(x, g, w)


def _window_of_lane(shape):
    lane = lax.broadcasted_iota(jnp.int32, shape, len(shape) - 1)
    return jnp.where(lane < POOL_GW, POOL_WINDOWS[0],
                     jnp.where(lane < 2 * POOL_GW, POOL_WINDOWS[1],
                               jnp.where(lane < 3 * POOL_GW, POOL_WINDOWS[2], POOL_WINDOWS[3])))


def _pool_select(s2, s4, s8, s16, shape):
    lane = lax.broadcasted_iota(jnp.int32, shape, len(shape) - 1)
    return jnp.where(lane < POOL_GW, s2,
                     jnp.where(lane < 2 * POOL_GW, s4, jnp.where(lane < 3 * POOL_GW, s8, s16)))


_POOL_PAD = SUBLANES
_POOL_HALO = 2 * SUBLANES
_POOL_BASE = _POOL_PAD + _POOL_HALO


def _pool_mix_tile(u, li, tl, e_ref, s2_ref, s4_ref, s8_ref, w_bd, scale):
    n = _POOL_HALO + tl

    @pl.when(li == 0)
    def _():
        e_ref[0:_POOL_BASE, :] = jnp.zeros((_POOL_BASE, MIX_W), F32)
        s2_ref[0:_POOL_PAD, :] = jnp.zeros((_POOL_PAD, MIX_W), F32)
        s4_ref[0:_POOL_PAD, :] = jnp.zeros((_POOL_PAD, MIX_W), F32)
        s8_ref[0:_POOL_PAD, :] = jnp.zeros((_POOL_PAD, MIX_W), F32)

    e_ref[_POOL_BASE:_POOL_BASE + tl, :] = u
    s2 = e_ref[_POOL_PAD:_POOL_PAD + n, :] + e_ref[_POOL_PAD - 1:_POOL_PAD - 1 + n, :]
    s2_ref[_POOL_PAD:_POOL_PAD + n, :] = s2
    s4 = s2 + s2_ref[_POOL_PAD - 2:_POOL_PAD - 2 + n, :]
    s4_ref[_POOL_PAD:_POOL_PAD + n, :] = s4
    s8 = s4 + s4_ref[_POOL_PAD - 4:_POOL_PAD - 4 + n, :]
    s8_ref[_POOL_PAD:_POOL_PAD + n, :] = s8
    s16 = s8[_POOL_HALO:, :] + s8_ref[_POOL_BASE - 8:_POOL_BASE - 8 + tl, :]
    shape = (tl, MIX_W)
    ssel = _pool_select(s2[_POOL_HALO:, :], s4[_POOL_HALO:, :], s8[_POOL_HALO:, :], s16, shape)
    pos = li * tl + lax.broadcasted_iota(jnp.int32, shape, 0)
    cnt = jnp.minimum(_window_of_lane(shape), pos + 1).astype(F32)
    diff = ssel / cnt - u
    mix = _dot(diff, w_bd) * scale
    e_ref[_POOL_PAD:_POOL_BASE, :] = e_ref[_POOL_PAD + tl:_POOL_BASE + tl, :]
    return mix


def _pool_layer_prompt_body(x_ref, g_ref, win_ref, wbd_ref, sc_ref, kt_ref, vt_ref, qg_ref, wo_ref,
                            o_ref, st_ref, e_ref, s2_ref, s4_ref, s8_ref, *, tl):
    x = x_ref[0]
    h = _rms_rows(x, g_ref[...])
    u = _dot(h, win_ref[:, :MIX_W])
    qm = _dot(h, win_ref[:, MIX_W:])
    st_ref[0] = u[tl - _POOL_HALO:, :]
    mix = _pool_mix_tile(u, pl.program_id(1), tl, e_ref, s2_ref, s4_ref, s8_ref, wbd_ref[...], sc_ref[...])
    mem = _mem_attend_tile(qm, kt_ref[0], vt_ref[0], qg_ref[...])
    o_ref[0] = x + _dot(mix, wo_ref[:MIX_W, :]) + _dot(mem, wo_ref[MIX_W:, :])


def _pool_layer_prompt(x, g, w_in, w_bd, scale, mem_kt, mem_vt, qn_g, w_out, tl):
    b, s, _ = x.shape
    rows = _POOL_BASE + tl
    const = lambda shape: pl.BlockSpec(shape, lambda bi, li: tuple(0 for _ in shape))
    per_b = lambda shape: pl.BlockSpec(shape, lambda bi, li: (bi, 0, 0))
    return pl.pallas_call(
        functools.partial(_pool_layer_prompt_body, tl=tl),
        grid=(b, s // tl),
        in_specs=[
            pl.BlockSpec((1, tl, D_MODEL), lambda bi, li: (bi, li, 0)),
            const((1, D_MODEL)), const((D_MODEL, D_MODEL)), const((MIX_W, MIX_W)), const((1, MIX_W)),
            per_b((1, MEM_W, MEM_LEN)), per_b((1, MEM_W, MEM_LEN)), const((1, MEM_W)),
            const((D_MODEL, D_MODEL)),
        ],
        out_specs=[pl.BlockSpec((1, tl, D_MODEL), lambda bi, li: (bi, li, 0)),
                   per_b((1, _POOL_HALO, MIX_W))],
        out_shape=[jax.ShapeDtypeStruct((b, s, D_MODEL), F32),
                   jax.ShapeDtypeStruct((b, _POOL_HALO, MIX_W), F32)],
        scratch_shapes=[pltpu.VMEM((rows, MIX_W), F32)] * 4,
        compiler_params=_cparams("arbitrary", "arbitrary", vmem=VMEM_LIMIT),
        name="pool_layer_prompt",
    )(x, g, w_in, w_bd, scale, mem_kt, mem_vt, jnp.tile(qn_g, MEM_HEADS)[None, :], w_out)


def _pool_sample_body(st_ref, u_ref, w_ref, sc_ref, o_ref, ns_ref, *, n_new):
    def ext(j):
        return st_ref[j] if j < POOL_STATE else u_ref[j - POOL_STATE]

    for l in range(n_new):
        r = POOL_STATE + l
        s2 = ext(r) + ext(r - 1)
        s4 = s2 + ext(r - 2) + ext(r - 3)
        s8 = s4
        for j in range(4, 8):
            s8 = s8 + ext(r - j)
        s16 = s8
        for j in range(8, 16):
            s16 = s16 + ext(r - j)
        shape = s2.shape
        cnt = _window_of_lane(shape).astype(F32)
        diff = _pool_select(s2, s4, s8, s16, shape) / cnt - ext(r)
        o_ref[l] = _dot(diff, w_ref[...]) * sc_ref[...]
    for j in range(POOL_STATE):
        ns_ref[j] = ext(j + n_new)


def _pool_sample(state_t, u_t, w_bd, scale):
    n_new, b, _ = u_t.shape
    return pl.pallas_call(
        functools.partial(_pool_sample_body, n_new=n_new),
        out_shape=[jax.ShapeDtypeStruct((n_new, b, MIX_W), F32),
                   jax.ShapeDtypeStruct((POOL_STATE, b, MIX_W), F32)],
        name="pool_sample",
    )(state_t, u_t, w_bd, scale)


def _mem_attend_tile(qm, kt, vt, g_tiled):
    q = _head_rms_lanes(qm, g_tiled) * SCALE
    lh = _lane_head(q.shape, 1)
    out = jnp.zeros(q.shape, F32)
    for h in range(MEM_HEADS):
        s = _dot(jnp.where(lh == h, q, 0.0), kt)
        e = jnp.exp(s - jnp.max(s, axis=-1, keepdims=True))
        p = e / jnp.sum(e, axis=-1, keepdims=True)
        out = jnp.where(lh == h, _dot_nt(p, vt), out)
    return out


def _mem_attend_body(q_ref, kt_ref, vt_ref, g_ref, o_ref):
    o_ref[0] = _mem_attend_tile(q_ref[0], kt_ref[0], vt_ref[0], g_ref[...])


def _mem_attend(qm, kt, vt, qn_g, tl):
    b, l, _ = qm.shape
    return pl.pallas_call(
        _mem_attend_body,
        grid=(b, l // tl),
        in_specs=[
            pl.BlockSpec((1, tl, MEM_W), lambda bi, li: (bi, li, 0)),
            pl.BlockSpec((1, MEM_W, MEM_LEN), lambda bi, li: (bi, 0, 0)),
            pl.BlockSpec((1, MEM_W, MEM_LEN), lambda bi, li: (bi, 0, 0)),
            pl.BlockSpec((1, MEM_W), lambda bi, li: (0, 0)),
        ],
        out_specs=pl.BlockSpec((1, tl, MEM_W), lambda bi, li: (bi, li, 0)),
        out_shape=jax.ShapeDtypeStruct((b, l, MEM_W), F32),
        compiler_params=_cparams("arbitrary", "arbitrary"),
        name="mem_attend",
    )(qm, kt, vt, jnp.tile(qn_g, MEM_HEADS)[None, :])


def _out_proj_body(x_ref, mix_ref, mem_ref, w_ref, o_ref, *, mix_transposed):
    mix = mix_ref[0].T if mix_transposed else mix_ref[...]
    o_ref[...] = x_ref[...] + _dot(mix, w_ref[:MIX_W, :]) + _dot(mem_ref[...], w_ref[MIX_W:, :])


def _out_proj(x, mix, mem, w, tm, mix_transposed=False):
    t = x.shape[0]
    if mix_transposed:
        per_seq = mix.shape[2] // tm
        mix_spec = pl.BlockSpec((1, MIX_W, tm), lambda i: (i // per_seq, 0, i % per_seq))
    else:
        mix_spec = pl.BlockSpec((tm, MIX_W), lambda i: (i, 0))
    return pl.pallas_call(
        functools.partial(_out_proj_body, mix_transposed=mix_transposed),
        grid=(t // tm,),
        in_specs=[
            pl.BlockSpec((tm, D_MODEL), lambda i: (i, 0)),
            mix_spec,
            pl.BlockSpec((tm, MEM_W), lambda i: (i, 0)),
            pl.BlockSpec((D_MODEL, D_MODEL), lambda i: (0, 0)),
        ],
        out_specs=pl.BlockSpec((tm, D_MODEL), lambda i: (i, 0)),
        out_shape=jax.ShapeDtypeStruct((t, D_MODEL), F32),
        compiler_params=_cparams("arbitrary", vmem=VMEM_LIMIT),
        name="out_proj",
    )(x, mix, mem, w)


def _silu(g):
    return g / (1.0 + jnp.exp(-g))


def _ffn_body(x_ref, g_ref, wgu_ref, wd_ref, o_ref):
    x = x_ref[...]
    h = _rms_rows(x, g_ref[...])
    o_ref[...] = x
    for c in range(N_FF_CHUNKS):
        gate = slice(c * FF_CHUNK, (c + 1) * FF_CHUNK)
        up = slice(D_FF + c * FF_CHUNK, D_FF + (c + 1) * FF_CHUNK)
        a = _silu(_dot(h, wgu_ref[:, gate])) * _dot(h, wgu_ref[:, up])
        o_ref[...] += _dot(a, wd_ref[gate, :])


def _ffn(x, g, w_gu, w_down, tm):
    t = x.shape[0]
    resident = lambda shape: pl.BlockSpec(shape, lambda i: (0, 0), pipeline_mode=pl.Buffered(1))
    return pl.pallas_call(
        _ffn_body,
        grid=(t // tm,),
        in_specs=[
            pl.BlockSpec((tm, D_MODEL), lambda i: (i, 0)),
            pl.BlockSpec((1, D_MODEL), lambda i: (0, 0)),
            resident((D_MODEL, 2 * D_FF)),
            resident((D_FF, D_MODEL)),
        ],
        out_specs=pl.BlockSpec((tm, D_MODEL), lambda i: (i, 0)),
        out_shape=jax.ShapeDtypeStruct((t, D_MODEL), F32),
        compiler_params=_cparams("arbitrary", vmem=VMEM_LIMIT),
        name="ffn",
    )(x, g, w_gu, w_down)


_FL_ROWS = 2 * SUBLANES
_AUG0 = HEAD_DIM


def _aug_pieces(c_col):
    hi, mid, lo = _split3(c_col)
    bc = lambda a: jnp.broadcast_to(a.astype(F32), (c_col.shape[0], LANES))
    return bc(hi), bc(mid), bc(lo)


def _fox_proj_prompt_body(x_ref, g_ref, wq_ref, wk_ref, wv_ref, wqm_ref, wf_ref, bf_ref, qg_ref, kg_ref,
                          qp_ref, kp_ref, kt_ref, vt_ref, lft_ref, qm_ref, carry_ref, *, tm, per_seq):
    @pl.when(pl.program_id(0) % per_seq == 0)
    def _():
        carry_ref[...] = jnp.zeros(carry_ref.shape, F32)

    h = _rms_rows(x_ref[...], g_ref[...])
    q = _head_rms_lanes(_dot(h, wq_ref[...]), qg_ref[...]) * (SCALE * LOG2E)
    k = _head_rms_lanes(_dot(h, wk_ref[...]), kg_ref[...])
    v = _dot(h, wv_ref[...])
    qm_ref[...] = _dot(h, wqm_ref[...])
    kt_ref[0] = k.T
    vt_ref[0] = v.T
    lane = lax.broadcasted_iota(jnp.int32, (tm, LANES), 1)
    lf = jnp.where(lane < FOX_HEADS, _log_sigmoid(_dot(h, wf_ref[...]) + bf_ref[...]), 0.0)
    lft_ref[...] = lf.T[:_FL_ROWS, :]
    r = lax.broadcasted_iota(jnp.int32, (tm, tm), 0)
    c = lax.broadcasted_iota(jnp.int32, (tm, tm), 1)
    lower = (c <= r).astype(BF16)
    hi, mid, lo = _split3(lf)
    csum = _dot(lower, hi) + _dot(lower, mid) + _dot(lower, lo) + carry_ref[0:1, :]
    carry_ref[...] = jnp.broadcast_to(csum[tm - 1:tm, :], carry_ref.shape)
    c2 = csum * LOG2E
    one = jnp.ones((tm, LANES), F32)
    zero = jnp.zeros((tm, LANES), F32)
    for hd in range(FOX_HEADS):
        chi, cmid, clo = _aug_pieces(c2[:, hd:hd + 1])
        aug_q = jnp.where(lane == _AUG0, chi, jnp.where(lane == _AUG0 + 1, cmid, jnp.where(
            lane == _AUG0 + 2, clo, jnp.where(lane < _AUG0 + 6, one, zero))))
        aug_k = jnp.where(lane < _AUG0 + 3, one, jnp.where(lane == _AUG0 + 3, -chi, jnp.where(
            lane == _AUG0 + 4, -cmid, jnp.where(lane == _AUG0 + 5, -clo, zero))))
        col = slice((hd // 2) * LANES, (hd // 2 + 1) * LANES)
        qc, kc = q[:, col], k[:, col]
        if hd % 2:
            qc = pltpu.roll(qc, HEAD_DIM, 1)
            kc = pltpu.roll(kc, HEAD_DIM, 1)
        qp_ref[0, hd] = jnp.where(lane < HEAD_DIM, qc, aug_q)
        kp_ref[0, hd] = jnp.where(lane < HEAD_DIM, kc, aug_k)


def _fox_proj_prompt(x, g, wq, wk, wv, wqm, wf, bf_row, qg_t, kg_t, tm, n_seq):
    t = x.shape[0]
    seq = t // n_seq
    per_seq = seq // tm

    def const(shape):
        return pl.BlockSpec(shape, lambda i: tuple(0 for _ in shape))

    heads_spec = pl.BlockSpec((1, FOX_HEADS, tm, LANES), lambda i: (i // per_seq, 0, i % per_seq, 0))
    t_spec = pl.BlockSpec((1, MIX_W, tm), lambda i: (i // per_seq, 0, i % per_seq))
    return pl.pallas_call(
        functools.partial(_fox_proj_prompt_body, tm=tm, per_seq=per_seq),
        grid=(t // tm,),
        in_specs=[
            pl.BlockSpec((tm, D_MODEL), lambda i: (i, 0)),
            const((1, D_MODEL)),
            const((D_MODEL, MIX_W)), const((D_MODEL, MIX_W)), const((D_MODEL, MIX_W)),
            const((D_MODEL, MEM_W)), const((D_MODEL, LANES)),
            const((1, LANES)), const((1, MIX_W)), const((1, MIX_W)),
        ],
        out_specs=[heads_spec, heads_spec, t_spec, t_spec,
                   pl.BlockSpec((_FL_ROWS, tm), lambda i: (0, i)),
                   pl.BlockSpec((tm, MEM_W), lambda i: (i, 0))],
        out_shape=[
            jax.ShapeDtypeStruct((n_seq, FOX_HEADS, seq, LANES), F32),
            jax.ShapeDtypeStruct((n_seq, FOX_HEADS, seq, LANES), F32),
            jax.ShapeDtypeStruct((n_seq, MIX_W, seq), F32),
            jax.ShapeDtypeStruct((n_seq, MIX_W, seq), F32),
            jax.ShapeDtypeStruct((_FL_ROWS, t), F32),
            jax.ShapeDtypeStruct((t, MEM_W), F32),
        ],
        scratch_shapes=[pltpu.VMEM((SUBLANES, LANES), F32)],
        compiler_params=_cparams("arbitrary", vmem=VMEM_LIMIT),
        name="fox_proj_prompt",
    )(x, g, wq, wk, wv, wqm, wf, bf_row, qg_t, kg_t)


def _fox_proj_sample_body(x_ref, g_ref, wq_ref, wk_ref, wv_ref, wqm_ref, wf_ref, bf_ref, qg_ref, kg_ref,
                          q_ref, k_ref, v_ref, lft_ref, qm_ref):
    h = _rms_rows(x_ref[...], g_ref[...])
    q_ref[...] = _head_rms_lanes(_dot(h, wq_ref[...]), qg_ref[...])
    k_ref[...] = _head_rms_lanes(_dot(h, wk_ref[...]), kg_ref[...])
    v_ref[...] = _dot(h, wv_ref[...])
    qm_ref[...] = _dot(h, wqm_ref[...])
    lane = lax.broadcasted_iota(jnp.int32, (x_ref.shape[0], LANES), 1)
    lf = jnp.where(lane < FOX_HEADS, _log_sigmoid(_dot(h, wf_ref[...]) + bf_ref[...]), 0.0)
    lft_ref[...] = lf.T[:_FL_ROWS, :]


def _fox_proj_sample(x, g, wq, wk, wv, wqm, wf, bf_row, qg_t, kg_t):
    t = x.shape[0]
    rows = jax.ShapeDtypeStruct((t, MIX_W), F32)
    return pl.pallas_call(
        _fox_proj_sample_body,
        out_shape=[rows, rows, rows, jax.ShapeDtypeStruct((_FL_ROWS, t), F32),
                   jax.ShapeDtypeStruct((t, MEM_W), F32)],
        compiler_params=_cparams(vmem=VMEM_LIMIT),
        name="fox_proj_sample",
    )(x, g, wq, wk, wv, wqm, wf, bf_row, qg_t, kg_t)


def _fox_flash_body(qi_ref, ki_ref, qp_ref, kp_ref, vt_ref, o_ref, m_ref, l_ref, acc_ref, st_ref, *, blk):
    p = pl.program_id(2)
    qi = qi_ref[p]
    ki = ki_ref[p]

    @pl.when(ki == 0)
    def _():
        m_ref[...] = jnp.full(m_ref.shape, -jnp.inf, F32)
        l_ref[...] = jnp.zeros(l_ref.shape, F32)
        acc_ref[...] = jnp.zeros(acc_ref.shape, F32)

    def step(diagonal):
        for hh in range(FLASH_HEADS):
            pair = hh // 2
            vt = vt_ref[0, pair * LANES:(pair + 1) * LANES, :]
            st = _dot_nt(kp_ref[0, hh], qp_ref[0, hh])
            if diagonal:
                key = lax.broadcasted_iota(jnp.int32, st.shape, 0)
                qry = lax.broadcasted_iota(jnp.int32, st.shape, 1)
                st = jnp.where(key <= qry, st, -jnp.inf)
            st_ref[hh % 2] = st
            st = st_ref[hh % 2]
            m_prev = m_ref[hh]
            m_new = jnp.maximum(m_prev, jnp.max(st, axis=0, keepdims=True))
            alpha = jnp.exp2(m_prev - m_new)
            pt = jnp.exp2(st - m_new)
            l_ref[hh] = alpha * l_ref[hh] + jnp.sum(pt, axis=0, keepdims=True)
            acc_ref[hh] = alpha * acc_ref[hh] + _dot(vt, pt)
            m_ref[hh] = m_new

    @pl.when(ki < qi)
    def _():
        step(False)

    @pl.when(ki == qi)
    def _():
        step(True)
        row = lax.broadcasted_iota(jnp.int32, (LANES, blk), 0)
        for pair in range(FLASH_HEADS // 2):
            a, c = 2 * pair, 2 * pair + 1
            o_ref[0, pair * LANES:(pair + 1) * LANES, :] = jnp.where(
                row < HEAD_DIM, acc_ref[a] / l_ref[a], acc_ref[c] / l_ref[c])


def _fox_flash(qp, kp, vt, blk):
    b, _, s, _ = qp.shape
    n = s // blk
    pairs = [(qi, ki) for qi in range(n) for ki in range(qi + 1)]
    qi_tab = jnp.asarray([p[0] for p in pairs], jnp.int32)
    ki_tab = jnp.asarray([p[1] for p in pairs], jnp.int32)
    fh = FLASH_HEADS
    ch = fh // 2 * LANES
    grid_spec = pltpu.PrefetchScalarGridSpec(
        num_scalar_prefetch=2,
        grid=(b, FOX_HEADS // fh, len(pairs)),
        in_specs=[
            pl.BlockSpec((1, fh, blk, LANES), lambda bi, hg, p, qt, kt: (bi, hg, qt[p], 0)),
            pl.BlockSpec((1, fh, blk, LANES), lambda bi, hg, p, qt, kt: (bi, hg, kt[p], 0)),
            pl.BlockSpec((1, ch, blk), lambda bi, hg, p, qt, kt: (bi, hg, kt[p])),
        ],
        out_specs=pl.BlockSpec((1, ch, blk), lambda bi, hg, p, qt, kt: (bi, hg, qt[p])),
        scratch_shapes=[pltpu.VMEM((fh, 1, blk), F32), pltpu.VMEM((fh, 1, blk), F32),
                        pltpu.VMEM((fh, LANES, blk), F32), pltpu.VMEM((2, blk, blk), F32)],
    )
    return pl.pallas_call(
        functools.partial(_fox_flash_body, blk=blk),
        grid_spec=grid_spec,
        out_shape=jax.ShapeDtypeStruct((b, MIX_W, s), F32),
        compiler_params=_cparams("arbitrary", "arbitrary", "arbitrary", vmem=VMEM_LIMIT),
        name="fox_flash",
    )(qi_tab, ki_tab, qp, kp, vt)


_QROWS = FOX_HEADS * SAMPLE_ROWS
_CHUNK = PAGES_PER_STEP * PAGE_SIZE


def _expand_heads(x):
    n = x.shape[1]
    return jnp.broadcast_to(x[:FOX_HEADS, None, :], (FOX_HEADS, SAMPLE_ROWS, n)).reshape(_QROWS, n)


def _page_copies(pt_ref, kt_hbm, vt_hbm, lf_hbm, kt_buf, vt_buf, lf_buf, sem, g, slot, steps, n_pages):
    bi = g // steps
    first = n_pages - (g % steps + 1) * PAGES_PER_STEP
    copies = []
    for i in range(PAGES_PER_STEP):
        page = pt_ref[bi, first + i]
        lanes = pl.ds(i * PAGE_SIZE, PAGE_SIZE)
        copies.append(pltpu.make_async_copy(kt_hbm.at[page], kt_buf.at[slot, :, lanes], sem.at[slot, 0]))
        copies.append(pltpu.make_async_copy(vt_hbm.at[page], vt_buf.at[slot, :, lanes], sem.at[slot, 1]))
        copies.append(pltpu.make_async_copy(lf_hbm.at[:, page], lf_buf.at[slot, i, pl.ds(0, FOX_HEADS)],
                                            sem.at[slot, 2]))
    return copies


def _fox_decode_body(pt_ref, kt_hbm, vt_hbm, lf_hbm, q_ref, kn_ref, vn_ref, lfn_ref, o_ref,
                     kt_buf, vt_buf, lf_buf, sem, qbd_ref, m_ref, l_ref, acc_ref, carry_ref, crow_ref,
                     *, n_new, steps, n_pages):
    g = pl.program_id(0)
    n_steps = pl.num_programs(0)
    j = g % steps
    slot = g % DECODE_SLOTS
    copies = functools.partial(_page_copies, pt_ref, kt_hbm, vt_hbm, lf_hbm, kt_buf, vt_buf, lf_buf, sem,
                               steps=steps, n_pages=n_pages)
    row_l = lax.broadcasted_iota(jnp.int32, (_QROWS, 1), 0) % SAMPLE_ROWS

    @pl.when(g == 0)
    def _():
        lf_buf[...] = jnp.zeros(lf_buf.shape, F32)
        for ahead in range(DECODE_SLOTS - 1):
            for cp in copies(g=ahead, slot=ahead):
                cp.start()

    @pl.when(g + DECODE_SLOTS - 1 < n_steps)
    def _():
        for cp in copies(g=g + DECODE_SLOTS - 1, slot=(g + DECODE_SLOTS - 1) % DECODE_SLOTS):
            cp.start()

    @pl.when(j == 0)
    def _():
        q = q_ref[0] * SCALE
        lh = _lane_head(q.shape, 1)
        for h in range(FOX_HEADS):
            qbd_ref[h * SAMPLE_ROWS:(h + 1) * SAMPLE_ROWS, :] = jnp.where(lh == h, q, 0.0)
        lfn = lfn_ref[0]
        lane = lax.broadcasted_iota(jnp.int32, lfn.shape, 1)
        c = jnp.zeros(lfn.shape, F32)
        for m in range(n_new):
            cm = jnp.sum(jnp.where(lane <= m, lfn, 0.0), axis=1, keepdims=True)
            c = jnp.where(lane == m, cm, c)
        c_q = _expand_heads(c)
        col = lax.broadcasted_iota(jnp.int32, c_q.shape, 1)
        crow = jnp.sum(jnp.where(col == row_l, c_q, 0.0), axis=1, keepdims=True)
        crow_ref[...] = crow
        s = _dot_nt(qbd_ref[...], kn_ref[0]) + (crow - c_q)
        valid = (col < n_new) & ((col <= row_l) | (row_l >= n_new))
        s = jnp.where(valid, s, -jnp.inf)
        m0 = jnp.max(s, axis=-1, keepdims=True)
        p = jnp.exp(s - m0)
        m_ref[...] = m0
        l_ref[...] = jnp.sum(p, axis=-1, keepdims=True)
        acc_ref[...] = _dot(p, vn_ref[0])
        carry_ref[...] = jnp.zeros(carry_ref.shape, F32)

    for cp in copies(g=g, slot=slot):
        cp.wait()

    r = lax.broadcasted_iota(jnp.int32, (PAGE_SIZE, PAGE_SIZE), 0)
    cc = lax.broadcasted_iota(jnp.int32, (PAGE_SIZE, PAGE_SIZE), 1)
    later = (r > cc).astype(BF16)
    lf3 = lf_buf[slot]
    tot = jnp.sum(lf3, axis=2, keepdims=True)
    hi, mid, lo = _split3(lf3.reshape(PAGES_PER_STEP * _FL_ROWS, PAGE_SIZE))
    d_in = (_dot(hi, later) + _dot(mid, later) + _dot(lo, later)).reshape(PAGES_PER_STEP, _FL_ROWS, PAGE_SIZE)
    after = carry_ref[:, 0:1]
    d_pages = [None] * PAGES_PER_STEP
    for i in reversed(range(PAGES_PER_STEP)):
        d_pages[i] = _expand_heads(d_in[i] + after)
        after = after + tot[i]
    carry_ref[...] = jnp.broadcast_to(after, carry_ref.shape)
    d = jnp.concatenate(d_pages, axis=1)

    vt = vt_buf[slot]
    s = _dot(qbd_ref[...], kt_buf[slot]) + crow_ref[...] + d
    m_prev = m_ref[...]
    m_new = jnp.maximum(m_prev, jnp.max(s, axis=-1, keepdims=True))
    alpha = jnp.exp(m_prev - m_new)
    p = jnp.exp(s - m_new)
    l_ref[...] = alpha * l_ref[...] + jnp.sum(p, axis=-1, keepdims=True)
    acc_ref[...] = alpha * acc_ref[...] + _dot_nt(p, vt)
    m_ref[...] = m_new

    @pl.when(j == steps - 1)
    def _():
        res = acc_ref[...] / l_ref[...]
        lh = _lane_head((SAMPLE_ROWS, MIX_W), 1)
        out = jnp.zeros((SAMPLE_ROWS, MIX_W), F32)
        for h in range(FOX_HEADS):
            out = jnp.where(lh == h, res[h * SAMPLE_ROWS:(h + 1) * SAMPLE_ROWS, :], out)
        o_ref[0] = out


def _fox_decode(page_table, kt_pages, vt_pages, lf_heads, q8, kn8, vn8, lfn, n_new):
    b, n_pages = page_table.shape
    steps = n_pages // PAGES_PER_STEP
    per_b = lambda g, pt: (g // steps, 0, 0)
    any_spec = pl.BlockSpec(memory_space=pl.ANY)
    grid_spec = pltpu.PrefetchScalarGridSpec(
        num_scalar_prefetch=1,
        grid=(b * steps,),
        in_specs=[any_spec, any_spec, any_spec,
                  pl.BlockSpec((1, SAMPLE_ROWS, MIX_W), per_b),
                  pl.BlockSpec((1, SAMPLE_ROWS, MIX_W), per_b),
                  pl.BlockSpec((1, SAMPLE_ROWS, MIX_W), per_b),
                  pl.BlockSpec((1, _FL_ROWS, SAMPLE_ROWS), per_b)],
        out_specs=pl.BlockSpec((1, SAMPLE_ROWS, MIX_W), per_b),
        scratch_shapes=[
            pltpu.VMEM((DECODE_SLOTS, MIX_W, _CHUNK), F32),
            pltpu.VMEM((DECODE_SLOTS, MIX_W, _CHUNK), F32),
            pltpu.VMEM((DECODE_SLOTS, PAGES_PER_STEP, _FL_ROWS, PAGE_SIZE), F32),
            pltpu.SemaphoreType.DMA((DECODE_SLOTS, 3)),
            pltpu.VMEM((_QROWS, MIX_W), F32),
            pltpu.VMEM((_QROWS, 1), F32), pltpu.VMEM((_QROWS, 1), F32),
            pltpu.VMEM((_QROWS, MIX_W), F32),
            pltpu.VMEM((_FL_ROWS, LANES), F32),
            pltpu.VMEM((_QROWS, 1), F32),
        ],
    )
    return pl.pallas_call(
        functools.partial(_fox_decode_body, n_new=n_new, steps=steps, n_pages=n_pages),
        grid_spec=grid_spec,
        out_shape=jax.ShapeDtypeStruct((b, SAMPLE_ROWS, MIX_W), F32),
        compiler_params=_cparams("arbitrary", vmem=VMEM_LIMIT),
        name="fox_decode",
    )(page_table, kt_pages, vt_pages, lf_heads, q8, kn8, vn8, lfn)


def _router_body(x_ref, g_ref, wrt_ref, h_ref, r_ref, cnt_ref, carry_ref, *, tm):
    @pl.when(pl.program_id(0) == 0)
    def _():
        carry_ref[...] = jnp.zeros(carry_ref.shape, F32)

    h = _rms_rows(x_ref[...], g_ref[...])
    h_ref[...] = h
    h_hi = h.astype(BF16)
    h_lo = (h - h_hi.astype(F32)).astype(BF16)
    w = wrt_ref[...]
    w_hi = w.astype(BF16)
    w_lo = (w - w_hi.astype(F32)).astype(BF16)
    lg = _dot_nt(w_hi, h_hi) + _dot_nt(w_hi, h_lo) + _dot_nt(w_lo, h_hi)
    idx = lax.broadcasted_iota(jnp.int32, lg.shape, 0)
    m1 = jnp.max(lg, axis=0, keepdims=True)
    i1 = jnp.min(jnp.where(lg == m1, idx, N_EXPERTS), axis=0, keepdims=True)
    sel1 = idx == i1
    lg2 = jnp.where(sel1, -jnp.inf, lg)
    m2 = jnp.max(lg2, axis=0, keepdims=True)
    i2 = jnp.min(jnp.where(lg2 == m2, idx, N_EXPERTS), axis=0, keepdims=True)
    sel2 = idx == i2
    e = jnp.exp(m2 - m1)
    g1 = 1.0 / (1.0 + e)
    g2 = e / (1.0 + e)
    assign = jnp.where(sel1 | sel2, 1.0, 0.0)
    r = lax.broadcasted_iota(jnp.int32, (tm, tm), 0)
    c = lax.broadcasted_iota(jnp.int32, (tm, tm), 1)
    before = (r < c).astype(BF16)
    rank = _dot(assign.astype(BF16), before) + carry_ref[:, 0:1]
    r1 = jnp.sum(jnp.where(sel1, rank, 0.0), axis=0, keepdims=True)
    r2 = jnp.sum(jnp.where(sel2, rank, 0.0), axis=0, keepdims=True)
    carry = carry_ref[...] + jnp.sum(assign, axis=1, keepdims=True)
    carry_ref[...] = carry
    cnt_ref[...] = carry
    rows = [i1.astype(F32), i2.astype(F32), r1, r2, g1, g2]
    out = jnp.zeros(lg.shape, F32)
    for k, v in enumerate(rows):
        out = jnp.where(idx == k, v, out)
    r_ref[...] = out


def _router(x, g, wrt, tm):
    t = x.shape[0]
    return pl.pallas_call(
        functools.partial(_router_body, tm=tm),
        grid=(t // tm,),
        in_specs=[
            pl.BlockSpec((tm, D_MODEL), lambda i: (i, 0)),
            pl.BlockSpec((1, D_MODEL), lambda i: (0, 0)),
            pl.BlockSpec((N_EXPERTS, D_MODEL), lambda i: (0, 0)),
        ],
        out_specs=[
            pl.BlockSpec((tm, D_MODEL), lambda i: (i, 0)),
            pl.BlockSpec((N_EXPERTS, tm), lambda i: (0, i)),
            pl.BlockSpec((N_EXPERTS, LANES), lambda i: (0, 0)),
        ],
        out_shape=[
            jax.ShapeDtypeStruct((t, D_MODEL), F32),
            jax.ShapeDtypeStruct((N_EXPERTS, t), F32),
            jax.ShapeDtypeStruct((N_EXPERTS, LANES), F32),
        ],
        scratch_shapes=[pltpu.VMEM((N_EXPERTS, LANES), F32)],
        compiler_params=_cparams("arbitrary", vmem=VMEM_LIMIT),
        name="router",
    )(x, g, wrt)


_SCATTER_TM = 128


def _row_copy(src_ref, src_row, dst_ref, dst_row, sem):
    return pltpu.make_async_copy(src_ref.at[pl.ds(src_row, 1)], dst_ref.at[pl.ds(dst_row, 1)], sem)


def _scatter_body(d1_ref, d2_ref, hp_ref, hs_ref, xs_ref, sem, *, n_prompt_tiles, n_tiles):
    i = pl.program_id(0)
    tm = _SCATTER_TM

    def issue(src_ref, base):
        def start(r, carry):
            _row_copy(src_ref, base + r, xs_ref, d1_ref[0, 0, r], sem.at[0]).start(priority=0)
            _row_copy(src_ref, base + r, xs_ref, d2_ref[0, 0, r], sem.at[1]).start(priority=1)
            return carry
        lax.fori_loop(0, tm, start, 0, unroll=DMA_UNROLL)

    def drain_one_step():
        for k in range(2):
            pltpu.make_async_copy(hp_ref.at[pl.ds(0, tm)], xs_ref.at[pl.ds(0, tm)], sem.at[k]).wait()

    @pl.when(i < n_prompt_tiles)
    def _():
        issue(hp_ref, i * tm)

    @pl.when(i >= n_prompt_tiles)
    def _():
        issue(hs_ref, (i - n_prompt_tiles) * tm)

    @pl.when(i > 0)
    def _():
        drain_one_step()

    @pl.when(i == n_tiles - 1)
    def _():
        drain_one_step()


def _scatter_rows(d1, d2, h_p, h_s, n_sorted):
    tm = _SCATTER_TM
    npt = h_p.shape[0] // tm
    nst = h_s.shape[0] // tm
    smem = lambda: pl.BlockSpec((1, 1, tm), lambda i: (i, 0, 0), memory_space=pltpu.SMEM)
    return pl.pallas_call(
        functools.partial(_scatter_body, n_prompt_tiles=npt, n_tiles=npt + nst),
        grid=(npt + nst,),
        in_specs=[
            smem(), smem(),
            pl.BlockSpec(memory_space=pl.ANY),
            pl.BlockSpec(memory_space=pl.ANY),
        ],
        out_specs=pl.BlockSpec(memory_space=pl.ANY),
        out_shape=jax.ShapeDtypeStruct((n_sorted, D_MODEL), F32),
        scratch_shapes=[pltpu.SemaphoreType.DMA((2,))],
        compiler_params=_cparams("arbitrary", vmem=VMEM_LIMIT),
        name="moe_scatter",
    )(d1.reshape(-1, 1, tm), d2.reshape(-1, 1, tm), h_p, h_s)


def _gmm_body(tile_ref, exp_ref, lo_ref, hi_ref, first_ref, x_ref, wg_ref, wu_ref, wd_ref, o_ref):
    del tile_ref, exp_ref
    v = pl.program_id(0)
    c = pl.program_id(1)

    @pl.when((first_ref[v] == 1) & (c == 0))
    def _():
        o_ref[...] = jnp.zeros(o_ref.shape, F32)

    lo = lo_ref[v]
    hi = hi_ref[v]
    whole = (lo == 0) & (hi == MOE_TM)

    def swiglu_into(rows, mask):
        x = x_ref[rows, :]
        for off, n in _MOE_SUBS:
            cols = slice(off, off + n)
            a = _silu(_dot(x, wg_ref[0, :, cols])) * _dot(x, wu_ref[0, :, cols])
            y = _dot(a, wd_ref[0, cols, :])
            o_ref[rows, :] += y if mask is None else jnp.where(mask, y, 0.0)

    @pl.when(whole)
    def _():
        swiglu_into(slice(None), None)

    @pl.when(jnp.logical_not(whole) & (hi > lo))
    def _():
        for r0 in range(0, MOE_TM, MOE_SUB_ROWS):
            @pl.when((lo < r0 + MOE_SUB_ROWS) & (hi > r0))
            def _():
                rows = r0 + lax.broadcasted_iota(jnp.int32, (MOE_SUB_ROWS, 1), 0)
                swiglu_into(slice(r0, r0 + MOE_SUB_ROWS), (rows >= lo) & (rows < hi))


_MOE_FF = D_FF // MOE_FF_SPLIT
_MOE_SUBS = tuple((off, min(MXU_N, _MOE_FF - off)) for off in range(0, _MOE_FF, MXU_N))


def _gmm(meta, xs, w_gu, w_down):
    tile, expert, lo, hi, first = meta
    n_visits = tile.shape[0]
    grid_spec = pltpu.PrefetchScalarGridSpec(
        num_scalar_prefetch=5,
        grid=(n_visits, MOE_FF_SPLIT),
        in_specs=[
            pl.BlockSpec((MOE_TM, D_MODEL), lambda v, c, t, e, *_: (t[v], 0)),
            pl.BlockSpec((1, D_MODEL, _MOE_FF), lambda v, c, t, e, *_: (e[v], 0, c)),
            pl.BlockSpec((1, D_MODEL, _MOE_FF), lambda v, c, t, e, *_: (e[v], 0, MOE_FF_SPLIT + c)),
            pl.BlockSpec((1, _MOE_FF, D_MODEL), lambda v, c, t, e, *_: (e[v], c, 0)),
        ],
        out_specs=pl.BlockSpec((MOE_TM, D_MODEL), lambda v, c, t, e, *_: (t[v], 0)),
    )
    return pl.pallas_call(
        _gmm_body,
        grid_spec=grid_spec,
        out_shape=jax.ShapeDtypeStruct(xs.shape, F32),
        compiler_params=_cparams("arbitrary", "arbitrary", vmem=VMEM_LIMIT),
        name="moe_gmm",
    )(tile, expert, lo, hi, first, xs, w_gu, w_gu, w_down)


def _gmm_meta(counts, n_sorted):
    n_tiles = n_sorted // MOE_TM
    n_visits = n_tiles + N_EXPERTS - 1
    ends = jnp.cumsum(counts)
    starts = ends - counts
    first_tile = starts // MOE_TM
    last_tile = jnp.maximum(ends - 1, 0) // MOE_TM
    nv = jnp.where(counts > 0, last_tile - first_tile + 1, 0)
    cv = jnp.cumsum(nv)
    v = jnp.arange(n_visits, dtype=jnp.int32)
    total = cv[-1]
    valid = v < total
    vc = jnp.minimum(v, total - 1)
    expert = jnp.sum((cv[None, :] <= vc[:, None]).astype(jnp.int32), axis=1)
    tile = first_tile[expert] + (vc - (cv[expert] - nv[expert]))
    lo = jnp.maximum(starts[expert], tile * MOE_TM) - tile * MOE_TM
    hi = jnp.minimum(ends[expert], (tile + 1) * MOE_TM) - tile * MOE_TM
    lo = jnp.where(valid, lo, 0)
    hi = jnp.where(valid, hi, 0)
    prev_tile = jnp.concatenate([jnp.full((1,), -1, jnp.int32), tile[:-1]])
    first = (valid & (tile != prev_tile)).astype(jnp.int32)
    as_i32 = lambda a: a.astype(jnp.int32)
    return as_i32(tile), as_i32(expert), as_i32(lo), as_i32(hi), first


def _combine_body(d1_ref, d2_ref, x_ref, g_ref, os_ref, y_ref, buf_ref, sem, *, tm, n_tiles):
    i = pl.program_id(0)
    slot = i % 2

    def issue(tile, slot_):
        def body(r, carry):
            _row_copy(os_ref, d1_ref[tile, r], buf_ref.at[slot_, 0], r, sem.at[slot_, 0]).start(priority=0)
            _row_copy(os_ref, d2_ref[tile, r], buf_ref.at[slot_, 1], r, sem.at[slot_, 1]).start(priority=1)
            return carry
        lax.fori_loop(0, tm, body, 0, unroll=DMA_UNROLL)

    @pl.when(i == 0)
    def _():
        issue(0, 0)

    @pl.when(i + 1 < n_tiles)
    def _():
        issue(i + 1, 1 - slot)

    for k in range(2):
        pltpu.make_async_copy(os_ref.at[pl.ds(0, tm)], buf_ref.at[slot, k], sem.at[slot, k]).wait()

    g = g_ref[...]
    y_ref[...] = x_ref[...] + g[:, 0:1] * buf_ref[slot, 0] + g[:, 1:2] * buf_ref[slot, 1]


def _combine(d1, d2, x, gates, o_sorted, tm):
    t = x.shape[0]
    n_tiles = t // tm
    grid_spec = pltpu.PrefetchScalarGridSpec(
        num_scalar_prefetch=2,
        grid=(n_tiles,),
        in_specs=[
            pl.BlockSpec((tm, D_MODEL), lambda i, *_: (i, 0)),
            pl.BlockSpec((tm, 2), lambda i, *_: (i, 0)),
            pl.BlockSpec(memory_space=pl.ANY),
        ],
        out_specs=pl.BlockSpec((tm, D_MODEL), lambda i, *_: (i, 0)),
        scratch_shapes=[pltpu.VMEM((2, 2, tm, D_MODEL), F32), pltpu.SemaphoreType.DMA((2, 2))],
    )
    return pl.pallas_call(
        functools.partial(_combine_body, tm=tm, n_tiles=n_tiles),
        grid_spec=grid_spec,
        out_shape=jax.ShapeDtypeStruct((t, D_MODEL), F32),
        compiler_params=_cparams("arbitrary", vmem=VMEM_LIMIT),
        name="moe_combine",
    )(d1.reshape(n_tiles, tm), d2.reshape(n_tiles, tm), x, gates, o_sorted)


def _block_diag(w_grp):
    g, n, _ = w_grp.shape
    out = jnp.zeros((g * n, g * n), w_grp.dtype)
    for i in range(g):
        out = out.at[i * n:(i + 1) * n, i * n:(i + 1) * n].set(w_grp[i])
    return out


def kernel(x_prompt, x_sample, cache_mem_k, cache_mem_v, state_pool, cache_fox_k, cache_fox_v, cache_fox_logf, page_table, mem_prompt, norm_mix_g, norm_ffn_g, norm_mem_g, w_mem_kv, mem_q_norm_g, mem_k_norm_g, w_out, w_in_pool, w_pool_group, pool_scale, w_in_fox, b_forget, fox_q_norm_g, fox_k_norm_g, w_ffn_gu, w_ffn_down, w_router, w_exp_gu, w_exp_down):
    b, s, d = x_prompt.shape
    bs, ls, _ = x_sample.shape
    tp, ts = b * s, bs * ls
    tm_p, tm_s = 512, ts

    xp = x_prompt.reshape(tp, d)
    xs = x_sample.reshape(ts, d)

    mem_kt_p, mem_vt_p = _memkv(mem_prompt, norm_mem_g, w_mem_kv, mem_k_norm_g)
    mem_kt_s = cache_mem_k.transpose(0, 1, 3, 4, 2).reshape(2, bs, MEM_W, MEM_LEN)
    mem_vt_s = cache_mem_v.transpose(0, 1, 3, 4, 2).reshape(2, bs, MEM_W, MEM_LEN)

    def mem_attend_sample(qm_s, layer):
        qs8 = jnp.pad(qm_s.reshape(bs, ls, MEM_W), ((0, 0), (0, SAMPLE_ROWS - ls), (0, 0)))
        mo_s = _mem_attend(qs8, mem_kt_s[layer], mem_vt_s[layer], mem_q_norm_g[layer], SAMPLE_ROWS)
        return mo_s[:, :ls].reshape(ts, MEM_W)

    g_mix0 = norm_mix_g[0][None, :]
    w_bd = _block_diag(w_pool_group[0])
    pscale = pool_scale[0][None, :]
    xp, up_tail = _pool_layer_prompt(x_prompt, g_mix0, w_in_pool[0], w_bd, pscale, mem_kt_p[0], mem_vt_p[0],
                                     mem_q_norm_g[0], w_out[0], tm_p)
    xp = xp.reshape(tp, d)
    pool_p = up_tail[:, _POOL_HALO - POOL_STATE:][None]
    us, qms = _pool_proj(xs, g_mix0, w_in_pool[0], tm_s)
    us_t = us.reshape(bs, ls, MIX_W).transpose(1, 0, 2)
    mix_s_t, new_state_t = _pool_sample(state_pool[0].transpose(1, 0, 2), us_t, w_bd, pscale)
    mix_s = mix_s_t.transpose(1, 0, 2).reshape(ts, MIX_W)
    pool_s = new_state_t.transpose(1, 0, 2)[None]
    xs = _out_proj(xs, mix_s, mem_attend_sample(qms, 0), w_out[0], tm_s)
    g_ffn0 = norm_ffn_g[0][None, :]
    xp = _ffn(xp, g_ffn0, w_ffn_gu[0], w_ffn_down[0], tm_p)
    xs = _ffn(xs, g_ffn0, w_ffn_gu[0], w_ffn_down[0], tm_s)

    w_in = w_in_fox[0]
    wq = w_in[:, :MIX_W]
    wk = w_in[:, MIX_W:2 * MIX_W]
    wv = w_in[:, 2 * MIX_W:3 * MIX_W]
    wf = jnp.pad(w_in[:, 3 * MIX_W:3 * MIX_W + FOX_HEADS], ((0, 0), (0, LANES - FOX_HEADS)))
    wqm = w_in[:, 3 * MIX_W + FOX_HEADS:]
    bf_row = jnp.pad(b_forget[0], (0, LANES - FOX_HEADS))[None, :]
    qg_t = jnp.tile(fox_q_norm_g[0], FOX_HEADS)[None, :]
    kg_t = jnp.tile(fox_k_norm_g[0], FOX_HEADS)[None, :]
    g_mix1 = norm_mix_g[1][None, :]
    proj_w = (g_mix1, wq, wk, wv, wqm, wf, bf_row, qg_t, kg_t)
    qpp, kpp, ktp, vtp, lftp, qmp = _fox_proj_prompt(xp, *proj_w, tm=tm_p, n_seq=b)
    qs, kns, vns, lfts, qms = _fox_proj_sample(xs, *proj_w)

    mix_pt = _fox_flash(qpp, kpp, vtp, 512)

    pad_rows = ((0, 0), (0, SAMPLE_ROWS - ls), (0, 0))
    q8 = jnp.pad(qs.reshape(bs, ls, MIX_W), pad_rows)
    kn8 = jnp.pad(kns.reshape(bs, ls, MIX_W), pad_rows)
    vn8 = jnp.pad(vns.reshape(bs, ls, MIX_W), pad_rows)
    lfn = jnp.pad(lfts.reshape(_FL_ROWS, bs, ls).transpose(1, 0, 2),
                  ((0, 0), (0, 0), (0, SAMPLE_ROWS - ls)))
    n_phys = cache_fox_k.shape[1]
    kt_pages = cache_fox_k[0].transpose(0, 2, 3, 1).reshape(n_phys, MIX_W, PAGE_SIZE)
    vt_pages = cache_fox_v[0].transpose(0, 2, 3, 1).reshape(n_phys, MIX_W, PAGE_SIZE)
    lf_heads = cache_fox_logf[0].transpose(2, 0, 1)
    mix_s = _fox_decode(page_table, kt_pages, vt_pages, lf_heads, q8, kn8, vn8, lfn, ls)
    mix_s = mix_s[:, :ls].reshape(ts, MIX_W)

    mo_p = _mem_attend(qmp.reshape(b, s, MEM_W), mem_kt_p[1], mem_vt_p[1], mem_q_norm_g[1],
                       tm_p).reshape(tp, MEM_W)
    xp = _out_proj(xp, mix_pt, mo_p, w_out[1], tm_p, mix_transposed=True)
    xs = _out_proj(xs, mix_s, mem_attend_sample(qms, 1), w_out[1], tm_s)

    g_ffn1 = norm_ffn_g[1][None, :]
    wrt = w_router[0].T
    hp, rp, cnt_p = _router(xp, g_ffn1, wrt, tm_p)
    hs, rs, cnt_s = _router(xs, g_ffn1, wrt, tm_s)
    cnt_p = cnt_p[:, 0].astype(jnp.int32)
    cnt_s = cnt_s[:, 0].astype(jnp.int32)
    counts = cnt_p + cnt_s
    offsets = jnp.cumsum(counts) - counts
    n_sorted = 2 * (tp + ts)

    def dests(r, base):
        i1, i2 = r[0].astype(jnp.int32), r[1].astype(jnp.int32)
        return base[i1] + r[2].astype(jnp.int32), base[i2] + r[3].astype(jnp.int32)

    d1p, d2p = dests(rp, offsets)
    d1s, d2s = dests(rs, offsets + cnt_p)
    x_sorted = _scatter_rows(jnp.concatenate([d1p, d1s]), jnp.concatenate([d2p, d2s]), hp, hs, n_sorted)
    o_sorted = _gmm(_gmm_meta(counts, n_sorted), x_sorted, w_exp_gu[0], w_exp_down[0])
    yp = _combine(d1p, d2p, xp, rp[4:6].T, o_sorted, 256)
    ys = _combine(d1s, d2s, xs, rs[4:6].T, o_sorted, ts)

    def heads_t(a_t, n_b, n_h):
        return a_t.reshape(n_b, n_h, HEAD_DIM, a_t.shape[-1]).transpose(0, 3, 1, 2)

    mem_k_p = jnp.stack([heads_t(mem_kt_p[i], b, MEM_HEADS) for i in range(2)])
    mem_v_p = jnp.stack([heads_t(mem_vt_p[i], b, MEM_HEADS) for i in range(2)])
    fk_p = heads_t(ktp, b, FOX_HEADS)[None]
    fv_p = heads_t(vtp, b, FOX_HEADS)[None]
    fl_p = lftp[:FOX_HEADS].reshape(FOX_HEADS, b, s).transpose(1, 2, 0)[None]
    fk_s = kns.reshape(1, bs, ls, FOX_HEADS, HEAD_DIM)
    fv_s = vns.reshape(1, bs, ls, FOX_HEADS, HEAD_DIM)
    fl_s = lfts[:FOX_HEADS].reshape(FOX_HEADS, bs, ls).transpose(1, 2, 0)[None]
    return (yp.reshape(b, s, d), ys.reshape(bs, ls, d), mem_k_p, mem_v_p, pool_p, pool_s,
            fk_p, fv_p, fl_p, fk_s, fv_s, fl_s)
```

```python
import functools
import math

import jax
import jax.numpy as jnp
from jax import lax
from jax.experimental import pallas as pl
from jax.experimental.pallas import tpu as pltpu

F32 = jnp.float32
BF16 = jnp.bfloat16

D_MODEL = 1024
HEAD_DIM = 64
MEM_LEN = 256
MEM_HEADS = 4
MEM_W = MEM_HEADS * HEAD_DIM
MIX_W = D_MODEL - MEM_W
FOX_HEADS = MIX_W // HEAD_DIM
POOL_WINDOWS = (2, 4, 8, 16)
POOL_GW = MIX_W // len(POOL_WINDOWS)
POOL_STATE = max(POOL_WINDOWS) - 1
D_FF = 2816
N_EXPERTS = 8
PAGE_SIZE = 128
EPS = 1e-6
SCALE = HEAD_DIM ** -0.5
LOG2E = math.log2(math.e)

LANES = 128
SUBLANES = 8
MXU_N = 256
VMEM_LIMIT = 56 * 1024 * 1024

FF_CHUNK = MXU_N
N_FF_CHUNKS = D_FF // FF_CHUNK
MOE_TM = 768
PAGES_PER_STEP = 8
SAMPLE_ROWS = 8
DMA_UNROLL = 8
DECODE_SLOTS = 3
FLASH_HEADS = 6
MOE_FF_SPLIT = 2
MOE_SUB_ROWS = 256


def _cparams(*sem, vmem=None):
    return pltpu.CompilerParams(dimension_semantics=sem, vmem_limit_bytes=vmem)


def _dot(a, b):
    return jnp.dot(a, b, preferred_element_type=F32)


def _dot_nt(a, b):
    return lax.dot_general(a, b, (((1,), (1,)), ((), ())), preferred_element_type=F32)


def _rms_rows(x, g):
    return x * lax.rsqrt(jnp.mean(x * x, axis=-1, keepdims=True) + EPS) * g


def _split3(x):
    hi = x.astype(BF16)
    r = x - hi.astype(F32)
    mid = r.astype(BF16)
    lo = (r - mid.astype(F32)).astype(BF16)
    return hi, mid, lo


def _head_sumsq_lanes(x):
    r = lax.broadcasted_iota(jnp.int32, (LANES, LANES), 0) // HEAD_DIM
    c = lax.broadcasted_iota(jnp.int32, (LANES, LANES), 1) // HEAD_DIM
    ones_bd = (r == c).astype(BF16)
    xx = x * x
    hi = xx.astype(BF16)
    lo = (xx - hi.astype(F32)).astype(BF16)
    parts = []
    for j in range(x.shape[1] // LANES):
        sl = slice(j * LANES, (j + 1) * LANES)
        parts.append(_dot(hi[:, sl], ones_bd) + _dot(lo[:, sl], ones_bd))
    return jnp.concatenate(parts, axis=1)


def _head_rms_lanes(x, g_tiled):
    ssq = _head_sumsq_lanes(x)
    return x * lax.rsqrt(ssq * (1.0 / HEAD_DIM) + EPS) * g_tiled


def _head_rms_rows_t(xt, g_col, n_heads):
    outs = []
    for h in range(n_heads):
        blk = xt[h * HEAD_DIM:(h + 1) * HEAD_DIM, :]
        ms = jnp.mean(blk * blk, axis=0, keepdims=True)
        outs.append(blk * lax.rsqrt(ms + EPS) * g_col)
    return outs


def _log_sigmoid(x):
    return jnp.minimum(x, 0.0) - jnp.log1p(jnp.exp(-jnp.abs(x)))


def _lane_head(shape, axis):
    return lax.broadcasted_iota(jnp.int32, shape, axis) // HEAD_DIM


def _memkv_body(mem_ref, gm_ref, w_ref, kg_ref, kt_ref, vt_ref):
    h = _rms_rows(mem_ref[0], gm_ref[0])
    z = _dot(h, w_ref[0])
    kt = z[:, :MEM_W].T
    vt_ref[0, 0] = z[:, MEM_W:].T
    pieces = _head_rms_rows_t(kt, kg_ref[0], MEM_HEADS)
    for h_i, p in enumerate(pieces):
        kt_ref[0, 0, h_i * HEAD_DIM:(h_i + 1) * HEAD_DIM, :] = p


def _memkv(mem, g_mem, w_kv, kn_g):
    depth, batch = w_kv.shape[0], mem.shape[0]
    out = jax.ShapeDtypeStruct((depth, batch, MEM_W, MEM_LEN), F32)
    return pl.pallas_call(
        _memkv_body,
        grid=(depth, batch),
        in_specs=[
            pl.BlockSpec((1, MEM_LEN, D_MODEL), lambda i, b: (b, 0, 0)),
            pl.BlockSpec((1, 1, D_MODEL), lambda i, b: (i, 0, 0)),
            pl.BlockSpec((1, D_MODEL, 2 * MEM_W), lambda i, b: (i, 0, 0)),
            pl.BlockSpec((1, HEAD_DIM, 1), lambda i, b: (i, 0, 0)),
        ],
        out_specs=[pl.BlockSpec((1, 1, MEM_W, MEM_LEN), lambda i, b: (i, b, 0, 0))] * 2,
        out_shape=[out, out],
        compiler_params=_cparams("arbitrary", "arbitrary"),
        name="memkv",
    )(mem, g_mem[:, None, :], w_kv, kn_g[:, :, None])


def _pool_proj_body(x_ref, g_ref, w_ref, u_ref, qm_ref):
    h = _rms_rows(x_ref[...], g_ref[...])
    u_ref[...] = _dot(h, w_ref[:, :MIX_W])
    qm_ref[...] = _dot(h, w_ref[:, MIX_W:])


def _pool_proj(x, g, w, tm):
    t = x.shape[0]
    return pl.pallas_call(
        _pool_proj_body,
        grid=(t // tm,),
        in_specs=[
            pl.BlockSpec((tm, D_MODEL), lambda i: (i, 0)),
            pl.BlockSpec((1, D_MODEL), lambda i: (0, 0)),
            pl.BlockSpec((D_MODEL, D_MODEL), lambda i: (0, 0)),
        ],
        out_specs=[pl.BlockSpec((tm, MIX_W), lambda i: (i, 0)),
                   pl.BlockSpec((tm, MEM_W), lambda i: (i, 0))],
        out_shape=[jax.ShapeDtypeStruct((t, MIX_W), F32), jax.ShapeDtypeStruct((t, MEM_W), F32)],
        compiler_params=_cparams("arbitrary", vmem=VMEM_LIMIT),
        name="pool_proj",
    )(x, g, w)


def _window_of_lane(shape):
    lane = lax.broadcasted_iota(jnp.int32, shape, len(shape) - 1)
    return jnp.where(lane < POOL_GW, POOL_WINDOWS[0],
                     jnp.where(lane < 2 * POOL_GW, POOL_WINDOWS[1],
                               jnp.where(lane < 3 * POOL_GW, POOL_WINDOWS[2], POOL_WINDOWS[3])))


def _pool_select(s2, s4, s8, s16, shape):
    lane = lax.broadcasted_iota(jnp.int32, shape, len(shape) - 1)
    return jnp.where(lane < POOL_GW, s2,
                     jnp.where(lane < 2 * POOL_GW, s4, jnp.where(lane < 3 * POOL_GW, s8, s16)))


_POOL_PAD = SUBLANES
_POOL_HALO = 2 * SUBLANES
_POOL_BASE = _POOL_PAD + _POOL_HALO


def _pool_mix_tile(u, li, tl, e_ref, s2_ref, s4_ref, s8_ref, w_bd, scale):
    n = _POOL_HALO + tl

    @pl.when(li == 0)
    def _():
        e_ref[0:_POOL_BASE, :] = jnp.zeros((_POOL_BASE, MIX_W), F32)
        s2_ref[0:_POOL_PAD, :] = jnp.zeros((_POOL_PAD, MIX_W), F32)
        s4_ref[0:_POOL_PAD, :] = jnp.zeros((_POOL_PAD, MIX_W), F32)
        s8_ref[0:_POOL_PAD, :] = jnp.zeros((_POOL_PAD, MIX_W), F32)

    e_ref[_POOL_BASE:_POOL_BASE + tl, :] = u
    s2 = e_ref[_POOL_PAD:_POOL_PAD + n, :] + e_ref[_POOL_PAD - 1:_POOL_PAD - 1 + n, :]
    s2_ref[_POOL_PAD:_POOL_PAD + n, :] = s2
    s4 = s2 + s2_ref[_POOL_PAD - 2:_POOL_PAD - 2 + n, :]
    s4_ref[_POOL_PAD:_POOL_PAD + n, :] = s4
    s8 = s4 + s4_ref[_POOL_PAD - 4:_POOL_PAD - 4 + n, :]
    s8_ref[_POOL_PAD:_POOL_PAD + n, :] = s8
    s16 = s8[_POOL_HALO:, :] + s8_ref[_POOL_BASE - 8:_POOL_BASE - 8 + tl, :]
    shape = (tl, MIX_W)
    ssel = _pool_select(s2[_POOL_HALO:, :], s4[_POOL_HALO:, :], s8[_POOL_HALO:, :], s16, shape)
    pos = li * tl + lax.broadcasted_iota(jnp.int32, shape, 0)
    cnt = jnp.minimum(_window_of_lane(shape), pos + 1).astype(F32)
    diff = ssel / cnt - u
    mix = _dot(diff, w_bd) * scale
    e_ref[_POOL_PAD:_POOL_BASE, :] = e_ref[_POOL_PAD + tl:_POOL_BASE + tl, :]
    return mix


def _pool_layer_prompt_body(x_ref, g_ref, win_ref, wbd_ref, sc_ref, kt_ref, vt_ref, qg_ref, wo_ref,
                            o_ref, st_ref, e_ref, s2_ref, s4_ref, s8_ref, *, tl):
    x = x_ref[0]
    h = _rms_rows(x, g_ref[...])
    u = _dot(h, win_ref[:, :MIX_W])
    qm = _dot(h, win_ref[:, MIX_W:])
    st_ref[0] = u[tl - _POOL_HALO:, :]
    mix = _pool_mix_tile(u, pl.program_id(1), tl, e_ref, s2_ref, s4_ref, s8_ref, wbd_ref[...], sc_ref[...])
    mem = _mem_attend_tile(qm, kt_ref[0], vt_ref[0], qg_ref[...])
    o_ref[0] = x + _dot(mix, wo_ref[:MIX_W, :]) + _dot(mem, wo_ref[MIX_W:, :])


def _pool_layer_prompt(x, g, w_in, w_bd, scale, mem_kt, mem_vt, qn_g, w_out, tl):
    b, s, _ = x.shape
    rows = _POOL_BASE + tl
    const = lambda shape: pl.BlockSpec(shape, lambda bi, li: tuple(0 for _ in shape))
    per_b = lambda shape: pl.BlockSpec(shape, lambda bi, li: (bi, 0, 0))
    return pl.pallas_call(
        functools.partial(_pool_layer_prompt_body, tl=tl),
        grid=(b, s // tl),
        in_specs=[
            pl.BlockSpec((1, tl, D_MODEL), lambda bi, li: (bi, li, 0)),
            const((1, D_MODEL)), const((D_MODEL, D_MODEL)), const((MIX_W, MIX_W)), const((1, MIX_W)),
            per_b((1, MEM_W, MEM_LEN)), per_b((1, MEM_W, MEM_LEN)), const((1, MEM_W)),
            const((D_MODEL, D_MODEL)),
        ],
        out_specs=[pl.BlockSpec((1, tl, D_MODEL), lambda bi, li: (bi, li, 0)),
                   per_b((1, _POOL_HALO, MIX_W))],
        out_shape=[jax.ShapeDtypeStruct((b, s, D_MODEL), F32),
                   jax.ShapeDtypeStruct((b, _POOL_HALO, MIX_W), F32)],
        scratch_shapes=[pltpu.VMEM((rows, MIX_W), F32)] * 4,
        compiler_params=_cparams("arbitrary", "arbitrary", vmem=VMEM_LIMIT),
        name="pool_layer_prompt",
    )(x, g, w_in, w_bd, scale, mem_kt, mem_vt, jnp.tile(qn_g, MEM_HEADS)[None, :], w_out)


def _pool_sample_body(st_ref, u_ref, w_ref, sc_ref, o_ref, ns_ref, *, n_new):
    def ext(j):
        return st_ref[j] if j < POOL_STATE else u_ref[j - POOL_STATE]

    for l in range(n_new):
        r = POOL_STATE + l
        s2 = ext(r) + ext(r - 1)
        s4 = s2 + ext(r - 2) + ext(r - 3)
        s8 = s4
        for j in range(4, 8):
            s8 = s8 + ext(r - j)
        s16 = s8
        for j in range(8, 16):
            s16 = s16 + ext(r - j)
        shape = s2.shape
        cnt = _window_of_lane(shape).astype(F32)
        diff = _pool_select(s2, s4, s8, s16, shape) / cnt - ext(r)
        o_ref[l] = _dot(diff, w_ref[...]) * sc_ref[...]
    for j in range(POOL_STATE):
        ns_ref[j] = ext(j + n_new)


def _pool_sample(state_t, u_t, w_bd, scale):
    n_new, b, _ = u_t.shape
    return pl.pallas_call(
        functools.partial(_pool_sample_body, n_new=n_new),
        out_shape=[jax.ShapeDtypeStruct((n_new, b, MIX_W), F32),
                   jax.ShapeDtypeStruct((POOL_STATE, b, MIX_W), F32)],
        name="pool_sample",
    )(state_t, u_t, w_bd, scale)


def _mem_attend_tile(qm, kt, vt, g_tiled):
    q = _head_rms_lanes(qm, g_tiled) * SCALE
    lh = _lane_head(q.shape, 1)
    out = jnp.zeros(q.shape, F32)
    for h in range(MEM_HEADS):
        s = _dot(jnp.where(lh == h, q, 0.0), kt)
        e = jnp.exp(s - jnp.max(s, axis=-1, keepdims=True))
        p = e / jnp.sum(e, axis=-1, keepdims=True)
        out = jnp.where(lh == h, _dot_nt(p, vt), out)
    return out


def _mem_attend_body(q_ref, kt_ref, vt_ref, g_ref, o_ref):
    for i in range(q_ref.shape[0]):
        o_ref[i] = _mem_attend_tile(q_ref[i], kt_ref[0, i], vt_ref[0, i], g_ref[...])


def _mem_attend(qm, kt_layers, vt_layers, layer, qn_g, tl, bb):
    b, l, _ = qm.shape
    kv_spec = pl.BlockSpec((1, bb, MEM_W, MEM_LEN), lambda bi, li: (layer, bi, 0, 0))
    return pl.pallas_call(
        _mem_attend_body,
        grid=(b // bb, l // tl),
        in_specs=[
            pl.BlockSpec((bb, tl, MEM_W), lambda bi, li: (bi, li, 0)),
            kv_spec, kv_spec,
            pl.BlockSpec((1, MEM_W), lambda bi, li: (0, 0)),
        ],
        out_specs=pl.BlockSpec((bb, tl, MEM_W), lambda bi, li: (bi, li, 0)),
        out_shape=jax.ShapeDtypeStruct((b, l, MEM_W), F32),
        compiler_params=_cparams("arbitrary", "arbitrary"),
        name="mem_attend",
    )(qm, kt_layers, vt_layers, jnp.tile(qn_g, MEM_HEADS)[None, :])


def _out_proj_body(x_ref, mix_ref, mem_ref, w_ref, o_ref):
    o_ref[...] = (x_ref[...] + _dot(mix_ref[...], w_ref[0, :MIX_W, :])
                  + _dot(mem_ref[...], w_ref[0, MIX_W:, :]))


def _out_proj(x, mix, mem, w_layers, layer, tm):
    t = x.shape[0]
    return pl.pallas_call(
        _out_proj_body,
        grid=(t // tm,),
        in_specs=[
            pl.BlockSpec((tm, D_MODEL), lambda i: (i, 0)),
            pl.BlockSpec((tm, MIX_W), lambda i: (i, 0)),
            pl.BlockSpec((tm, MEM_W), lambda i: (i, 0)),
            pl.BlockSpec((1, D_MODEL, D_MODEL), lambda i: (layer, 0, 0)),
        ],
        out_specs=pl.BlockSpec((tm, D_MODEL), lambda i: (i, 0)),
        out_shape=jax.ShapeDtypeStruct((t, D_MODEL), F32),
        compiler_params=_cparams("arbitrary", vmem=VMEM_LIMIT),
        name="out_proj",
    )(x, mix, mem, w_layers)


def _silu(g):
    return g / (1.0 + jnp.exp(-g))


def _ffn_body(x_ref, g_ref, wgu_ref, wd_ref, o_ref):
    x = x_ref[...]
    h = _rms_rows(x, g_ref[...])
    o_ref[...] = x
    for c in range(N_FF_CHUNKS):
        gate = slice(c * FF_CHUNK, (c + 1) * FF_CHUNK)
        up = slice(D_FF + c * FF_CHUNK, D_FF + (c + 1) * FF_CHUNK)
        a = _silu(_dot(h, wgu_ref[:, gate])) * _dot(h, wgu_ref[:, up])
        o_ref[...] += _dot(a, wd_ref[gate, :])


def _ffn(x, g, w_gu, w_down, tm):
    t = x.shape[0]
    resident = lambda shape: pl.BlockSpec(shape, lambda i: (0, 0), pipeline_mode=pl.Buffered(1))
    return pl.pallas_call(
        _ffn_body,
        grid=(t // tm,),
        in_specs=[
            pl.BlockSpec((tm, D_MODEL), lambda i: (i, 0)),
            pl.BlockSpec((1, D_MODEL), lambda i: (0, 0)),
            resident((D_MODEL, 2 * D_FF)),
            resident((D_FF, D_MODEL)),
        ],
        out_specs=pl.BlockSpec((tm, D_MODEL), lambda i: (i, 0)),
        out_shape=jax.ShapeDtypeStruct((t, D_MODEL), F32),
        compiler_params=_cparams("arbitrary", vmem=VMEM_LIMIT),
        name="ffn",
    )(x, g, w_gu, w_down)


_FL_ROWS = 2 * SUBLANES
_AUG0 = HEAD_DIM


def _aug_pieces(c_col):
    hi, mid, lo = _split3(c_col)
    bc = lambda a: jnp.broadcast_to(a.astype(F32), (c_col.shape[0], LANES))
    return bc(hi), bc(mid), bc(lo)


def _fox_proj_prompt_body(x_ref, g_ref, wq_ref, wk_ref, wv_ref, wqm_ref, wf_ref, bf_ref, qg_ref, kg_ref,
                          qp_ref, kp_ref, kt_ref, vt_ref, lft_ref, qm_ref, carry_ref, *, tm, per_seq):
    @pl.when(pl.program_id(0) % per_seq == 0)
    def _():
        carry_ref[...] = jnp.zeros(carry_ref.shape, F32)

    h = _rms_rows(x_ref[...], g_ref[...])
    q_pieces = _head_rms_rows_t(_dot_nt(wq_ref[...], h), qg_ref[...], FOX_HEADS)
    k_pieces = _head_rms_rows_t(_dot_nt(wk_ref[...], h), kg_ref[...], FOX_HEADS)
    for h_i, piece in enumerate(k_pieces):
        kt_ref[0, h_i * HEAD_DIM:(h_i + 1) * HEAD_DIM, :] = piece
    vt_ref[0] = _dot_nt(wv_ref[...], h)
    q = jnp.concatenate(q_pieces, axis=0).T * (SCALE * LOG2E)
    k = jnp.concatenate(k_pieces, axis=0).T
    qm_ref[...] = _dot(h, wqm_ref[...])
    lane = lax.broadcasted_iota(jnp.int32, (tm, LANES), 1)
    lf = jnp.where(lane < FOX_HEADS, _log_sigmoid(_dot(h, wf_ref[...]) + bf_ref[...]), 0.0)
    lft_ref[...] = lf.T[:_FL_ROWS, :]
    r = lax.broadcasted_iota(jnp.int32, (tm, tm), 0)
    c = lax.broadcasted_iota(jnp.int32, (tm, tm), 1)
    lower = (c <= r).astype(BF16)
    hi, mid, lo = _split3(lf)
    csum = _dot(lower, hi) + _dot(lower, mid) + _dot(lower, lo) + carry_ref[0:1, :]
    carry_ref[...] = jnp.broadcast_to(csum[tm - 1:tm, :], carry_ref.shape)
    c2 = csum * LOG2E
    one = jnp.ones((tm, LANES), F32)
    zero = jnp.zeros((tm, LANES), F32)
    for hd in range(FOX_HEADS):
        chi, cmid, clo = _aug_pieces(c2[:, hd:hd + 1])
        aug_q = jnp.where(lane == _AUG0, chi, jnp.where(lane == _AUG0 + 1, cmid, jnp.where(
            lane == _AUG0 + 2, clo, jnp.where(lane < _AUG0 + 6, one, zero))))
        aug_k = jnp.where(lane < _AUG0 + 3, one, jnp.where(lane == _AUG0 + 3, -chi, jnp.where(
            lane == _AUG0 + 4, -cmid, jnp.where(lane == _AUG0 + 5, -clo, zero))))
        col = slice((hd // 2) * LANES, (hd // 2 + 1) * LANES)
        qc, kc = q[:, col], k[:, col]
        if hd % 2:
            qc = pltpu.roll(qc, HEAD_DIM, 1)
            kc = pltpu.roll(kc, HEAD_DIM, 1)
        qp_ref[0, hd] = jnp.where(lane < HEAD_DIM, qc, aug_q)
        kp_ref[0, hd] = jnp.where(lane < HEAD_DIM, kc, aug_k)


def _fox_proj_prompt(x, g, wq_t, wk_t, wv_t, wqm, wf, bf_row, qg_col, kg_col, tm, n_seq):
    t = x.shape[0]
    seq = t // n_seq
    per_seq = seq // tm

    def const(shape):
        return pl.BlockSpec(shape, lambda i: tuple(0 for _ in shape))

    heads_spec = pl.BlockSpec((1, FOX_HEADS, tm, LANES), lambda i: (i // per_seq, 0, i % per_seq, 0))
    t_spec = pl.BlockSpec((1, MIX_W, tm), lambda i: (i // per_seq, 0, i % per_seq))
    return pl.pallas_call(
        functools.partial(_fox_proj_prompt_body, tm=tm, per_seq=per_seq),
        grid=(t // tm,),
        in_specs=[
            pl.BlockSpec((tm, D_MODEL), lambda i: (i, 0)),
            const((1, D_MODEL)),
            const((MIX_W, D_MODEL)), const((MIX_W, D_MODEL)), const((MIX_W, D_MODEL)),
            const((D_MODEL, MEM_W)), const((D_MODEL, LANES)),
            const((1, LANES)), const((HEAD_DIM, 1)), const((HEAD_DIM, 1)),
        ],
        out_specs=[heads_spec, heads_spec, t_spec, t_spec,
                   pl.BlockSpec((_FL_ROWS, tm), lambda i: (0, i)),
                   pl.BlockSpec((tm, MEM_W), lambda i: (i, 0))],
        out_shape=[
            jax.ShapeDtypeStruct((n_seq, FOX_HEADS, seq, LANES), F32),
            jax.ShapeDtypeStruct((n_seq, FOX_HEADS, seq, LANES), F32),
            jax.ShapeDtypeStruct((n_seq, MIX_W, seq), F32),
            jax.ShapeDtypeStruct((n_seq, MIX_W, seq), F32),
            jax.ShapeDtypeStruct((_FL_ROWS, t), F32),
            jax.ShapeDtypeStruct((t, MEM_W), F32),
        ],
        scratch_shapes=[pltpu.VMEM((SUBLANES, LANES), F32)],
        compiler_params=_cparams("arbitrary", vmem=VMEM_LIMIT),
        name="fox_proj_prompt",
    )(x, g, wq_t, wk_t, wv_t, wqm, wf, bf_row, qg_col, kg_col)


def _fox_proj_sample_body(x_ref, g_ref, wq_ref, wk_ref, wv_ref, wqm_ref, wf_ref, bf_ref, qg_ref, kg_ref,
                          q_ref, k_ref, v_ref, lft_ref, qm_ref):
    h = _rms_rows(x_ref[...], g_ref[...])
    q_ref[...] = _head_rms_lanes(_dot(h, wq_ref[...]), qg_ref[...])
    k_ref[...] = _head_rms_lanes(_dot(h, wk_ref[...]), kg_ref[...])
    v_ref[...] = _dot(h, wv_ref[...])
    qm_ref[...] = _dot(h, wqm_ref[...])
    lane = lax.broadcasted_iota(jnp.int32, (x_ref.shape[0], LANES), 1)
    lf = jnp.where(lane < FOX_HEADS, _log_sigmoid(_dot(h, wf_ref[...]) + bf_ref[...]), 0.0)
    lft_ref[...] = lf.T[:_FL_ROWS, :]


def _fox_proj_sample(x, g, wq, wk, wv, wqm, wf, bf_row, qg_t, kg_t):
    t = x.shape[0]
    rows = jax.ShapeDtypeStruct((t, MIX_W), F32)
    return pl.pallas_call(
        _fox_proj_sample_body,
        out_shape=[rows, rows, rows, jax.ShapeDtypeStruct((_FL_ROWS, t), F32),
                   jax.ShapeDtypeStruct((t, MEM_W), F32)],
        compiler_params=_cparams(vmem=VMEM_LIMIT),
        name="fox_proj_sample",
    )(x, g, wq, wk, wv, wqm, wf, bf_row, qg_t, kg_t)


def _fox_flash_body(qi_ref, ki_ref, qp_ref, kp_ref, vt_ref, o_ref, m_ref, l_ref, acc_ref, *, blk):
    p = pl.program_id(2)
    qi = qi_ref[p]
    ki = ki_ref[p]

    @pl.when(ki == 0)
    def _():
        m_ref[...] = jnp.full(m_ref.shape, -jnp.inf, F32)
        l_ref[...] = jnp.zeros(l_ref.shape, F32)
        acc_ref[...] = jnp.zeros(acc_ref.shape, F32)

    def step(diagonal):
        for hh in range(FLASH_HEADS):
            pair = hh // 2
            vt = vt_ref[0, pair * LANES:(pair + 1) * LANES, :]
            st = _dot_nt(kp_ref[0, hh], qp_ref[0, hh])
            if diagonal:
                key = lax.broadcasted_iota(jnp.int32, st.shape, 0)
                qry = lax.broadcasted_iota(jnp.int32, st.shape, 1)
                st = jnp.where(key <= qry, st, -jnp.inf)
            m_prev = m_ref[hh]
            m_new = jnp.maximum(m_prev, jnp.max(st, axis=0, keepdims=True))
            alpha = jnp.exp2(m_prev - m_new)
            pt = jnp.exp2(st - m_new)
            l_ref[hh] = alpha * l_ref[hh] + jnp.sum(pt, axis=0, keepdims=True)
            acc_ref[hh] = alpha * acc_ref[hh] + _dot(vt, pt)
            m_ref[hh] = m_new

    @pl.when(ki < qi)
    def _():
        step(False)

    @pl.when(ki == qi)
    def _():
        step(True)
        row = lax.broadcasted_iota(jnp.int32, (LANES, blk), 0)
        for pair in range(FLASH_HEADS // 2):
            a, c = 2 * pair, 2 * pair + 1
            o_ref[0, pair * LANES:(pair + 1) * LANES, :] = jnp.where(
                row < HEAD_DIM, acc_ref[a] / l_ref[a], acc_ref[c] / l_ref[c])


def _fox_flash(qp, kp, vt, blk):
    b, _, s, _ = qp.shape
    n = s // blk
    pairs = [(qi, ki) for qi in range(n) for ki in range(qi + 1)]
    qi_tab = jnp.asarray([p[0] for p in pairs], jnp.int32)
    ki_tab = jnp.asarray([p[1] for p in pairs], jnp.int32)
    fh = FLASH_HEADS
    ch = fh // 2 * LANES
    grid_spec = pltpu.PrefetchScalarGridSpec(
        num_scalar_prefetch=2,
        grid=(b, FOX_HEADS // fh, len(pairs)),
        in_specs=[
            pl.BlockSpec((1, fh, blk, LANES), lambda bi, hg, p, qt, kt: (bi, hg, qt[p], 0)),
            pl.BlockSpec((1, fh, blk, LANES), lambda bi, hg, p, qt, kt: (bi, hg, kt[p], 0)),
            pl.BlockSpec((1, ch, blk), lambda bi, hg, p, qt, kt: (bi, hg, kt[p])),
        ],
        out_specs=pl.BlockSpec((1, ch, blk), lambda bi, hg, p, qt, kt: (bi, hg, qt[p])),
        scratch_shapes=[pltpu.VMEM((fh, 1, blk), F32), pltpu.VMEM((fh, 1, blk), F32),
                        pltpu.VMEM((fh, LANES, blk), F32)],
    )
    return pl.pallas_call(
        functools.partial(_fox_flash_body, blk=blk),
        grid_spec=grid_spec,
        out_shape=jax.ShapeDtypeStruct((b, MIX_W, s), F32),
        compiler_params=_cparams("arbitrary", "arbitrary", "arbitrary", vmem=VMEM_LIMIT),
        name="fox_flash",
    )(qi_tab, ki_tab, qp, kp, vt)


_QROWS = FOX_HEADS * SAMPLE_ROWS
_CHUNK = PAGES_PER_STEP * PAGE_SIZE


def _expand_heads(x):
    n = x.shape[1]
    return jnp.broadcast_to(x[:FOX_HEADS, None, :], (FOX_HEADS, SAMPLE_ROWS, n)).reshape(_QROWS, n)


def _page_copies(pt_ref, kt_hbm, vt_hbm, lf_hbm, kt_buf, vt_buf, lf_buf, sem, g, slot, steps, n_pages):
    bi = g // steps
    first = n_pages - (g % steps + 1) * PAGES_PER_STEP
    copies = []
    for i in range(PAGES_PER_STEP):
        page = pt_ref[bi, first + i]
        lanes = pl.ds(i * PAGE_SIZE, PAGE_SIZE)
        copies.append(pltpu.make_async_copy(kt_hbm.at[page], kt_buf.at[slot, :, lanes], sem.at[slot, 0]))
        copies.append(pltpu.make_async_copy(vt_hbm.at[page], vt_buf.at[slot, :, lanes], sem.at[slot, 1]))
        copies.append(pltpu.make_async_copy(lf_hbm.at[:, page], lf_buf.at[slot, i, pl.ds(0, FOX_HEADS)],
                                            sem.at[slot, 2]))
    return copies


def _fox_decode_body(pt_ref, kt_hbm, vt_hbm, lf_hbm, q_ref, kn_ref, vn_ref, lfn_ref, o_ref,
                     kt_buf, vt_buf, lf_buf, sem, qbd_ref, m_ref, l_ref, acc_ref, carry_ref, crow_ref,
                     *, n_new, steps, n_pages):
    g = pl.program_id(0)
    n_steps = pl.num_programs(0)
    j = g % steps
    slot = g % DECODE_SLOTS
    copies = functools.partial(_page_copies, pt_ref, kt_hbm, vt_hbm, lf_hbm, kt_buf, vt_buf, lf_buf, sem,
                               steps=steps, n_pages=n_pages)
    row_l = lax.broadcasted_iota(jnp.int32, (_QROWS, 1), 0) % SAMPLE_ROWS

    @pl.when(g == 0)
    def _():
        lf_buf[...] = jnp.zeros(lf_buf.shape, F32)
        for ahead in range(DECODE_SLOTS - 1):
            for cp in copies(g=ahead, slot=ahead):
                cp.start()

    @pl.when(g + DECODE_SLOTS - 1 < n_steps)
    def _():
        for cp in copies(g=g + DECODE_SLOTS - 1, slot=(g + DECODE_SLOTS - 1) % DECODE_SLOTS):
            cp.start()

    @pl.when(j == 0)
    def _():
        q = q_ref[0] * SCALE
        lh = _lane_head(q.shape, 1)
        for h in range(FOX_HEADS):
            qbd_ref[h * SAMPLE_ROWS:(h + 1) * SAMPLE_ROWS, :] = jnp.where(lh == h, q, 0.0)
        lfn = lfn_ref[0]
        lane = lax.broadcasted_iota(jnp.int32, lfn.shape, 1)
        c = jnp.zeros(lfn.shape, F32)
        for m in range(n_new):
            cm = jnp.sum(jnp.where(lane <= m, lfn, 0.0), axis=1, keepdims=True)
            c = jnp.where(lane == m, cm, c)
        c_q = _expand_heads(c)
        col = lax.broadcasted_iota(jnp.int32, c_q.shape, 1)
        crow = jnp.sum(jnp.where(col == row_l, c_q, 0.0), axis=1, keepdims=True)
        crow_ref[...] = crow
        s = _dot_nt(qbd_ref[...], kn_ref[0]) + (crow - c_q)
        valid = (col < n_new) & ((col <= row_l) | (row_l >= n_new))
        s = jnp.where(valid, s, -jnp.inf)
        m0 = jnp.max(s, axis=-1, keepdims=True)
        p = jnp.exp(s - m0)
        m_ref[...] = m0
        l_ref[...] = jnp.sum(p, axis=-1, keepdims=True)
        acc_ref[...] = _dot(p, vn_ref[0])
        carry_ref[...] = jnp.zeros(carry_ref.shape, F32)

    for cp in copies(g=g, slot=slot):
        cp.wait()

    r = lax.broadcasted_iota(jnp.int32, (PAGE_SIZE, PAGE_SIZE), 0)
    cc = lax.broadcasted_iota(jnp.int32, (PAGE_SIZE, PAGE_SIZE), 1)
    later = (r > cc).astype(BF16)
    lf3 = lf_buf[slot]
    tot = jnp.sum(lf3, axis=2, keepdims=True)
    hi, mid, lo = _split3(lf3.reshape(PAGES_PER_STEP * _FL_ROWS, PAGE_SIZE))
    d_in = (_dot(hi, later) + _dot(mid, later) + _dot(lo, later)).reshape(PAGES_PER_STEP, _FL_ROWS, PAGE_SIZE)
    after = carry_ref[:, 0:1]
    d_pages = [None] * PAGES_PER_STEP
    for i in reversed(range(PAGES_PER_STEP)):
        d_pages[i] = _expand_heads(d_in[i] + after)
        after = after + tot[i]
    carry_ref[...] = jnp.broadcast_to(after, carry_ref.shape)
    d = jnp.concatenate(d_pages, axis=1)

    vt = vt_buf[slot]
    s = _dot(qbd_ref[...], kt_buf[slot]) + crow_ref[...] + d
    m_prev = m_ref[...]
    m_new = jnp.maximum(m_prev, jnp.max(s, axis=-1, keepdims=True))
    alpha = jnp.exp(m_prev - m_new)
    p = jnp.exp(s - m_new)
    l_ref[...] = alpha * l_ref[...] + jnp.sum(p, axis=-1, keepdims=True)
    acc_ref[...] = alpha * acc_ref[...] + _dot_nt(p, vt)
    m_ref[...] = m_new

    @pl.when(j == steps - 1)
    def _():
        res = acc_ref[...] / l_ref[...]
        lh = _lane_head((SAMPLE_ROWS, MIX_W), 1)
        out = jnp.zeros((SAMPLE_ROWS, MIX_W), F32)
        for h in range(FOX_HEADS):
            out = jnp.where(lh == h, res[h * SAMPLE_ROWS:(h + 1) * SAMPLE_ROWS, :], out)
        o_ref[0] = out


def _fox_decode(page_table, kt_pages, vt_pages, lf_heads, q8, kn8, vn8, lfn, n_new):
    b, n_pages = page_table.shape
    steps = n_pages // PAGES_PER_STEP
    per_b = lambda g, pt: (g // steps, 0, 0)
    any_spec = pl.BlockSpec(memory_space=pl.ANY)
    grid_spec = pltpu.PrefetchScalarGridSpec(
        num_scalar_prefetch=1,
        grid=(b * steps,),
        in_specs=[any_spec, any_spec, any_spec,
                  pl.BlockSpec((1, SAMPLE_ROWS, MIX_W), per_b),
                  pl.BlockSpec((1, SAMPLE_ROWS, MIX_W), per_b),
                  pl.BlockSpec((1, SAMPLE_ROWS, MIX_W), per_b),
                  pl.BlockSpec((1, _FL_ROWS, SAMPLE_ROWS), per_b)],
        out_specs=pl.BlockSpec((1, SAMPLE_ROWS, MIX_W), per_b),
        scratch_shapes=[
            pltpu.VMEM((DECODE_SLOTS, MIX_W, _CHUNK), F32),
            pltpu.VMEM((DECODE_SLOTS, MIX_W, _CHUNK), F32),
            pltpu.VMEM((DECODE_SLOTS, PAGES_PER_STEP, _FL_ROWS, PAGE_SIZE), F32),
            pltpu.SemaphoreType.DMA((DECODE_SLOTS, 3)),
            pltpu.VMEM((_QROWS, MIX_W), F32),
            pltpu.VMEM((_QROWS, 1), F32), pltpu.VMEM((_QROWS, 1), F32),
            pltpu.VMEM((_QROWS, MIX_W), F32),
            pltpu.VMEM((_FL_ROWS, LANES), F32),
            pltpu.VMEM((_QROWS, 1), F32),
        ],
    )
    return pl.pallas_call(
        functools.partial(_fox_decode_body, n_new=n_new, steps=steps, n_pages=n_pages),
        grid_spec=grid_spec,
        out_shape=jax.ShapeDtypeStruct((b, SAMPLE_ROWS, MIX_W), F32),
        compiler_params=_cparams("arbitrary", vmem=VMEM_LIMIT),
        name="fox_decode",
    )(page_table, kt_pages, vt_pages, lf_heads, q8, kn8, vn8, lfn)


def _router_body(x_ref, g_ref, wrt_ref, h_ref, r_ref, cnt_ref, carry_ref, *, tm):
    _route_tile(x_ref[...], g_ref, wrt_ref, h_ref, r_ref, cnt_ref, carry_ref, tm)


def _out_proj_router_body(x_ref, mixt_ref, mem_ref, wo_ref, g_ref, wrt_ref,
                          x1_ref, h_ref, r_ref, cnt_ref, carry_ref, *, tm):
    x1 = (x_ref[...] + _dot(mixt_ref[0].T, wo_ref[0, :MIX_W, :])
          + _dot(mem_ref[...], wo_ref[0, MIX_W:, :]))
    x1_ref[...] = x1
    _route_tile(x1, g_ref, wrt_ref, h_ref, r_ref, cnt_ref, carry_ref, tm)


def _route_tile(x, g_ref, wrt_ref, h_ref, r_ref, cnt_ref, carry_ref, tm):
    @pl.when(pl.program_id(0) == 0)
    def _():
        carry_ref[...] = jnp.zeros(carry_ref.shape, F32)

    h = _rms_rows(x, g_ref[...])
    h_ref[...] = h
    h_hi = h.astype(BF16)
    h_lo = (h - h_hi.astype(F32)).astype(BF16)
    w = wrt_ref[...]
    w_hi = w.astype(BF16)
    w_lo = (w - w_hi.astype(F32)).astype(BF16)
    lg = _dot_nt(w_hi, h_hi) + _dot_nt(w_hi, h_lo) + _dot_nt(w_lo, h_hi)
    idx = lax.broadcasted_iota(jnp.int32, lg.shape, 0)
    m1 = jnp.max(lg, axis=0, keepdims=True)
    i1 = jnp.min(jnp.where(lg == m1, idx, N_EXPERTS), axis=0, keepdims=True)
    sel1 = idx == i1
    lg2 = jnp.where(sel1, -jnp.inf, lg)
    m2 = jnp.max(lg2, axis=0, keepdims=True)
    i2 = jnp.min(jnp.where(lg2 == m2, idx, N_EXPERTS), axis=0, keepdims=True)
    sel2 = idx == i2
    e = jnp.exp(m2 - m1)
    g1 = 1.0 / (1.0 + e)
    g2 = e / (1.0 + e)
    assign = jnp.where(sel1 | sel2, 1.0, 0.0)
    r = lax.broadcasted_iota(jnp.int32, (tm, tm), 0)
    c = lax.broadcasted_iota(jnp.int32, (tm, tm), 1)
    before = (r < c).astype(BF16)
    rank = _dot(assign.astype(BF16), before) + carry_ref[:, 0:1]
    r1 = jnp.sum(jnp.where(sel1, rank, 0.0), axis=0, keepdims=True)
    r2 = jnp.sum(jnp.where(sel2, rank, 0.0), axis=0, keepdims=True)
    carry = carry_ref[...] + jnp.sum(assign, axis=1, keepdims=True)
    carry_ref[...] = carry
    cnt_ref[...] = carry
    rows = [i1.astype(F32), i2.astype(F32), r1, r2, g1, g2]
    out = jnp.zeros(lg.shape, F32)
    for k, v in enumerate(rows):
        out = jnp.where(idx == k, v, out)
    r_ref[...] = out


def _router(x, g, wrt, tm):
    t = x.shape[0]
    return pl.pallas_call(
        functools.partial(_router_body, tm=tm),
        grid=(t // tm,),
        in_specs=[
            pl.BlockSpec((tm, D_MODEL), lambda i: (i, 0)),
            pl.BlockSpec((1, D_MODEL), lambda i: (0, 0)),
            pl.BlockSpec((N_EXPERTS, D_MODEL), lambda i: (0, 0)),
        ],
        out_specs=[
            pl.BlockSpec((tm, D_MODEL), lambda i: (i, 0)),
            pl.BlockSpec((N_EXPERTS, tm), lambda i: (0, i)),
            pl.BlockSpec((N_EXPERTS, LANES), lambda i: (0, 0)),
        ],
        out_shape=[
            jax.ShapeDtypeStruct((t, D_MODEL), F32),
            jax.ShapeDtypeStruct((N_EXPERTS, t), F32),
            jax.ShapeDtypeStruct((N_EXPERTS, LANES), F32),
        ],
        scratch_shapes=[pltpu.VMEM((N_EXPERTS, LANES), F32)],
        compiler_params=_cparams("arbitrary", vmem=VMEM_LIMIT),
        name="router",
    )(x, g, wrt)


def _out_proj_router(x, mix_t, mem, w_layers, layer, g, wrt, tm):
    t = x.shape[0]
    per_seq = mix_t.shape[2] // tm
    return pl.pallas_call(
        functools.partial(_out_proj_router_body, tm=tm),
        grid=(t // tm,),
        in_specs=[
            pl.BlockSpec((tm, D_MODEL), lambda i: (i, 0)),
            pl.BlockSpec((1, MIX_W, tm), lambda i: (i // per_seq, 0, i % per_seq)),
            pl.BlockSpec((tm, MEM_W), lambda i: (i, 0)),
            pl.BlockSpec((1, D_MODEL, D_MODEL), lambda i: (layer, 0, 0)),
            pl.BlockSpec((1, D_MODEL), lambda i: (0, 0)),
            pl.BlockSpec((N_EXPERTS, D_MODEL), lambda i: (0, 0)),
        ],
        out_specs=[
            pl.BlockSpec((tm, D_MODEL), lambda i: (i, 0)),
            pl.BlockSpec((tm, D_MODEL), lambda i: (i, 0)),
            pl.BlockSpec((N_EXPERTS, tm), lambda i: (0, i)),
            pl.BlockSpec((N_EXPERTS, LANES), lambda i: (0, 0)),
        ],
        out_shape=[
            jax.ShapeDtypeStruct((t, D_MODEL), F32),
            jax.ShapeDtypeStruct((t, D_MODEL), F32),
            jax.ShapeDtypeStruct((N_EXPERTS, t), F32),
            jax.ShapeDtypeStruct((N_EXPERTS, LANES), F32),
        ],
        scratch_shapes=[pltpu.VMEM((N_EXPERTS, LANES), F32)],
        compiler_params=_cparams("arbitrary", vmem=VMEM_LIMIT),
        name="out_proj_router",
    )(x, mix_t, mem, w_layers, g, wrt)


_SCATTER_TM = 256


def _row_copy(src_ref, src_row, dst_ref, dst_row, sem):
    return pltpu.make_async_copy(src_ref.at[pl.ds(src_row, 1)], dst_ref.at[pl.ds(dst_row, 1)], sem)


def _scatter_body(d1_ref, d2_ref, hp_ref, hs_ref, xs_ref, sem, *, n_prompt_tiles):
    i = pl.program_id(0)

    def run(src_ref):
        n_rows = src_ref.shape[0]

        def start(r, carry):
            _row_copy(src_ref, r, xs_ref, d1_ref[0, 0, r], sem.at[0]).start(priority=0)
            _row_copy(src_ref, r, xs_ref, d2_ref[0, 0, r], sem.at[1]).start(priority=1)
            return carry

        lax.fori_loop(0, n_rows, start, 0, unroll=DMA_UNROLL)
        for k in range(2):
            pltpu.make_async_copy(src_ref, xs_ref.at[pl.ds(0, n_rows)], sem.at[k]).wait()

    @pl.when(i < n_prompt_tiles)
    def _():
        run(hp_ref)

    @pl.when(i >= n_prompt_tiles)
    def _():
        run(hs_ref)


def _scatter_rows(d1p, d2p, d1s, d2s, h_p, h_s, n_sorted):
    tm = _SCATTER_TM
    npt = h_p.shape[0] // tm
    n_s = h_s.shape[0]
    pad = jnp.zeros((tm - n_s,), jnp.int32)
    d1 = jnp.concatenate([d1p, d1s, pad])
    d2 = jnp.concatenate([d2p, d2s, pad])
    smem = lambda: pl.BlockSpec((1, 1, tm), lambda i: (i, 0, 0), memory_space=pltpu.SMEM)
    return pl.pallas_call(
        functools.partial(_scatter_body, n_prompt_tiles=npt),
        grid=(npt + 1,),
        in_specs=[
            smem(), smem(),
            pl.BlockSpec((tm, D_MODEL), lambda i: (jnp.minimum(i, npt - 1), 0)),
            pl.BlockSpec((n_s, D_MODEL), lambda i: (0, 0)),
        ],
        out_specs=pl.BlockSpec(memory_space=pl.ANY),
        out_shape=jax.ShapeDtypeStruct((n_sorted, D_MODEL), F32),
        scratch_shapes=[pltpu.SemaphoreType.DMA((2,))],
        compiler_params=_cparams("arbitrary", vmem=VMEM_LIMIT),
        name="moe_scatter",
    )(d1.reshape(-1, 1, tm), d2.reshape(-1, 1, tm), h_p, h_s)


def _gmm_body(tile_ref, exp_ref, lo_ref, hi_ref, first_ref, x_ref, wg_ref, wu_ref, wd_ref, o_ref):
    del tile_ref, exp_ref
    v = pl.program_id(0)
    c = pl.program_id(1)

    @pl.when((first_ref[v] == 1) & (c == 0))
    def _():
        o_ref[...] = jnp.zeros(o_ref.shape, F32)

    lo = lo_ref[v]
    hi = hi_ref[v]
    whole = (lo == 0) & (hi == MOE_TM)

    def swiglu_into(rows, mask):
        x = x_ref[rows, :]
        for off, n in _MOE_SUBS:
            cols = slice(off, off + n)
            a = _silu(_dot(x, wg_ref[0, :, cols])) * _dot(x, wu_ref[0, :, cols])
            y = _dot(a, wd_ref[0, cols, :])
            o_ref[rows, :] += y if mask is None else jnp.where(mask, y, 0.0)

    @pl.when(whole)
    def _():
        swiglu_into(slice(None), None)

    @pl.when(jnp.logical_not(whole) & (hi > lo))
    def _():
        for r0 in range(0, MOE_TM, MOE_SUB_ROWS):
            @pl.when((lo < r0 + MOE_SUB_ROWS) & (hi > r0))
            def _():
                rows = r0 + lax.broadcasted_iota(jnp.int32, (MOE_SUB_ROWS, 1), 0)
                swiglu_into(slice(r0, r0 + MOE_SUB_ROWS), (rows >= lo) & (rows < hi))


_MOE_FF = D_FF // MOE_FF_SPLIT
_MOE_SUBS = tuple((off, min(MXU_N, _MOE_FF - off)) for off in range(0, _MOE_FF, MXU_N))


def _gmm(meta, xs, w_gu, w_down):
    tile, expert, lo, hi, first = meta
    n_visits = tile.shape[0]
    grid_spec = pltpu.PrefetchScalarGridSpec(
        num_scalar_prefetch=5,
        grid=(n_visits, MOE_FF_SPLIT),
        in_specs=[
            pl.BlockSpec((MOE_TM, D_MODEL), lambda v, c, t, e, *_: (t[v], 0)),
            pl.BlockSpec((1, D_MODEL, _MOE_FF), lambda v, c, t, e, *_: (e[v], 0, c)),
            pl.BlockSpec((1, D_MODEL, _MOE_FF), lambda v, c, t, e, *_: (e[v], 0, MOE_FF_SPLIT + c)),
            pl.BlockSpec((1, _MOE_FF, D_MODEL), lambda v, c, t, e, *_: (e[v], c, 0)),
        ],
        out_specs=pl.BlockSpec((MOE_TM, D_MODEL), lambda v, c, t, e, *_: (t[v], 0)),
    )
    return pl.pallas_call(
        _gmm_body,
        grid_spec=grid_spec,
        out_shape=jax.ShapeDtypeStruct(xs.shape, F32),
        compiler_params=_cparams("arbitrary", "arbitrary", vmem=VMEM_LIMIT),
        name="moe_gmm",
    )(tile, expert, lo, hi, first, xs, w_gu, w_gu, w_down)


def _gmm_meta(counts, n_sorted):
    n_tiles = n_sorted // MOE_TM
    n_visits = n_tiles + N_EXPERTS - 1
    ends = jnp.cumsum(counts)
    starts = ends - counts
    first_tile = starts // MOE_TM
    last_tile = jnp.maximum(ends - 1, 0) // MOE_TM
    nv = jnp.where(counts > 0, last_tile - first_tile + 1, 0)
    cv = jnp.cumsum(nv)
    v = jnp.arange(n_visits, dtype=jnp.int32)
    total = cv[-1]
    valid = v < total
    vc = jnp.minimum(v, total - 1)
    expert = jnp.sum((cv[None, :] <= vc[:, None]).astype(jnp.int32), axis=1)
    tile = first_tile[expert] + (vc - (cv[expert] - nv[expert]))
    lo = jnp.maximum(starts[expert], tile * MOE_TM) - tile * MOE_TM
    hi = jnp.minimum(ends[expert], (tile + 1) * MOE_TM) - tile * MOE_TM
    lo = jnp.where(valid, lo, 0)
    hi = jnp.where(valid, hi, 0)
    prev_tile = jnp.concatenate([jnp.full((1,), -1, jnp.int32), tile[:-1]])
    first = (valid & (tile != prev_tile)).astype(jnp.int32)
    as_i32 = lambda a: a.astype(jnp.int32)
    return as_i32(tile), as_i32(expert), as_i32(lo), as_i32(hi), first


def _combine_body(d1_ref, d2_ref, x_ref, g_ref, os_ref, y_ref, buf_ref, sem, *, tm, n_tiles):
    i = pl.program_id(0)
    slot = i % 2

    def issue(tile, slot_):
        def body(r, carry):
            _row_copy(os_ref, d1_ref[tile, r], buf_ref.at[slot_, 0], r, sem.at[slot_, 0]).start(priority=0)
            _row_copy(os_ref, d2_ref[tile, r], buf_ref.at[slot_, 1], r, sem.at[slot_, 1]).start(priority=1)
            return carry
        lax.fori_loop(0, tm, body, 0, unroll=DMA_UNROLL)

    @pl.when(i == 0)
    def _():
        issue(0, 0)

    @pl.when(i + 1 < n_tiles)
    def _():
        issue(i + 1, 1 - slot)

    for k in range(2):
        pltpu.make_async_copy(os_ref.at[pl.ds(0, tm)], buf_ref.at[slot, k], sem.at[slot, k]).wait()

    g = g_ref[...]
    y_ref[...] = x_ref[...] + g[:, 0:1] * buf_ref[slot, 0] + g[:, 1:2] * buf_ref[slot, 1]


def _combine(d1, d2, x, gates, o_sorted, tm):
    t = x.shape[0]
    n_tiles = t // tm
    grid_spec = pltpu.PrefetchScalarGridSpec(
        num_scalar_prefetch=2,
        grid=(n_tiles,),
        in_specs=[
            pl.BlockSpec((tm, D_MODEL), lambda i, *_: (i, 0)),
            pl.BlockSpec((tm, 2), lambda i, *_: (i, 0)),
            pl.BlockSpec(memory_space=pl.ANY),
        ],
        out_specs=pl.BlockSpec((tm, D_MODEL), lambda i, *_: (i, 0)),
        scratch_shapes=[pltpu.VMEM((2, 2, tm, D_MODEL), F32), pltpu.SemaphoreType.DMA((2, 2))],
    )
    return pl.pallas_call(
        functools.partial(_combine_body, tm=tm, n_tiles=n_tiles),
        grid_spec=grid_spec,
        out_shape=jax.ShapeDtypeStruct((t, D_MODEL), F32),
        compiler_params=_cparams("arbitrary", vmem=VMEM_LIMIT),
        name="moe_combine",
    )(d1.reshape(n_tiles, tm), d2.reshape(n_tiles, tm), x, gates, o_sorted)


def _block_diag(w_grp):
    g, n, _ = w_grp.shape
    out = jnp.zeros((g * n, g * n), w_grp.dtype)
    for i in range(g):
        out = out.at[i * n:(i + 1) * n, i * n:(i + 1) * n].set(w_grp[i])
    return out


def kernel(x_prompt, x_sample, cache_mem_k, cache_mem_v, state_pool, cache_fox_k, cache_fox_v, cache_fox_logf, page_table, mem_prompt, norm_mix_g, norm_ffn_g, norm_mem_g, w_mem_kv, mem_q_norm_g, mem_k_norm_g, w_out, w_in_pool, w_pool_group, pool_scale, w_in_fox, b_forget, fox_q_norm_g, fox_k_norm_g, w_ffn_gu, w_ffn_down, w_router, w_exp_gu, w_exp_down):
    b, s, d = x_prompt.shape
    bs, ls, _ = x_sample.shape
    tp, ts = b * s, bs * ls
    tm_p, tm_s = 512, ts

    xp = x_prompt.reshape(tp, d)
    xs = x_sample.reshape(ts, d)

    mem_kt_p, mem_vt_p = _memkv(mem_prompt, norm_mem_g, w_mem_kv, mem_k_norm_g)
    mem_kt_s = cache_mem_k.transpose(0, 1, 3, 4, 2).reshape(2, bs, MEM_W, MEM_LEN)
    mem_vt_s = cache_mem_v.transpose(0, 1, 3, 4, 2).reshape(2, bs, MEM_W, MEM_LEN)

    def mem_attend_sample(qm_s, layer):
        qs8 = jnp.pad(qm_s.reshape(bs, ls, MEM_W), ((0, 0), (0, SAMPLE_ROWS - ls), (0, 0)))
        mo_s = _mem_attend(qs8, mem_kt_s, mem_vt_s, layer, mem_q_norm_g[layer], SAMPLE_ROWS, 4)
        return mo_s[:, :ls].reshape(ts, MEM_W)

    g_mix0 = norm_mix_g[0][None, :]
    w_bd = _block_diag(w_pool_group[0])
    pscale = pool_scale[0][None, :]
    xp, up_tail = _pool_layer_prompt(x_prompt, g_mix0, w_in_pool[0], w_bd, pscale, mem_kt_p[0], mem_vt_p[0],
                                     mem_q_norm_g[0], w_out[0], tm_p)
    xp = xp.reshape(tp, d)
    pool_p = up_tail[:, _POOL_HALO - POOL_STATE:][None]
    us, qms = _pool_proj(xs, g_mix0, w_in_pool[0], tm_s)
    us_t = us.reshape(bs, ls, MIX_W).transpose(1, 0, 2)
    mix_s_t, new_state_t = _pool_sample(state_pool[0].transpose(1, 0, 2), us_t, w_bd, pscale)
    mix_s = mix_s_t.transpose(1, 0, 2).reshape(ts, MIX_W)
    pool_s = new_state_t.transpose(1, 0, 2)[None]
    xs = _out_proj(xs, mix_s, mem_attend_sample(qms, 0), w_out, 0, tm_s)
    g_ffn0 = norm_ffn_g[0][None, :]
    xp = _ffn(xp, g_ffn0, w_ffn_gu[0], w_ffn_down[0], tm_p)
    xs = _ffn(xs, g_ffn0, w_ffn_gu[0], w_ffn_down[0], tm_s)

    w_in = w_in_fox[0]
    wq = w_in[:, :MIX_W]
    wk = w_in[:, MIX_W:2 * MIX_W]
    wv = w_in[:, 2 * MIX_W:3 * MIX_W]
    wf = jnp.pad(w_in[:, 3 * MIX_W:3 * MIX_W + FOX_HEADS], ((0, 0), (0, LANES - FOX_HEADS)))
    wqm = w_in[:, 3 * MIX_W + FOX_HEADS:]
    bf_row = jnp.pad(b_forget[0], (0, LANES - FOX_HEADS))[None, :]
    qg_t = jnp.tile(fox_q_norm_g[0], FOX_HEADS)[None, :]
    kg_t = jnp.tile(fox_k_norm_g[0], FOX_HEADS)[None, :]
    g_mix1 = norm_mix_g[1][None, :]
    qpp, kpp, ktp, vtp, lftp, qmp = _fox_proj_prompt(
        xp, g_mix1, wq.T, wk.T, wv.T, wqm, wf, bf_row, fox_q_norm_g[0][:, None], fox_k_norm_g[0][:, None],
        tm=tm_p, n_seq=b)
    qs, kns, vns, lfts, qms = _fox_proj_sample(xs, g_mix1, wq, wk, wv, wqm, wf, bf_row, qg_t, kg_t)

    mix_pt = _fox_flash(qpp, kpp, vtp, 512)

    pad_rows = ((0, 0), (0, SAMPLE_ROWS - ls), (0, 0))
    q8 = jnp.pad(qs.reshape(bs, ls, MIX_W), pad_rows)
    kn8 = jnp.pad(kns.reshape(bs, ls, MIX_W), pad_rows)
    vn8 = jnp.pad(vns.reshape(bs, ls, MIX_W), pad_rows)
    lfn = jnp.pad(lfts.reshape(_FL_ROWS, bs, ls).transpose(1, 0, 2),
                  ((0, 0), (0, 0), (0, SAMPLE_ROWS - ls)))
    n_phys = cache_fox_k.shape[1]
    kt_pages = cache_fox_k[0].transpose(0, 2, 3, 1).reshape(n_phys, MIX_W, PAGE_SIZE)
    vt_pages = cache_fox_v[0].transpose(0, 2, 3, 1).reshape(n_phys, MIX_W, PAGE_SIZE)
    lf_heads = cache_fox_logf[0].transpose(2, 0, 1)
    mix_s = _fox_decode(page_table, kt_pages, vt_pages, lf_heads, q8, kn8, vn8, lfn, ls)
    mix_s = mix_s[:, :ls].reshape(ts, MIX_W)

    mo_p = _mem_attend(qmp.reshape(b, s, MEM_W), mem_kt_p, mem_vt_p, 1, mem_q_norm_g[1],
                       tm_p, 1).reshape(tp, MEM_W)
    xs = _out_proj(xs, mix_s, mem_attend_sample(qms, 1), w_out, 1, tm_s)

    g_ffn1 = norm_ffn_g[1][None, :]
    wrt = w_router[0].T
    xp, hp, rp, cnt_p = _out_proj_router(xp, mix_pt, mo_p, w_out, 1, g_ffn1, wrt, tm_p)
    hs, rs, cnt_s = _router(xs, g_ffn1, wrt, tm_s)
    cnt_p = cnt_p[:, 0].astype(jnp.int32)
    cnt_s = cnt_s[:, 0].astype(jnp.int32)
    counts = cnt_p + cnt_s
    offsets = jnp.cumsum(counts) - counts
    n_sorted = 2 * (tp + ts)

    def dests(r, base):
        i1, i2 = r[0].astype(jnp.int32), r[1].astype(jnp.int32)
        return base[i1] + r[2].astype(jnp.int32), base[i2] + r[3].astype(jnp.int32)

    d1p, d2p = dests(rp, offsets)
    d1s, d2s = dests(rs, offsets + cnt_p)
    x_sorted = _scatter_rows(d1p, d2p, d1s, d2s, hp, hs, n_sorted)
    o_sorted = _gmm(_gmm_meta(counts, n_sorted), x_sorted, w_exp_gu[0], w_exp_down[0])
    yp = _combine(d1p, d2p, xp, rp[4:6].T, o_sorted, 256)
    ys = _combine(d1s, d2s, xs, rs[4:6].T, o_sorted, ts)

    def heads_t(a_t, n_b, n_h):
        return a_t.reshape(n_b, n_h, HEAD_DIM, a_t.shape[-1]).transpose(0, 3, 1, 2)

    mem_k_p = jnp.stack([heads_t(mem_kt_p[i], b, MEM_HEADS) for i in range(2)])
    mem_v_p = jnp.stack([heads_t(mem_vt_p[i], b, MEM_HEADS) for i in range(2)])
    fk_p = heads_t(ktp, b, FOX_HEADS)[None]
    fv_p = heads_t(vtp, b, FOX_HEADS)[None]
    fl_p = lftp[:FOX_HEADS].reshape(FOX_HEADS, b, s).transpose(1, 2, 0)[None]
    fk_s = kns.reshape(1, bs, ls, FOX_HEADS, HEAD_DIM)
    fv_s = vns.reshape(1, bs, ls, FOX_HEADS, HEAD_DIM)
    fl_s = lfts[:FOX_HEADS].reshape(FOX_HEADS, bs, ls).transpose(1, 2, 0)[None]
    return (yp.reshape(b, s, d), ys.reshape(bs, ls, d), mem_k_p, mem_v_p, pool_p, pool_s,
            fk_p, fv_p, fl_p, fk_s, fv_s, fl_s)
```

```python
import functools
import math

import jax
import jax.numpy as jnp
from jax import lax
from jax.experimental import pallas as pl
from jax.experimental.pallas import tpu as pltpu

F32 = jnp.float32
BF16 = jnp.bfloat16

D_MODEL = 1024
HEAD_DIM = 64
MEM_LEN = 256
MEM_HEADS = 4
MEM_W = MEM_HEADS * HEAD_DIM
MIX_W = D_MODEL - MEM_W
FOX_HEADS = MIX_W // HEAD_DIM
POOL_WINDOWS = (2, 4, 8, 16)
POOL_GW = MIX_W // len(POOL_WINDOWS)
POOL_STATE = max(POOL_WINDOWS) - 1
D_FF = 2816
N_EXPERTS = 8
PAGE_SIZE = 128
EPS = 1e-6
SCALE = HEAD_DIM ** -0.5
LOG2E = math.log2(math.e)

LANES = 128
SUBLANES = 8
MXU_N = 256
VMEM_LIMIT = 56 * 1024 * 1024

FF_CHUNK = MXU_N
N_FF_CHUNKS = D_FF // FF_CHUNK
MOE_TM = 768
PAGES_PER_STEP = 8
SAMPLE_ROWS = 8
DMA_UNROLL = 8
DECODE_SLOTS = 3
FLASH_HEADS = 12
MOE_FF_SPLIT = 2
MOE_SUB_ROWS = 256


def _cparams(*sem, vmem=None):
    return pltpu.CompilerParams(dimension_semantics=sem, vmem_limit_bytes=vmem)


def _dot(a, b):
    return jnp.dot(a, b, preferred_element_type=F32)


def _dot_nt(a, b):
    return lax.dot_general(a, b, (((1,), (1,)), ((), ())), preferred_element_type=F32)


def _rms_rows(x, g):
    return x * lax.rsqrt(jnp.mean(x * x, axis=-1, keepdims=True) + EPS) * g


def _split3(x):
    hi = x.astype(BF16)
    r = x - hi.astype(F32)
    mid = r.astype(BF16)
    lo = (r - mid.astype(F32)).astype(BF16)
    return hi, mid, lo


def _head_sumsq_lanes(x):
    r = lax.broadcasted_iota(jnp.int32, (LANES, LANES), 0) // HEAD_DIM
    c = lax.broadcasted_iota(jnp.int32, (LANES, LANES), 1) // HEAD_DIM
    ones_bd = (r == c).astype(BF16)
    xx = x * x
    hi = xx.astype(BF16)
    lo = (xx - hi.astype(F32)).astype(BF16)
    parts = []
    for j in range(x.shape[1] // LANES):
        sl = slice(j * LANES, (j + 1) * LANES)
        parts.append(_dot(hi[:, sl], ones_bd) + _dot(lo[:, sl], ones_bd))
    return jnp.concatenate(parts, axis=1)


def _head_rms_lanes(x, g_tiled):
    ssq = _head_sumsq_lanes(x)
    return x * lax.rsqrt(ssq * (1.0 / HEAD_DIM) + EPS) * g_tiled


def _head_rms_rows_t(xt, g_col, n_heads):
    outs = []
    for h in range(n_heads):
        blk = xt[h * HEAD_DIM:(h + 1) * HEAD_DIM, :]
        ms = jnp.mean(blk * blk, axis=0, keepdims=True)
        outs.append(blk * lax.rsqrt(ms + EPS) * g_col)
    return outs


def _log_sigmoid(x):
    return jnp.minimum(x, 0.0) - jnp.log1p(jnp.exp(-jnp.abs(x)))


def _lane_head(shape, axis):
    return lax.broadcasted_iota(jnp.int32, shape, axis) // HEAD_DIM


def _memkv_body(mem_ref, gm_ref, w_ref, kg_ref, kt_ref, vt_ref):
    h = _rms_rows(mem_ref[0], gm_ref[0])
    z = _dot(h, w_ref[0])
    kt = z[:, :MEM_W].T
    vt_ref[0, 0] = z[:, MEM_W:].T
    pieces = _head_rms_rows_t(kt, kg_ref[0], MEM_HEADS)
    for h_i, p in enumerate(pieces):
        kt_ref[0, 0, h_i * HEAD_DIM:(h_i + 1) * HEAD_DIM, :] = p


def _memkv(mem, g_mem, w_kv, kn_g):
    depth, batch = w_kv.shape[0], mem.shape[0]
    out = jax.ShapeDtypeStruct((depth, batch, MEM_W, MEM_LEN), F32)
    return pl.pallas_call(
        _memkv_body,
        grid=(depth, batch),
        in_specs=[
            pl.BlockSpec((1, MEM_LEN, D_MODEL), lambda i, b: (b, 0, 0)),
            pl.BlockSpec((1, 1, D_MODEL), lambda i, b: (i, 0, 0)),
            pl.BlockSpec((1, D_MODEL, 2 * MEM_W), lambda i, b: (i, 0, 0)),
            pl.BlockSpec((1, HEAD_DIM, 1), lambda i, b: (i, 0, 0)),
        ],
        out_specs=[pl.BlockSpec((1, 1, MEM_W, MEM_LEN), lambda i, b: (i, b, 0, 0))] * 2,
        out_shape=[out, out],
        compiler_params=_cparams("arbitrary", "arbitrary"),
        name="memkv",
    )(mem, g_mem[:, None, :], w_kv, kn_g[:, :, None])


def _pool_proj_body(x_ref, g_ref, w_ref, u_ref, qm_ref):
    h = _rms_rows(x_ref[...], g_ref[...])
    u_ref[...] = _dot(h, w_ref[:, :MIX_W])
    qm_ref[...] = _dot(h, w_ref[:, MIX_W:])


def _pool_proj(x, g, w, tm):
    t = x.shape[0]
    return pl.pallas_call(
        _pool_proj_body,
        grid=(t // tm,),
        in_specs=[
            pl.BlockSpec((tm, D_MODEL), lambda i: (i, 0)),
            pl.BlockSpec((1, D_MODEL), lambda i: (0, 0)),
            pl.BlockSpec((D_MODEL, D_MODEL), lambda i: (0, 0)),
        ],
        out_specs=[pl.BlockSpec((tm, MIX_W), lambda i: (i, 0)),
                   pl.BlockSpec((tm, MEM_W), lambda i: (i, 0))],
        out_shape=[jax.ShapeDtypeStruct((t, MIX_W), F32), jax.ShapeDtypeStruct((t, MEM_W), F32)],
        compiler_params=_cparams("arbitrary", vmem=VMEM_LIMIT),
        name="pool_proj",
    )(x, g, w)


def _window_of_lane(shape):
    lane = lax.broadcasted_iota(jnp.int32, shape, len(shape) - 1)
    return jnp.where(lane < POOL_GW, POOL_WINDOWS[0],
                     jnp.where(lane < 2 * POOL_GW, POOL_WINDOWS[1],
                               jnp.where(lane < 3 * POOL_GW, POOL_WINDOWS[2], POOL_WINDOWS[3])))


def _pool_select(s2, s4, s8, s16, shape):
    lane = lax.broadcasted_iota(jnp.int32, shape, len(shape) - 1)
    return jnp.where(lane < POOL_GW, s2,
                     jnp.where(lane < 2 * POOL_GW, s4, jnp.where(lane < 3 * POOL_GW, s8, s16)))


_POOL_PAD = SUBLANES
_POOL_HALO = 2 * SUBLANES
_POOL_BASE = _POOL_PAD + _POOL_HALO


def _pool_mix_tile(u, li, tl, e_ref, s2_ref, s4_ref, s8_ref, w_bd, scale):
    n = _POOL_HALO + tl

    @pl.when(li == 0)
    def _():
        e_ref[0:_POOL_BASE, :] = jnp.zeros((_POOL_BASE, MIX_W), F32)
        s2_ref[0:_POOL_PAD, :] = jnp.zeros((_POOL_PAD, MIX_W), F32)
        s4_ref[0:_POOL_PAD, :] = jnp.zeros((_POOL_PAD, MIX_W), F32)
        s8_ref[0:_POOL_PAD, :] = jnp.zeros((_POOL_PAD, MIX_W), F32)

    e_ref[_POOL_BASE:_POOL_BASE + tl, :] = u
    s2 = e_ref[_POOL_PAD:_POOL_PAD + n, :] + e_ref[_POOL_PAD - 1:_POOL_PAD - 1 + n, :]
    s2_ref[_POOL_PAD:_POOL_PAD + n, :] = s2
    s4 = s2 + s2_ref[_POOL_PAD - 2:_POOL_PAD - 2 + n, :]
    s4_ref[_POOL_PAD:_POOL_PAD + n, :] = s4
    s8 = s4 + s4_ref[_POOL_PAD - 4:_POOL_PAD - 4 + n, :]
    s8_ref[_POOL_PAD:_POOL_PAD + n, :] = s8
    s16 = s8[_POOL_HALO:, :] + s8_ref[_POOL_BASE - 8:_POOL_BASE - 8 + tl, :]
    shape = (tl, MIX_W)
    ssel = _pool_select(s2[_POOL_HALO:, :], s4[_POOL_HALO:, :], s8[_POOL_HALO:, :], s16, shape)
    pos = li * tl + lax.broadcasted_iota(jnp.int32, shape, 0)
    cnt = jnp.minimum(_window_of_lane(shape), pos + 1).astype(F32)
    diff = ssel / cnt - u
    mix = _dot(diff, w_bd) * scale
    e_ref[_POOL_PAD:_POOL_BASE, :] = e_ref[_POOL_PAD + tl:_POOL_BASE + tl, :]
    return mix


def _pool_layer_prompt_body(x_ref, g_ref, win_ref, wbd_ref, sc_ref, kt_ref, vt_ref, qg_ref, wo_ref,
                            o_ref, st_ref, e_ref, s2_ref, s4_ref, s8_ref, *, tl):
    x = x_ref[0]
    h = _rms_rows(x, g_ref[...])
    u = _dot(h, win_ref[:, :MIX_W])
    qm = _dot(h, win_ref[:, MIX_W:])
    st_ref[0] = u[tl - _POOL_HALO:, :]
    mix = _pool_mix_tile(u, pl.program_id(1), tl, e_ref, s2_ref, s4_ref, s8_ref, wbd_ref[...], sc_ref[...])
    mem = _mem_attend_tile(qm, kt_ref[0], vt_ref[0], qg_ref[...])
    o_ref[0] = x + _dot(mix, wo_ref[:MIX_W, :]) + _dot(mem, wo_ref[MIX_W:, :])


def _pool_layer_prompt(x, g, w_in, w_bd, scale, mem_kt, mem_vt, qn_g, w_out, tl):
    b, s, _ = x.shape
    rows = _POOL_BASE + tl
    const = lambda shape: pl.BlockSpec(shape, lambda bi, li: tuple(0 for _ in shape))
    per_b = lambda shape: pl.BlockSpec(shape, lambda bi, li: (bi, 0, 0))
    return pl.pallas_call(
        functools.partial(_pool_layer_prompt_body, tl=tl),
        grid=(b, s // tl),
        in_specs=[
            pl.BlockSpec((1, tl, D_MODEL), lambda bi, li: (bi, li, 0)),
            const((1, D_MODEL)), const((D_MODEL, D_MODEL)), const((MIX_W, MIX_W)), const((1, MIX_W)),
            per_b((1, MEM_W, MEM_LEN)), per_b((1, MEM_W, MEM_LEN)), const((1, MEM_W)),
            const((D_MODEL, D_MODEL)),
        ],
        out_specs=[pl.BlockSpec((1, tl, D_MODEL), lambda bi, li: (bi, li, 0)),
                   per_b((1, _POOL_HALO, MIX_W))],
        out_shape=[jax.ShapeDtypeStruct((b, s, D_MODEL), F32),
                   jax.ShapeDtypeStruct((b, _POOL_HALO, MIX_W), F32)],
        scratch_shapes=[pltpu.VMEM((rows, MIX_W), F32)] * 4,
        compiler_params=_cparams("arbitrary", "arbitrary", vmem=VMEM_LIMIT),
        name="pool_layer_prompt",
    )(x, g, w_in, w_bd, scale, mem_kt, mem_vt, jnp.tile(qn_g, MEM_HEADS)[None, :], w_out)


def _pool_sample_body(st_ref, u_ref, w_ref, sc_ref, o_ref, ns_ref, *, n_new):
    def ext(j):
        return st_ref[j] if j < POOL_STATE else u_ref[j - POOL_STATE]

    for l in range(n_new):
        r = POOL_STATE + l
        s2 = ext(r) + ext(r - 1)
        s4 = s2 + ext(r - 2) + ext(r - 3)
        s8 = s4
        for j in range(4, 8):
            s8 = s8 + ext(r - j)
        s16 = s8
        for j in range(8, 16):
            s16 = s16 + ext(r - j)
        shape = s2.shape
        cnt = _window_of_lane(shape).astype(F32)
        diff = _pool_select(s2, s4, s8, s16, shape) / cnt - ext(r)
        o_ref[l] = _dot(diff, w_ref[...]) * sc_ref[...]
    for j in range(POOL_STATE):
        ns_ref[j] = ext(j + n_new)


def _pool_sample(state_t, u_t, w_bd, scale):
    n_new, b, _ = u_t.shape
    return pl.pallas_call(
        functools.partial(_pool_sample_body, n_new=n_new),
        out_shape=[jax.ShapeDtypeStruct((n_new, b, MIX_W), F32),
                   jax.ShapeDtypeStruct((POOL_STATE, b, MIX_W), F32)],
        name="pool_sample",
    )(state_t, u_t, w_bd, scale)


def _mem_attend_tile(qm, kt, vt, g_tiled):
    q = _head_rms_lanes(qm, g_tiled) * SCALE
    lh = _lane_head(q.shape, 1)
    out = jnp.zeros(q.shape, F32)
    for h in range(MEM_HEADS):
        s = _dot(jnp.where(lh == h, q, 0.0), kt)
        e = jnp.exp(s - jnp.max(s, axis=-1, keepdims=True))
        p = e / jnp.sum(e, axis=-1, keepdims=True)
        out = jnp.where(lh == h, _dot_nt(p, vt), out)
    return out


def _mem_attend_body(q_ref, kt_ref, vt_ref, g_ref, o_ref):
    for i in range(q_ref.shape[0]):
        o_ref[i] = _mem_attend_tile(q_ref[i], kt_ref[0, i], vt_ref[0, i], g_ref[...])


def _mem_attend(qm, kt_layers, vt_layers, layer, qn_g, tl, bb):
    b, l, _ = qm.shape
    kv_spec = pl.BlockSpec((1, bb, MEM_W, MEM_LEN), lambda bi, li: (layer, bi, 0, 0))
    return pl.pallas_call(
        _mem_attend_body,
        grid=(b // bb, l // tl),
        in_specs=[
            pl.BlockSpec((bb, tl, MEM_W), lambda bi, li: (bi, li, 0)),
            kv_spec, kv_spec,
            pl.BlockSpec((1, MEM_W), lambda bi, li: (0, 0)),
        ],
        out_specs=pl.BlockSpec((bb, tl, MEM_W), lambda bi, li: (bi, li, 0)),
        out_shape=jax.ShapeDtypeStruct((b, l, MEM_W), F32),
        compiler_params=_cparams("arbitrary", "arbitrary"),
        name="mem_attend",
    )(qm, kt_layers, vt_layers, jnp.tile(qn_g, MEM_HEADS)[None, :])


def _out_proj_body(x_ref, mix_ref, mem_ref, w_ref, o_ref):
    o_ref[...] = (x_ref[...] + _dot(mix_ref[...], w_ref[0, :MIX_W, :])
                  + _dot(mem_ref[...], w_ref[0, MIX_W:, :]))


def _out_proj(x, mix, mem, w_layers, layer, tm):
    t = x.shape[0]
    return pl.pallas_call(
        _out_proj_body,
        grid=(t // tm,),
        in_specs=[
            pl.BlockSpec((tm, D_MODEL), lambda i: (i, 0)),
            pl.BlockSpec((tm, MIX_W), lambda i: (i, 0)),
            pl.BlockSpec((tm, MEM_W), lambda i: (i, 0)),
            pl.BlockSpec((1, D_MODEL, D_MODEL), lambda i: (layer, 0, 0)),
        ],
        out_specs=pl.BlockSpec((tm, D_MODEL), lambda i: (i, 0)),
        out_shape=jax.ShapeDtypeStruct((t, D_MODEL), F32),
        compiler_params=_cparams("arbitrary", vmem=VMEM_LIMIT),
        name="out_proj",
    )(x, mix, mem, w_layers)


def _silu(g):
    return g / (1.0 + jnp.exp(-g))


def _ffn_body(x_ref, g_ref, wgu_ref, wd_ref, o_ref):
    x = x_ref[...]
    h = _rms_rows(x, g_ref[...])
    o_ref[...] = x
    for c in range(N_FF_CHUNKS):
        gate = slice(c * FF_CHUNK, (c + 1) * FF_CHUNK)
        up = slice(D_FF + c * FF_CHUNK, D_FF + (c + 1) * FF_CHUNK)
        a = _silu(_dot(h, wgu_ref[:, gate])) * _dot(h, wgu_ref[:, up])
        o_ref[...] += _dot(a, wd_ref[gate, :])


def _ffn(x, g, w_gu, w_down, tm):
    t = x.shape[0]
    resident = lambda shape: pl.BlockSpec(shape, lambda i: (0, 0), pipeline_mode=pl.Buffered(1))
    return pl.pallas_call(
        _ffn_body,
        grid=(t // tm,),
        in_specs=[
            pl.BlockSpec((tm, D_MODEL), lambda i: (i, 0)),
            pl.BlockSpec((1, D_MODEL), lambda i: (0, 0)),
            resident((D_MODEL, 2 * D_FF)),
            resident((D_FF, D_MODEL)),
        ],
        out_specs=pl.BlockSpec((tm, D_MODEL), lambda i: (i, 0)),
        out_shape=jax.ShapeDtypeStruct((t, D_MODEL), F32),
        compiler_params=_cparams("arbitrary", vmem=VMEM_LIMIT),
        name="ffn",
    )(x, g, w_gu, w_down)


_FL_ROWS = 2 * SUBLANES
_AUG0 = HEAD_DIM


def _aug_pieces(c_col):
    hi, mid, lo = _split3(c_col)
    bc = lambda a: jnp.broadcast_to(a.astype(F32), (c_col.shape[0], LANES))
    return bc(hi), bc(mid), bc(lo)


def _fox_proj_prompt_body(x_ref, g_ref, wq_ref, wk_ref, wv_ref, wqm_ref, wf_ref, bf_ref, qg_ref, kg_ref,
                          qp_ref, kp_ref, kt_ref, vt_ref, vtb_ref, lft_ref, qm_ref, carry_ref, *, tm, per_seq):
    @pl.when(pl.program_id(0) % per_seq == 0)
    def _():
        carry_ref[...] = jnp.zeros(carry_ref.shape, F32)

    h = _rms_rows(x_ref[...], g_ref[...])
    q_pieces = _head_rms_rows_t(_dot_nt(wq_ref[...], h), qg_ref[...], FOX_HEADS)
    k_pieces = _head_rms_rows_t(_dot_nt(wk_ref[...], h), kg_ref[...], FOX_HEADS)
    for h_i, piece in enumerate(k_pieces):
        kt_ref[0, h_i * HEAD_DIM:(h_i + 1) * HEAD_DIM, :] = piece
    vt = _dot_nt(wv_ref[...], h)
    vt_ref[0] = vt
    vtb_ref[0] = vt.astype(BF16)
    q = jnp.concatenate(q_pieces, axis=0).T * (SCALE * LOG2E)
    k = jnp.concatenate(k_pieces, axis=0).T
    qm_ref[...] = _dot(h, wqm_ref[...])
    lane = lax.broadcasted_iota(jnp.int32, (tm, LANES), 1)
    lf = jnp.where(lane < FOX_HEADS, _log_sigmoid(_dot(h, wf_ref[...]) + bf_ref[...]), 0.0)
    lft_ref[...] = lf.T[:_FL_ROWS, :]
    r = lax.broadcasted_iota(jnp.int32, (tm, tm), 0)
    c = lax.broadcasted_iota(jnp.int32, (tm, tm), 1)
    lower = (c <= r).astype(BF16)
    hi, mid, lo = _split3(lf)
    csum = _dot(lower, hi) + _dot(lower, mid) + _dot(lower, lo) + carry_ref[0:1, :]
    carry_ref[...] = jnp.broadcast_to(csum[tm - 1:tm, :], carry_ref.shape)
    c2 = csum * LOG2E
    one = jnp.ones((tm, LANES), F32)
    zero = jnp.zeros((tm, LANES), F32)
    for hd in range(FOX_HEADS):
        chi, cmid, clo = _aug_pieces(c2[:, hd:hd + 1])
        aug_q = jnp.where(lane == _AUG0, chi, jnp.where(lane == _AUG0 + 1, cmid, jnp.where(
            lane == _AUG0 + 2, clo, jnp.where(lane < _AUG0 + 6, one, zero))))
        aug_k = jnp.where(lane < _AUG0 + 3, one, jnp.where(lane == _AUG0 + 3, -chi, jnp.where(
            lane == _AUG0 + 4, -cmid, jnp.where(lane == _AUG0 + 5, -clo, zero))))
        col = slice((hd // 2) * LANES, (hd // 2 + 1) * LANES)
        qc, kc = q[:, col], k[:, col]
        if hd % 2:
            qc = pltpu.roll(qc, HEAD_DIM, 1)
            kc = pltpu.roll(kc, HEAD_DIM, 1)
        qp_ref[0, hd] = jnp.where(lane < HEAD_DIM, qc, aug_q).astype(BF16)
        kp_ref[0, hd] = jnp.where(lane < HEAD_DIM, kc, aug_k).astype(BF16)


def _fox_proj_prompt(x, g, wq_t, wk_t, wv_t, wqm, wf, bf_row, qg_col, kg_col, tm, n_seq):
    t = x.shape[0]
    seq = t // n_seq
    per_seq = seq // tm

    def const(shape):
        return pl.BlockSpec(shape, lambda i: tuple(0 for _ in shape))

    heads_spec = pl.BlockSpec((1, FOX_HEADS, tm, LANES), lambda i: (i // per_seq, 0, i % per_seq, 0))
    t_spec = pl.BlockSpec((1, MIX_W, tm), lambda i: (i // per_seq, 0, i % per_seq))
    return pl.pallas_call(
        functools.partial(_fox_proj_prompt_body, tm=tm, per_seq=per_seq),
        grid=(t // tm,),
        in_specs=[
            pl.BlockSpec((tm, D_MODEL), lambda i: (i, 0)),
            const((1, D_MODEL)),
            const((MIX_W, D_MODEL)), const((MIX_W, D_MODEL)), const((MIX_W, D_MODEL)),
            const((D_MODEL, MEM_W)), const((D_MODEL, LANES)),
            const((1, LANES)), const((HEAD_DIM, 1)), const((HEAD_DIM, 1)),
        ],
        out_specs=[heads_spec, heads_spec, t_spec, t_spec, t_spec,
                   pl.BlockSpec((_FL_ROWS, tm), lambda i: (0, i)),
                   pl.BlockSpec((tm, MEM_W), lambda i: (i, 0))],
        out_shape=[
            jax.ShapeDtypeStruct((n_seq, FOX_HEADS, seq, LANES), BF16),
            jax.ShapeDtypeStruct((n_seq, FOX_HEADS, seq, LANES), BF16),
            jax.ShapeDtypeStruct((n_seq, MIX_W, seq), F32),
            jax.ShapeDtypeStruct((n_seq, MIX_W, seq), F32),
            jax.ShapeDtypeStruct((n_seq, MIX_W, seq), BF16),
            jax.ShapeDtypeStruct((_FL_ROWS, t), F32),
            jax.ShapeDtypeStruct((t, MEM_W), F32),
        ],
        scratch_shapes=[pltpu.VMEM((SUBLANES, LANES), F32)],
        compiler_params=_cparams("arbitrary", vmem=VMEM_LIMIT),
        name="fox_proj_prompt",
    )(x, g, wq_t, wk_t, wv_t, wqm, wf, bf_row, qg_col, kg_col)


def _fox_proj_sample_body(x_ref, g_ref, wq_ref, wk_ref, wv_ref, wqm_ref, wf_ref, bf_ref, qg_ref, kg_ref,
                          q_ref, k_ref, v_ref, lft_ref, qm_ref):
    h = _rms_rows(x_ref[...], g_ref[...])
    q_ref[...] = _head_rms_lanes(_dot(h, wq_ref[...]), qg_ref[...])
    k_ref[...] = _head_rms_lanes(_dot(h, wk_ref[...]), kg_ref[...])
    v_ref[...] = _dot(h, wv_ref[...])
    qm_ref[...] = _dot(h, wqm_ref[...])
    lane = lax.broadcasted_iota(jnp.int32, (x_ref.shape[0], LANES), 1)
    lf = jnp.where(lane < FOX_HEADS, _log_sigmoid(_dot(h, wf_ref[...]) + bf_ref[...]), 0.0)
    lft_ref[...] = lf.T[:_FL_ROWS, :]


def _fox_proj_sample(x, g, wq, wk, wv, wqm, wf, bf_row, qg_t, kg_t):
    t = x.shape[0]
    rows = jax.ShapeDtypeStruct((t, MIX_W), F32)
    return pl.pallas_call(
        _fox_proj_sample_body,
        out_shape=[rows, rows, rows, jax.ShapeDtypeStruct((_FL_ROWS, t), F32),
                   jax.ShapeDtypeStruct((t, MEM_W), F32)],
        compiler_params=_cparams(vmem=VMEM_LIMIT),
        name="fox_proj_sample",
    )(x, g, wq, wk, wv, wqm, wf, bf_row, qg_t, kg_t)


def _fox_flash_body(qi_ref, ki_ref, qp_ref, kp_ref, vt_ref, o_ref, m_ref, l_ref, acc_ref, *, blk):
    p = pl.program_id(2)
    qi = qi_ref[p]
    ki = ki_ref[p]

    @pl.when(ki == 0)
    def _():
        m_ref[...] = jnp.full(m_ref.shape, -jnp.inf, F32)
        l_ref[...] = jnp.zeros(l_ref.shape, F32)
        acc_ref[...] = jnp.zeros(acc_ref.shape, F32)

    def step(diagonal):
        for hh in range(FLASH_HEADS):
            pair = hh // 2
            vt = vt_ref[0, pair * LANES:(pair + 1) * LANES, :]
            st = _dot_nt(kp_ref[0, hh], qp_ref[0, hh])
            if diagonal:
                key = lax.broadcasted_iota(jnp.int32, st.shape, 0)
                qry = lax.broadcasted_iota(jnp.int32, st.shape, 1)
                st = jnp.where(key <= qry, st, -jnp.inf)
            m_prev = m_ref[hh]
            m_new = jnp.maximum(m_prev, jnp.max(st, axis=0, keepdims=True))
            alpha = jnp.exp2(m_prev - m_new)
            pt = jnp.exp2(st - m_new)
            l_ref[hh] = alpha * l_ref[hh] + jnp.sum(pt, axis=0, keepdims=True)
            acc_ref[hh] = alpha * acc_ref[hh] + _dot(vt, pt.astype(BF16))
            m_ref[hh] = m_new

    @pl.when(ki < qi)
    def _():
        step(False)

    @pl.when(ki == qi)
    def _():
        step(True)
        row = lax.broadcasted_iota(jnp.int32, (LANES, blk), 0)
        for pair in range(FLASH_HEADS // 2):
            a, c = 2 * pair, 2 * pair + 1
            o_ref[0, pair * LANES:(pair + 1) * LANES, :] = jnp.where(
                row < HEAD_DIM, acc_ref[a] / l_ref[a], acc_ref[c] / l_ref[c])


def _fox_flash(qp, kp, vt, blk):
    b, _, s, _ = qp.shape
    n = s // blk
    pairs = [(qi, ki) for qi in range(n) for ki in range(qi + 1)]
    qi_tab = jnp.asarray([p[0] for p in pairs], jnp.int32)
    ki_tab = jnp.asarray([p[1] for p in pairs], jnp.int32)
    fh = FLASH_HEADS
    ch = fh // 2 * LANES
    grid_spec = pltpu.PrefetchScalarGridSpec(
        num_scalar_prefetch=2,
        grid=(b, FOX_HEADS // fh, len(pairs)),
        in_specs=[
            pl.BlockSpec((1, fh, blk, LANES), lambda bi, hg, p, qt, kt: (bi, hg, qt[p], 0)),
            pl.BlockSpec((1, fh, blk, LANES), lambda bi, hg, p, qt, kt: (bi, hg, kt[p], 0)),
            pl.BlockSpec((1, ch, blk), lambda bi, hg, p, qt, kt: (bi, hg, kt[p])),
        ],
        out_specs=pl.BlockSpec((1, ch, blk), lambda bi, hg, p, qt, kt: (bi, hg, qt[p])),
        scratch_shapes=[pltpu.VMEM((fh, 1, blk), F32), pltpu.VMEM((fh, 1, blk), F32),
                        pltpu.VMEM((fh, LANES, blk), F32)],
    )
    return pl.pallas_call(
        functools.partial(_fox_flash_body, blk=blk),
        grid_spec=grid_spec,
        out_shape=jax.ShapeDtypeStruct((b, MIX_W, s), F32),
        compiler_params=_cparams("arbitrary", "arbitrary", "arbitrary", vmem=VMEM_LIMIT),
        name="fox_flash",
    )(qi_tab, ki_tab, qp, kp, vt)


_QROWS = FOX_HEADS * SAMPLE_ROWS
_CHUNK = PAGES_PER_STEP * PAGE_SIZE


def _expand_heads(x):
    n = x.shape[1]
    return jnp.broadcast_to(x[:FOX_HEADS, None, :], (FOX_HEADS, SAMPLE_ROWS, n)).reshape(_QROWS, n)


def _page_copies(pt_ref, kt_hbm, vt_hbm, lf_hbm, kt_buf, vt_buf, lf_buf, sem, g, slot, steps, n_pages):
    bi = g // steps
    first = n_pages - (g % steps + 1) * PAGES_PER_STEP
    copies = []
    for i in range(PAGES_PER_STEP):
        page = pt_ref[bi, first + i]
        lanes = pl.ds(i * PAGE_SIZE, PAGE_SIZE)
        copies.append(pltpu.make_async_copy(kt_hbm.at[page], kt_buf.at[slot, :, lanes], sem.at[slot, 0]))
        copies.append(pltpu.make_async_copy(vt_hbm.at[page], vt_buf.at[slot, :, lanes], sem.at[slot, 1]))
        copies.append(pltpu.make_async_copy(lf_hbm.at[:, page], lf_buf.at[slot, i, pl.ds(0, FOX_HEADS)],
                                            sem.at[slot, 2]))
    return copies


def _fox_decode_body(pt_ref, kt_hbm, vt_hbm, lf_hbm, q_ref, kn_ref, vn_ref, lfn_ref, o_ref,
                     kt_buf, vt_buf, lf_buf, sem, qbd_ref, m_ref, l_ref, acc_ref, carry_ref, crow_ref,
                     *, n_new, steps, n_pages):
    g = pl.program_id(0)
    n_steps = pl.num_programs(0)
    j = g % steps
    slot = g % DECODE_SLOTS
    copies = functools.partial(_page_copies, pt_ref, kt_hbm, vt_hbm, lf_hbm, kt_buf, vt_buf, lf_buf, sem,
                               steps=steps, n_pages=n_pages)
    row_l = lax.broadcasted_iota(jnp.int32, (_QROWS, 1), 0) % SAMPLE_ROWS

    @pl.when(g == 0)
    def _():
        lf_buf[...] = jnp.zeros(lf_buf.shape, F32)
        for ahead in range(DECODE_SLOTS - 1):
            for cp in copies(g=ahead, slot=ahead):
                cp.start()

    @pl.when(g + DECODE_SLOTS - 1 < n_steps)
    def _():
        for cp in copies(g=g + DECODE_SLOTS - 1, slot=(g + DECODE_SLOTS - 1) % DECODE_SLOTS):
            cp.start()

    @pl.when(j == 0)
    def _():
        q = q_ref[0] * SCALE
        lh = _lane_head(q.shape, 1)
        for h in range(FOX_HEADS):
            qbd_ref[h * SAMPLE_ROWS:(h + 1) * SAMPLE_ROWS, :] = jnp.where(lh == h, q, 0.0)
        lfn = lfn_ref[0]
        lane = lax.broadcasted_iota(jnp.int32, lfn.shape, 1)
        c = jnp.zeros(lfn.shape, F32)
        for m in range(n_new):
            cm = jnp.sum(jnp.where(lane <= m, lfn, 0.0), axis=1, keepdims=True)
            c = jnp.where(lane == m, cm, c)
        c_q = _expand_heads(c)
        col = lax.broadcasted_iota(jnp.int32, c_q.shape, 1)
        crow = jnp.sum(jnp.where(col == row_l, c_q, 0.0), axis=1, keepdims=True)
        crow_ref[...] = crow
        s = _dot_nt(qbd_ref[...], kn_ref[0]) + (crow - c_q)
        valid = (col < n_new) & ((col <= row_l) | (row_l >= n_new))
        s = jnp.where(valid, s, -jnp.inf)
        m0 = jnp.max(s, axis=-1, keepdims=True)
        p = jnp.exp(s - m0)
        m_ref[...] = m0
        l_ref[...] = jnp.sum(p, axis=-1, keepdims=True)
        acc_ref[...] = _dot(p, vn_ref[0])
        carry_ref[...] = jnp.zeros(carry_ref.shape, F32)

    for cp in copies(g=g, slot=slot):
        cp.wait()

    r = lax.broadcasted_iota(jnp.int32, (PAGE_SIZE, PAGE_SIZE), 0)
    cc = lax.broadcasted_iota(jnp.int32, (PAGE_SIZE, PAGE_SIZE), 1)
    later = (r > cc).astype(BF16)
    lf3 = lf_buf[slot]
    tot = jnp.sum(lf3, axis=2, keepdims=True)
    hi, mid, lo = _split3(lf3.reshape(PAGES_PER_STEP * _FL_ROWS, PAGE_SIZE))
    d_in = (_dot(hi, later) + _dot(mid, later) + _dot(lo, later)).reshape(PAGES_PER_STEP, _FL_ROWS, PAGE_SIZE)
    after = carry_ref[:, 0:1]
    d_pages = [None] * PAGES_PER_STEP
    for i in reversed(range(PAGES_PER_STEP)):
        d_pages[i] = _expand_heads(d_in[i] + after)
        after = after + tot[i]
    carry_ref[...] = jnp.broadcast_to(after, carry_ref.shape)
    d = jnp.concatenate(d_pages, axis=1)

    vt = vt_buf[slot]
    s = _dot(qbd_ref[...], kt_buf[slot]) + crow_ref[...] + d
    m_prev = m_ref[...]
    m_new = jnp.maximum(m_prev, jnp.max(s, axis=-1, keepdims=True))
    alpha = jnp.exp(m_prev - m_new)
    p = jnp.exp(s - m_new)
    l_ref[...] = alpha * l_ref[...] + jnp.sum(p, axis=-1, keepdims=True)
    acc_ref[...] = alpha * acc_ref[...] + _dot_nt(p, vt)
    m_ref[...] = m_new

    @pl.when(j == steps - 1)
    def _():
        res = acc_ref[...] / l_ref[...]
        lh = _lane_head((SAMPLE_ROWS, MIX_W), 1)
        out = jnp.zeros((SAMPLE_ROWS, MIX_W), F32)
        for h in range(FOX_HEADS):
            out = jnp.where(lh == h, res[h * SAMPLE_ROWS:(h + 1) * SAMPLE_ROWS, :], out)
        o_ref[0] = out


def _fox_decode(page_table, kt_pages, vt_pages, lf_heads, q8, kn8, vn8, lfn, n_new):
    b, n_pages = page_table.shape
    steps = n_pages // PAGES_PER_STEP
    per_b = lambda g, pt: (g // steps, 0, 0)
    any_spec = pl.BlockSpec(memory_space=pl.ANY)
    grid_spec = pltpu.PrefetchScalarGridSpec(
        num_scalar_prefetch=1,
        grid=(b * steps,),
        in_specs=[any_spec, any_spec, any_spec,
                  pl.BlockSpec((1, SAMPLE_ROWS, MIX_W), per_b),
                  pl.BlockSpec((1, SAMPLE_ROWS, MIX_W), per_b),
                  pl.BlockSpec((1, SAMPLE_ROWS, MIX_W), per_b),
                  pl.BlockSpec((1, _FL_ROWS, SAMPLE_ROWS), per_b)],
        out_specs=pl.BlockSpec((1, SAMPLE_ROWS, MIX_W), per_b),
        scratch_shapes=[
            pltpu.VMEM((DECODE_SLOTS, MIX_W, _CHUNK), F32),
            pltpu.VMEM((DECODE_SLOTS, MIX_W, _CHUNK), F32),
            pltpu.VMEM((DECODE_SLOTS, PAGES_PER_STEP, _FL_ROWS, PAGE_SIZE), F32),
            pltpu.SemaphoreType.DMA((DECODE_SLOTS, 3)),
            pltpu.VMEM((_QROWS, MIX_W), F32),
            pltpu.VMEM((_QROWS, 1), F32), pltpu.VMEM((_QROWS, 1), F32),
            pltpu.VMEM((_QROWS, MIX_W), F32),
            pltpu.VMEM((_FL_ROWS, LANES), F32),
            pltpu.VMEM((_QROWS, 1), F32),
        ],
    )
    return pl.pallas_call(
        functools.partial(_fox_decode_body, n_new=n_new, steps=steps, n_pages=n_pages),
        grid_spec=grid_spec,
        out_shape=jax.ShapeDtypeStruct((b, SAMPLE_ROWS, MIX_W), F32),
        compiler_params=_cparams("arbitrary", vmem=VMEM_LIMIT),
        name="fox_decode",
    )(page_table, kt_pages, vt_pages, lf_heads, q8, kn8, vn8, lfn)


def _router_body(x_ref, g_ref, wrt_ref, h_ref, r_ref, cnt_ref, carry_ref, *, tm):
    _route_tile(x_ref[...], g_ref, wrt_ref, h_ref, r_ref, cnt_ref, carry_ref, tm)


def _out_proj_router_body(x_ref, mixt_ref, mem_ref, wo_ref, g_ref, wrt_ref,
                          x1_ref, h_ref, r_ref, cnt_ref, carry_ref, *, tm):
    x1 = (x_ref[...] + _dot(mixt_ref[0].T, wo_ref[0, :MIX_W, :])
          + _dot(mem_ref[...], wo_ref[0, MIX_W:, :]))
    x1_ref[...] = x1
    _route_tile(x1, g_ref, wrt_ref, h_ref, r_ref, cnt_ref, carry_ref, tm)


def _route_tile(x, g_ref, wrt_ref, h_ref, r_ref, cnt_ref, carry_ref, tm):
    @pl.when(pl.program_id(0) == 0)
    def _():
        carry_ref[...] = jnp.zeros(carry_ref.shape, F32)

    h = _rms_rows(x, g_ref[...])
    h_ref[...] = h
    h_hi = h.astype(BF16)
    h_lo = (h - h_hi.astype(F32)).astype(BF16)
    w = wrt_ref[...]
    w_hi = w.astype(BF16)
    w_lo = (w - w_hi.astype(F32)).astype(BF16)
    lg = _dot_nt(w_hi, h_hi) + _dot_nt(w_hi, h_lo) + _dot_nt(w_lo, h_hi)
    idx = lax.broadcasted_iota(jnp.int32, lg.shape, 0)
    m1 = jnp.max(lg, axis=0, keepdims=True)
    i1 = jnp.min(jnp.where(lg == m1, idx, N_EXPERTS), axis=0, keepdims=True)
    sel1 = idx == i1
    lg2 = jnp.where(sel1, -jnp.inf, lg)
    m2 = jnp.max(lg2, axis=0, keepdims=True)
    i2 = jnp.min(jnp.where(lg2 == m2, idx, N_EXPERTS), axis=0, keepdims=True)
    sel2 = idx == i2
    e = jnp.exp(m2 - m1)
    g1 = 1.0 / (1.0 + e)
    g2 = e / (1.0 + e)
    assign = jnp.where(sel1 | sel2, 1.0, 0.0)
    r = lax.broadcasted_iota(jnp.int32, (tm, tm), 0)
    c = lax.broadcasted_iota(jnp.int32, (tm, tm), 1)
    before = (r < c).astype(BF16)
    rank = _dot(assign.astype(BF16), before) + carry_ref[:, 0:1]
    r1 = jnp.sum(jnp.where(sel1, rank, 0.0), axis=0, keepdims=True)
    r2 = jnp.sum(jnp.where(sel2, rank, 0.0), axis=0, keepdims=True)
    carry = carry_ref[...] + jnp.sum(assign, axis=1, keepdims=True)
    carry_ref[...] = carry
    cnt_ref[...] = carry
    rows = [i1.astype(F32), i2.astype(F32), r1, r2, g1, g2]
    out = jnp.zeros(lg.shape, F32)
    for k, v in enumerate(rows):
        out = jnp.where(idx == k, v, out)
    r_ref[...] = out


def _router(x, g, wrt, tm):
    t = x.shape[0]
    return pl.pallas_call(
        functools.partial(_router_body, tm=tm),
        grid=(t // tm,),
        in_specs=[
            pl.BlockSpec((tm, D_MODEL), lambda i: (i, 0)),
            pl.BlockSpec((1, D_MODEL), lambda i: (0, 0)),
            pl.BlockSpec((N_EXPERTS, D_MODEL), lambda i: (0, 0)),
        ],
        out_specs=[
            pl.BlockSpec((tm, D_MODEL), lambda i: (i, 0)),
            pl.BlockSpec((N_EXPERTS, tm), lambda i: (0, i)),
            pl.BlockSpec((N_EXPERTS, LANES), lambda i: (0, 0)),
        ],
        out_shape=[
            jax.ShapeDtypeStruct((t, D_MODEL), F32),
            jax.ShapeDtypeStruct((N_EXPERTS, t), F32),
            jax.ShapeDtypeStruct((N_EXPERTS, LANES), F32),
        ],
        scratch_shapes=[pltpu.VMEM((N_EXPERTS, LANES), F32)],
        compiler_params=_cparams("arbitrary", vmem=VMEM_LIMIT),
        name="router",
    )(x, g, wrt)


def _out_proj_router(x, mix_t, mem, w_layers, layer, g, wrt, tm):
    t = x.shape[0]
    per_seq = mix_t.shape[2] // tm
    return pl.pallas_call(
        functools.partial(_out_proj_router_body, tm=tm),
        grid=(t // tm,),
        in_specs=[
            pl.BlockSpec((tm, D_MODEL), lambda i: (i, 0)),
            pl.BlockSpec((1, MIX_W, tm), lambda i: (i // per_seq, 0, i % per_seq)),
            pl.BlockSpec((tm, MEM_W), lambda i: (i, 0)),
            pl.BlockSpec((1, D_MODEL, D_MODEL), lambda i: (layer, 0, 0)),
            pl.BlockSpec((1, D_MODEL), lambda i: (0, 0)),
            pl.BlockSpec((N_EXPERTS, D_MODEL), lambda i: (0, 0)),
        ],
        out_specs=[
            pl.BlockSpec((tm, D_MODEL), lambda i: (i, 0)),
            pl.BlockSpec((tm, D_MODEL), lambda i: (i, 0)),
            pl.BlockSpec((N_EXPERTS, tm), lambda i: (0, i)),
            pl.BlockSpec((N_EXPERTS, LANES), lambda i: (0, 0)),
        ],
        out_shape=[
            jax.ShapeDtypeStruct((t, D_MODEL), F32),
            jax.ShapeDtypeStruct((t, D_MODEL), F32),
            jax.ShapeDtypeStruct((N_EXPERTS, t), F32),
            jax.ShapeDtypeStruct((N_EXPERTS, LANES), F32),
        ],
        scratch_shapes=[pltpu.VMEM((N_EXPERTS, LANES), F32)],
        compiler_params=_cparams("arbitrary", vmem=VMEM_LIMIT),
        name="out_proj_router",
    )(x, mix_t, mem, w_layers, g, wrt)


_SCATTER_TM = 512


def _row_copy(src_ref, src_row, dst_ref, dst_row, sem):
    return pltpu.make_async_copy(src_ref.at[pl.ds(src_row, 1)], dst_ref.at[pl.ds(dst_row, 1)], sem)


def _scatter_body(d1_ref, d2_ref, hp_ref, hs_ref, xs_ref, sem, *, n_prompt_tiles):
    i = pl.program_id(0)

    def run(src_ref):
        n_rows = src_ref.shape[0]

        def start(r, carry):
            _row_copy(src_ref, r, xs_ref, d1_ref[0, 0, r], sem.at[0]).start(priority=0)
            _row_copy(src_ref, r, xs_ref, d2_ref[0, 0, r], sem.at[1]).start(priority=1)
            return carry

        lax.fori_loop(0, n_rows, start, 0, unroll=DMA_UNROLL)
        for k in range(2):
            pltpu.make_async_copy(src_ref, xs_ref.at[pl.ds(0, n_rows)], sem.at[k]).wait()

    @pl.when(i < n_prompt_tiles)
    def _():
        run(hp_ref)

    @pl.when(i >= n_prompt_tiles)
    def _():
        run(hs_ref)


def _scatter_rows(d1p, d2p, d1s, d2s, h_p, h_s, n_sorted):
    tm = _SCATTER_TM
    npt = h_p.shape[0] // tm
    n_s = h_s.shape[0]
    pad = jnp.zeros((tm - n_s,), jnp.int32)
    d1 = jnp.concatenate([d1p, d1s, pad])
    d2 = jnp.concatenate([d2p, d2s, pad])
    smem = lambda: pl.BlockSpec((1, 1, tm), lambda i: (i, 0, 0), memory_space=pltpu.SMEM)
    return pl.pallas_call(
        functools.partial(_scatter_body, n_prompt_tiles=npt),
        grid=(npt + 1,),
        in_specs=[
            smem(), smem(),
            pl.BlockSpec((tm, D_MODEL), lambda i: (jnp.minimum(i, npt - 1), 0)),
            pl.BlockSpec((n_s, D_MODEL), lambda i: (0, 0)),
        ],
        out_specs=pl.BlockSpec(memory_space=pl.ANY),
        out_shape=jax.ShapeDtypeStruct((n_sorted, D_MODEL), F32),
        scratch_shapes=[pltpu.SemaphoreType.DMA((2,))],
        compiler_params=_cparams("arbitrary", vmem=VMEM_LIMIT),
        name="moe_scatter",
    )(d1.reshape(-1, 1, tm), d2.reshape(-1, 1, tm), h_p, h_s)


def _gmm_body(tile_ref, exp_ref, lo_ref, hi_ref, first_ref, x_ref, wg_ref, wu_ref, wd_ref, o_ref):
    del tile_ref, exp_ref
    v = pl.program_id(0)
    c = pl.program_id(1)

    @pl.when((first_ref[v] == 1) & (c == 0))
    def _():
        o_ref[...] = jnp.zeros(o_ref.shape, F32)

    lo = lo_ref[v]
    hi = hi_ref[v]
    whole = (lo == 0) & (hi == MOE_TM)

    def swiglu_into(rows, mask):
        x = x_ref[rows, :]
        for off, n in _MOE_SUBS:
            cols = slice(off, off + n)
            a = _silu(_dot(x, wg_ref[0, :, cols])) * _dot(x, wu_ref[0, :, cols])
            y = _dot(a, wd_ref[0, cols, :])
            o_ref[rows, :] += y if mask is None else jnp.where(mask, y, 0.0)

    @pl.when(whole)
    def _():
        swiglu_into(slice(None), None)

    @pl.when(jnp.logical_not(whole) & (hi > lo))
    def _():
        for r0 in range(0, MOE_TM, MOE_SUB_ROWS):
            @pl.when((lo < r0 + MOE_SUB_ROWS) & (hi > r0))
            def _():
                rows = r0 + lax.broadcasted_iota(jnp.int32, (MOE_SUB_ROWS, 1), 0)
                swiglu_into(slice(r0, r0 + MOE_SUB_ROWS), (rows >= lo) & (rows < hi))


_MOE_FF = D_FF // MOE_FF_SPLIT
_MOE_SUBS = tuple((off, min(MXU_N, _MOE_FF - off)) for off in range(0, _MOE_FF, MXU_N))


def _gmm(meta, xs, w_gu, w_down):
    tile, expert, lo, hi, first = meta
    n_visits = tile.shape[0]
    grid_spec = pltpu.PrefetchScalarGridSpec(
        num_scalar_prefetch=5,
        grid=(n_visits, MOE_FF_SPLIT),
        in_specs=[
            pl.BlockSpec((MOE_TM, D_MODEL), lambda v, c, t, e, *_: (t[v], 0)),
            pl.BlockSpec((1, D_MODEL, _MOE_FF), lambda v, c, t, e, *_: (e[v], 0, c)),
            pl.BlockSpec((1, D_MODEL, _MOE_FF), lambda v, c, t, e, *_: (e[v], 0, MOE_FF_SPLIT + c)),
            pl.BlockSpec((1, _MOE_FF, D_MODEL), lambda v, c, t, e, *_: (e[v], c, 0)),
        ],
        out_specs=pl.BlockSpec((MOE_TM, D_MODEL), lambda v, c, t, e, *_: (t[v], 0)),
    )
    return pl.pallas_call(
        _gmm_body,
        grid_spec=grid_spec,
        out_shape=jax.ShapeDtypeStruct(xs.shape, F32),
        compiler_params=_cparams("arbitrary", "arbitrary", vmem=VMEM_LIMIT),
        name="moe_gmm",
    )(tile, expert, lo, hi, first, xs, w_gu, w_gu, w_down)


def _gmm_meta(counts, n_sorted):
    n_tiles = n_sorted // MOE_TM
    n_visits = n_tiles + N_EXPERTS - 1
    ends = jnp.cumsum(counts)
    starts = ends - counts
    first_tile = starts // MOE_TM
    last_tile = jnp.maximum(ends - 1, 0) // MOE_TM
    nv = jnp.where(counts > 0, last_tile - first_tile + 1, 0)
    cv = jnp.cumsum(nv)
    v = jnp.arange(n_visits, dtype=jnp.int32)
    total = cv[-1]
    valid = v < total
    vc = jnp.minimum(v, total - 1)
    expert = jnp.sum((cv[None, :] <= vc[:, None]).astype(jnp.int32), axis=1)
    tile = first_tile[expert] + (vc - (cv[expert] - nv[expert]))
    lo = jnp.maximum(starts[expert], tile * MOE_TM) - tile * MOE_TM
    hi = jnp.minimum(ends[expert], (tile + 1) * MOE_TM) - tile * MOE_TM
    lo = jnp.where(valid, lo, 0)
    hi = jnp.where(valid, hi, 0)
    prev_tile = jnp.concatenate([jnp.full((1,), -1, jnp.int32), tile[:-1]])
    first = (valid & (tile != prev_tile)).astype(jnp.int32)
    as_i32 = lambda a: a.astype(jnp.int32)
    return as_i32(tile), as_i32(expert), as_i32(lo), as_i32(hi), first


def _combine_body(d1_ref, d2_ref, x_ref, g_ref, os_ref, y_ref, buf_ref, sem, *, tm, n_tiles):
    i = pl.program_id(0)
    slot = i % 2

    def issue(tile, slot_):
        def body(r, carry):
            _row_copy(os_ref, d1_ref[tile, r], buf_ref.at[slot_, 0], r, sem.at[slot_, 0]).start(priority=0)
            _row_copy(os_ref, d2_ref[tile, r], buf_ref.at[slot_, 1], r, sem.at[slot_, 1]).start(priority=1)
            return carry
        lax.fori_loop(0, tm, body, 0, unroll=DMA_UNROLL)

    @pl.when(i == 0)
    def _():
        issue(0, 0)

    @pl.when(i + 1 < n_tiles)
    def _():
        issue(i + 1, 1 - slot)

    for k in range(2):
        pltpu.make_async_copy(os_ref.at[pl.ds(0, tm)], buf_ref.at[slot, k], sem.at[slot, k]).wait()

    g = g_ref[...]
    y_ref[...] = x_ref[...] + g[:, 0:1] * buf_ref[slot, 0] + g[:, 1:2] * buf_ref[slot, 1]


def _combine(d1, d2, x, gates, o_sorted, tm):
    t = x.shape[0]
    n_tiles = t // tm
    grid_spec = pltpu.PrefetchScalarGridSpec(
        num_scalar_prefetch=2,
        grid=(n_tiles,),
        in_specs=[
            pl.BlockSpec((tm, D_MODEL), lambda i, *_: (i, 0)),
            pl.BlockSpec((tm, 2), lambda i, *_: (i, 0)),
            pl.BlockSpec(memory_space=pl.ANY),
        ],
        out_specs=pl.BlockSpec((tm, D_MODEL), lambda i, *_: (i, 0)),
        scratch_shapes=[pltpu.VMEM((2, 2, tm, D_MODEL), F32), pltpu.SemaphoreType.DMA((2, 2))],
    )
    return pl.pallas_call(
        functools.partial(_combine_body, tm=tm, n_tiles=n_tiles),
        grid_spec=grid_spec,
        out_shape=jax.ShapeDtypeStruct((t, D_MODEL), F32),
        compiler_params=_cparams("arbitrary", vmem=VMEM_LIMIT),
        name="moe_combine",
    )(d1.reshape(n_tiles, tm), d2.reshape(n_tiles, tm), x, gates, o_sorted)


def _block_diag(w_grp):
    g, n, _ = w_grp.shape
    out = jnp.zeros((g * n, g * n), w_grp.dtype)
    for i in range(g):
        out = out.at[i * n:(i + 1) * n, i * n:(i + 1) * n].set(w_grp[i])
    return out


def kernel(x_prompt, x_sample, cache_mem_k, cache_mem_v, state_pool, cache_fox_k, cache_fox_v, cache_fox_logf, page_table, mem_prompt, norm_mix_g, norm_ffn_g, norm_mem_g, w_mem_kv, mem_q_norm_g, mem_k_norm_g, w_out, w_in_pool, w_pool_group, pool_scale, w_in_fox, b_forget, fox_q_norm_g, fox_k_norm_g, w_ffn_gu, w_ffn_down, w_router, w_exp_gu, w_exp_down):
    b, s, d = x_prompt.shape
    bs, ls, _ = x_sample.shape
    tp, ts = b * s, bs * ls
    tm_p, tm_s = 512, ts

    xp = x_prompt.reshape(tp, d)
    xs = x_sample.reshape(ts, d)

    mem_kt_p, mem_vt_p = _memkv(mem_prompt, norm_mem_g, w_mem_kv, mem_k_norm_g)
    mem_kt_s = cache_mem_k.transpose(0, 1, 3, 4, 2).reshape(2, bs, MEM_W, MEM_LEN)
    mem_vt_s = cache_mem_v.transpose(0, 1, 3, 4, 2).reshape(2, bs, MEM_W, MEM_LEN)

    def mem_attend_sample(qm_s, layer):
        qs8 = jnp.pad(qm_s.reshape(bs, ls, MEM_W), ((0, 0), (0, SAMPLE_ROWS - ls), (0, 0)))
        mo_s = _mem_attend(qs8, mem_kt_s, mem_vt_s, layer, mem_q_norm_g[layer], SAMPLE_ROWS, 4)
        return mo_s[:, :ls].reshape(ts, MEM_W)

    g_mix0 = norm_mix_g[0][None, :]
    w_bd = _block_diag(w_pool_group[0])
    pscale = pool_scale[0][None, :]
    xp, up_tail = _pool_layer_prompt(x_prompt, g_mix0, w_in_pool[0], w_bd, pscale, mem_kt_p[0], mem_vt_p[0],
                                     mem_q_norm_g[0], w_out[0], tm_p)
    xp = xp.reshape(tp, d)
    pool_p = up_tail[:, _POOL_HALO - POOL_STATE:][None]
    us, qms = _pool_proj(xs, g_mix0, w_in_pool[0], tm_s)
    us_t = us.reshape(bs, ls, MIX_W).transpose(1, 0, 2)
    mix_s_t, new_state_t = _pool_sample(state_pool[0].transpose(1, 0, 2), us_t, w_bd, pscale)
    mix_s = mix_s_t.transpose(1, 0, 2).reshape(ts, MIX_W)
    pool_s = new_state_t.transpose(1, 0, 2)[None]
    xs = _out_proj(xs, mix_s, mem_attend_sample(qms, 0), w_out, 0, tm_s)
    g_ffn0 = norm_ffn_g[0][None, :]
    xp = _ffn(xp, g_ffn0, w_ffn_gu[0], w_ffn_down[0], tm_p)
    xs = _ffn(xs, g_ffn0, w_ffn_gu[0], w_ffn_down[0], tm_s)

    w_in = w_in_fox[0]
    wq = w_in[:, :MIX_W]
    wk = w_in[:, MIX_W:2 * MIX_W]
    wv = w_in[:, 2 * MIX_W:3 * MIX_W]
    wf = jnp.pad(w_in[:, 3 * MIX_W:3 * MIX_W + FOX_HEADS], ((0, 0), (0, LANES - FOX_HEADS)))
    wqm = w_in[:, 3 * MIX_W + FOX_HEADS:]
    bf_row = jnp.pad(b_forget[0], (0, LANES - FOX_HEADS))[None, :]
    qg_t = jnp.tile(fox_q_norm_g[0], FOX_HEADS)[None, :]
    kg_t = jnp.tile(fox_k_norm_g[0], FOX_HEADS)[None, :]
    g_mix1 = norm_mix_g[1][None, :]
    qpp, kpp, ktp, vtp, vtb, lftp, qmp = _fox_proj_prompt(
        xp, g_mix1, wq.T, wk.T, wv.T, wqm, wf, bf_row, fox_q_norm_g[0][:, None], fox_k_norm_g[0][:, None],
        tm=tm_p, n_seq=b)
    qs, kns, vns, lfts, qms = _fox_proj_sample(xs, g_mix1, wq, wk, wv, wqm, wf, bf_row, qg_t, kg_t)

    mix_pt = _fox_flash(qpp, kpp, vtb, 512)

    pad_rows = ((0, 0), (0, SAMPLE_ROWS - ls), (0, 0))
    q8 = jnp.pad(qs.reshape(bs, ls, MIX_W), pad_rows)
    kn8 = jnp.pad(kns.reshape(bs, ls, MIX_W), pad_rows)
    vn8 = jnp.pad(vns.reshape(bs, ls, MIX_W), pad_rows)
    lfn = jnp.pad(lfts.reshape(_FL_ROWS, bs, ls).transpose(1, 0, 2),
                  ((0, 0), (0, 0), (0, SAMPLE_ROWS - ls)))
    n_phys = cache_fox_k.shape[1]
    kt_pages = cache_fox_k[0].transpose(0, 2, 3, 1).reshape(n_phys, MIX_W, PAGE_SIZE)
    vt_pages = cache_fox_v[0].transpose(0, 2, 3, 1).reshape(n_phys, MIX_W, PAGE_SIZE)
    lf_heads = cache_fox_logf[0].transpose(2, 0, 1)
    mix_s = _fox_decode(page_table, kt_pages, vt_pages, lf_heads, q8, kn8, vn8, lfn, ls)
    mix_s = mix_s[:, :ls].reshape(ts, MIX_W)

    mo_p = _mem_attend(qmp.reshape(b, s, MEM_W), mem_kt_p, mem_vt_p, 1, mem_q_norm_g[1],
                       tm_p, 1).reshape(tp, MEM_W)
    xs = _out_proj(xs, mix_s, mem_attend_sample(qms, 1), w_out, 1, tm_s)

    g_ffn1 = norm_ffn_g[1][None, :]
    wrt = w_router[0].T
    xp, hp, rp, cnt_p = _out_proj_router(xp, mix_pt, mo_p, w_out, 1, g_ffn1, wrt, tm_p)
    hs, rs, cnt_s = _router(xs, g_ffn1, wrt, tm_s)
    cnt_p = cnt_p[:, 0].astype(jnp.int32)
    cnt_s = cnt_s[:, 0].astype(jnp.int32)
    counts = cnt_p + cnt_s
    offsets = jnp.cumsum(counts) - counts
    n_sorted = 2 * (tp + ts)

    def dests(r, base):
        i1, i2 = r[0].astype(jnp.int32), r[1].astype(jnp.int32)
        return base[i1] + r[2].astype(jnp.int32), base[i2] + r[3].astype(jnp.int32)

    d1p, d2p = dests(rp, offsets)
    d1s, d2s = dests(rs, offsets + cnt_p)
    x_sorted = _scatter_rows(d1p, d2p, d1s, d2s, hp, hs, n_sorted)
    o_sorted = _gmm(_gmm_meta(counts, n_sorted), x_sorted, w_exp_gu[0], w_exp_down[0])
    yp = _combine(d1p, d2p, xp, rp[4:6].T, o_sorted, tm_p)
    ys = _combine(d1s, d2s, xs, rs[4:6].T, o_sorted, ts)

    def heads_t(a_t, n_b, n_h):
        return a_t.reshape(n_b, n_h, HEAD_DIM, a_t.shape[-1]).transpose(0, 3, 1, 2)

    mem_k_p = jnp.stack([heads_t(mem_kt_p[i], b, MEM_HEADS) for i in range(2)])
    mem_v_p = jnp.stack([heads_t(mem_vt_p[i], b, MEM_HEADS) for i in range(2)])
    fk_p = heads_t(ktp, b, FOX_HEADS)[None]
    fv_p = heads_t(vtp, b, FOX_HEADS)[None]
    fl_p = lftp[:FOX_HEADS].reshape(FOX_HEADS, b, s).transpose(1, 2, 0)[None]
    fk_s = kns.reshape(1, bs, ls, FOX_HEADS, HEAD_DIM)
    fv_s = vns.reshape(1, bs, ls, FOX_HEADS, HEAD_DIM)
    fl_s = lfts[:FOX_HEADS].reshape(FOX_HEADS, bs, ls).transpose(1, 2, 0)[None]
    return (yp.reshape(b, s, d), ys.reshape(bs, ls, d), mem_k_p, mem_v_p, pool_p, pool_s,
            fk_p, fv_p, fl_p, fk_s, fv_s, fl_s)
```

```python
import functools
import math

import jax
import jax.numpy as jnp
from jax import lax
from jax.experimental import pallas as pl
from jax.experimental.pallas import tpu as pltpu

F32 = jnp.float32
BF16 = jnp.bfloat16

D_MODEL = 1024
HEAD_DIM = 64
MEM_LEN = 256
MEM_HEADS = 4
MEM_W = MEM_HEADS * HEAD_DIM
MIX_W = D_MODEL - MEM_W
FOX_HEADS = MIX_W // HEAD_DIM
POOL_WINDOWS = (2, 4, 8, 16)
POOL_GW = MIX_W // len(POOL_WINDOWS)
POOL_STATE = max(POOL_WINDOWS) - 1
D_FF = 2816
N_EXPERTS = 8
PAGE_SIZE = 128
EPS = 1e-6
SCALE = HEAD_DIM ** -0.5
LOG2E = math.log2(math.e)

LANES = 128
SUBLANES = 8
MXU_N = 256
VMEM_LIMIT = 56 * 1024 * 1024

FF_CHUNK = MXU_N
N_FF_CHUNKS = D_FF // FF_CHUNK
MOE_TM = 1032
PAGES_PER_STEP = 8
SAMPLE_ROWS = 8
DMA_UNROLL = 8
PROMPT_TM = 512
FLASH_BLK = 512
MEM_SEQS_PER_STEP = 4
DECODE_SLOTS = 3
FLASH_HEADS = 12
MOE_FF_SPLIT = 2
MOE_SUB_ROWS = 344


def _cparams(*sem, vmem=None):
    return pltpu.CompilerParams(dimension_semantics=sem, vmem_limit_bytes=vmem)


def _dot(a, b):
    return jnp.dot(a, b, preferred_element_type=F32)


def _dot_nt(a, b):
    return lax.dot_general(a, b, (((1,), (1,)), ((), ())), preferred_element_type=F32)


def _rms_rows(x, g):
    return x * lax.rsqrt(jnp.mean(x * x, axis=-1, keepdims=True) + EPS) * g


def _split3(x):
    hi = x.astype(BF16)
    r = x - hi.astype(F32)
    mid = r.astype(BF16)
    lo = (r - mid.astype(F32)).astype(BF16)
    return hi, mid, lo


def _head_sumsq_lanes(x):
    r = lax.broadcasted_iota(jnp.int32, (LANES, LANES), 0) // HEAD_DIM
    c = lax.broadcasted_iota(jnp.int32, (LANES, LANES), 1) // HEAD_DIM
    ones_bd = (r == c).astype(BF16)
    xx = x * x
    hi = xx.astype(BF16)
    lo = (xx - hi.astype(F32)).astype(BF16)
    parts = []
    for j in range(x.shape[1] // LANES):
        sl = slice(j * LANES, (j + 1) * LANES)
        parts.append(_dot(hi[:, sl], ones_bd) + _dot(lo[:, sl], ones_bd))
    return jnp.concatenate(parts, axis=1)


def _head_rms_lanes(x, g_tiled):
    ssq = _head_sumsq_lanes(x)
    return x * lax.rsqrt(ssq * (1.0 / HEAD_DIM) + EPS) * g_tiled


def _head_rms_rows_t(xt, g_col, n_heads):
    outs = []
    for h in range(n_heads):
        blk = xt[h * HEAD_DIM:(h + 1) * HEAD_DIM, :]
        ms = jnp.mean(blk * blk, axis=0, keepdims=True)
        outs.append(blk * lax.rsqrt(ms + EPS) * g_col)
    return outs


def _log_sigmoid(x):
    return jnp.minimum(x, 0.0) - jnp.log1p(jnp.exp(-jnp.abs(x)))


def _lane_head(shape, axis):
    return lax.broadcasted_iota(jnp.int32, shape, axis) // HEAD_DIM


def _memkv_body(mem_ref, gm_ref, w_ref, kg_ref, kt_ref, vt_ref):
    h = _rms_rows(mem_ref[0], gm_ref[0])
    z = _dot(h, w_ref[0])
    kt = z[:, :MEM_W].T
    vt_ref[0, 0] = z[:, MEM_W:].T
    pieces = _head_rms_rows_t(kt, kg_ref[0], MEM_HEADS)
    for h_i, p in enumerate(pieces):
        kt_ref[0, 0, h_i * HEAD_DIM:(h_i + 1) * HEAD_DIM, :] = p


def _memkv(mem, g_mem, w_kv, kn_g):
    depth, batch = w_kv.shape[0], mem.shape[0]
    out = jax.ShapeDtypeStruct((depth, batch, MEM_W, MEM_LEN), F32)
    return pl.pallas_call(
        _memkv_body,
        grid=(depth, batch),
        in_specs=[
            pl.BlockSpec((1, MEM_LEN, D_MODEL), lambda i, b: (b, 0, 0)),
            pl.BlockSpec((1, 1, D_MODEL), lambda i, b: (i, 0, 0)),
            pl.BlockSpec((1, D_MODEL, 2 * MEM_W), lambda i, b: (i, 0, 0)),
            pl.BlockSpec((1, HEAD_DIM, 1), lambda i, b: (i, 0, 0)),
        ],
        out_specs=[pl.BlockSpec((1, 1, MEM_W, MEM_LEN), lambda i, b: (i, b, 0, 0))] * 2,
        out_shape=[out, out],
        compiler_params=_cparams("arbitrary", "arbitrary"),
        name="memkv",
    )(mem, g_mem[:, None, :], w_kv, kn_g[:, :, None])


def _pool_proj_body(x_ref, g_ref, w_ref, u_ref, qm_ref):
    h = _rms_rows(x_ref[...], g_ref[...])
    u_ref[...] = _dot(h, w_ref[:, :MIX_W])
    qm_ref[...] = _dot(h, w_ref[:, MIX_W:])


def _pool_proj(x, g, w, tm):
    t = x.shape[0]
    return pl.pallas_call(
        _pool_proj_body,
        grid=(t // tm,),
        in_specs=[
            pl.BlockSpec((tm, D_MODEL), lambda i: (i, 0)),
            pl.BlockSpec((1, D_MODEL), lambda i: (0, 0)),
            pl.BlockSpec((D_MODEL, D_MODEL), lambda i: (0, 0)),
        ],
        out_specs=[pl.BlockSpec((tm, MIX_W), lambda i: (i, 0)),
                   pl.BlockSpec((tm, MEM_W), lambda i: (i, 0))],
        out_shape=[jax.ShapeDtypeStruct((t, MIX_W), F32), jax.ShapeDtypeStruct((t, MEM_W), F32)],
        compiler_params=_cparams("arbitrary", vmem=VMEM_LIMIT),
        name="pool_proj",
    )(x, g, w)


def _window_of_lane(shape):
    lane = lax.broadcasted_iota(jnp.int32, shape, len(shape) - 1)
    return jnp.where(lane < POOL_GW, POOL_WINDOWS[0],
                     jnp.where(lane < 2 * POOL_GW, POOL_WINDOWS[1],
                               jnp.where(lane < 3 * POOL_GW, POOL_WINDOWS[2], POOL_WINDOWS[3])))


def _pool_select(s2, s4, s8, s16, shape):
    lane = lax.broadcasted_iota(jnp.int32, shape, len(shape) - 1)
    return jnp.where(lane < POOL_GW, s2,
                     jnp.where(lane < 2 * POOL_GW, s4, jnp.where(lane < 3 * POOL_GW, s8, s16)))


_POOL_PAD = SUBLANES
_POOL_HALO = 2 * SUBLANES
_POOL_BASE = _POOL_PAD + _POOL_HALO


def _pool_mix_tile(u, li, tl, e_ref, s2_ref, s4_ref, s8_ref, w_bd, scale):
    n = _POOL_HALO + tl

    @pl.when(li == 0)
    def _():
        e_ref[0:_POOL_BASE, :] = jnp.zeros((_POOL_BASE, MIX_W), F32)
        s2_ref[0:_POOL_PAD, :] = jnp.zeros((_POOL_PAD, MIX_W), F32)
        s4_ref[0:_POOL_PAD, :] = jnp.zeros((_POOL_PAD, MIX_W), F32)
        s8_ref[0:_POOL_PAD, :] = jnp.zeros((_POOL_PAD, MIX_W), F32)

    e_ref[_POOL_BASE:_POOL_BASE + tl, :] = u
    s2 = e_ref[_POOL_PAD:_POOL_PAD + n, :] + e_ref[_POOL_PAD - 1:_POOL_PAD - 1 + n, :]
    s2_ref[_POOL_PAD:_POOL_PAD + n, :] = s2
    s4 = s2 + s2_ref[_POOL_PAD - 2:_POOL_PAD - 2 + n, :]
    s4_ref[_POOL_PAD:_POOL_PAD + n, :] = s4
    s8 = s4 + s4_ref[_POOL_PAD - 4:_POOL_PAD - 4 + n, :]
    s8_ref[_POOL_PAD:_POOL_PAD + n, :] = s8
    s16 = s8[_POOL_HALO:, :] + s8_ref[_POOL_BASE - 8:_POOL_BASE - 8 + tl, :]
    shape = (tl, MIX_W)
    ssel = _pool_select(s2[_POOL_HALO:, :], s4[_POOL_HALO:, :], s8[_POOL_HALO:, :], s16, shape)
    pos = li * tl + lax.broadcasted_iota(jnp.int32, shape, 0)
    cnt = jnp.minimum(_window_of_lane(shape), pos + 1).astype(F32)
    diff = ssel / cnt - u
    mix = _dot(diff, w_bd) * scale
    e_ref[_POOL_PAD:_POOL_BASE, :] = e_ref[_POOL_PAD + tl:_POOL_BASE + tl, :]
    return mix


def _pool_layer_prompt_body(x_ref, g_ref, win_ref, wbd_ref, sc_ref, kt_ref, vt_ref, qg_ref, wo_ref,
                            o_ref, st_ref, e_ref, s2_ref, s4_ref, s8_ref, *, tl):
    x = x_ref[0]
    h = _rms_rows(x, g_ref[...])
    u = _dot(h, win_ref[:, :MIX_W])
    qm = _dot(h, win_ref[:, MIX_W:])
    st_ref[0] = u[tl - _POOL_HALO:, :]
    mix = _pool_mix_tile(u, pl.program_id(1), tl, e_ref, s2_ref, s4_ref, s8_ref, wbd_ref[...], sc_ref[...])
    mem = _mem_attend_tile(qm, kt_ref[0], vt_ref[0], qg_ref[...])
    o_ref[0] = x + _dot(mix, wo_ref[:MIX_W, :]) + _dot(mem, wo_ref[MIX_W:, :])


def _pool_layer_prompt(x, g, w_in, w_bd, scale, mem_kt, mem_vt, qn_g, w_out, tl):
    b, s, _ = x.shape
    rows = _POOL_BASE + tl
    const = lambda shape: pl.BlockSpec(shape, lambda bi, li: tuple(0 for _ in shape))
    per_b = lambda shape: pl.BlockSpec(shape, lambda bi, li: (bi, 0, 0))
    return pl.pallas_call(
        functools.partial(_pool_layer_prompt_body, tl=tl),
        grid=(b, s // tl),
        in_specs=[
            pl.BlockSpec((1, tl, D_MODEL), lambda bi, li: (bi, li, 0)),
            const((1, D_MODEL)), const((D_MODEL, D_MODEL)), const((MIX_W, MIX_W)), const((1, MIX_W)),
            per_b((1, MEM_W, MEM_LEN)), per_b((1, MEM_W, MEM_LEN)), const((1, MEM_W)),
            const((D_MODEL, D_MODEL)),
        ],
        out_specs=[pl.BlockSpec((1, tl, D_MODEL), lambda bi, li: (bi, li, 0)),
                   per_b((1, _POOL_HALO, MIX_W))],
        out_shape=[jax.ShapeDtypeStruct((b, s, D_MODEL), F32),
                   jax.ShapeDtypeStruct((b, _POOL_HALO, MIX_W), F32)],
        scratch_shapes=[pltpu.VMEM((rows, MIX_W), F32)] * 4,
        compiler_params=_cparams("arbitrary", "arbitrary", vmem=VMEM_LIMIT),
        name="pool_layer_prompt",
    )(x, g, w_in, w_bd, scale, mem_kt, mem_vt, jnp.tile(qn_g, MEM_HEADS)[None, :], w_out)


def _pool_sample_body(st_ref, u_ref, w_ref, sc_ref, o_ref, ns_ref, *, n_new):
    def ext(j):
        return st_ref[j] if j < POOL_STATE else u_ref[j - POOL_STATE]

    for l in range(n_new):
        r = POOL_STATE + l
        s2 = ext(r) + ext(r - 1)
        s4 = s2 + ext(r - 2) + ext(r - 3)
        s8 = s4
        for j in range(4, 8):
            s8 = s8 + ext(r - j)
        s16 = s8
        for j in range(8, 16):
            s16 = s16 + ext(r - j)
        shape = s2.shape
        cnt = _window_of_lane(shape).astype(F32)
        diff = _pool_select(s2, s4, s8, s16, shape) / cnt - ext(r)
        o_ref[l] = _dot(diff, w_ref[...]) * sc_ref[...]
    for j in range(POOL_STATE):
        ns_ref[j] = ext(j + n_new)


def _pool_sample(state_t, u_t, w_bd, scale):
    n_new, b, _ = u_t.shape
    return pl.pallas_call(
        functools.partial(_pool_sample_body, n_new=n_new),
        out_shape=[jax.ShapeDtypeStruct((n_new, b, MIX_W), F32),
                   jax.ShapeDtypeStruct((POOL_STATE, b, MIX_W), F32)],
        name="pool_sample",
    )(state_t, u_t, w_bd, scale)


def _mem_attend_tile(qm, kt, vt, g_tiled):
    q = _head_rms_lanes(qm, g_tiled) * SCALE
    lh = _lane_head(q.shape, 1)
    out = jnp.zeros(q.shape, F32)
    for h in range(MEM_HEADS):
        s = _dot(jnp.where(lh == h, q, 0.0), kt)
        e = jnp.exp(s - jnp.max(s, axis=-1, keepdims=True))
        p = e / jnp.sum(e, axis=-1, keepdims=True)
        out = jnp.where(lh == h, _dot_nt(p, vt), out)
    return out


def _mem_attend_body(q_ref, kt_ref, vt_ref, g_ref, o_ref):
    for i in range(q_ref.shape[0]):
        o_ref[i] = _mem_attend_tile(q_ref[i], kt_ref[0, i], vt_ref[0, i], g_ref[...])


def _mem_attend(qm, kt_layers, vt_layers, layer, qn_g, tl, bb):
    b, l, _ = qm.shape
    kv_spec = pl.BlockSpec((1, bb, MEM_W, MEM_LEN), lambda bi, li: (layer, bi, 0, 0))
    return pl.pallas_call(
        _mem_attend_body,
        grid=(b // bb, l // tl),
        in_specs=[
            pl.BlockSpec((bb, tl, MEM_W), lambda bi, li: (bi, li, 0)),
            kv_spec, kv_spec,
            pl.BlockSpec((1, MEM_W), lambda bi, li: (0, 0)),
        ],
        out_specs=pl.BlockSpec((bb, tl, MEM_W), lambda bi, li: (bi, li, 0)),
        out_shape=jax.ShapeDtypeStruct((b, l, MEM_W), F32),
        compiler_params=_cparams("arbitrary", "arbitrary"),
        name="mem_attend",
    )(qm, kt_layers, vt_layers, jnp.tile(qn_g, MEM_HEADS)[None, :])


def _out_proj_body(x_ref, mix_ref, mem_ref, w_ref, o_ref):
    o_ref[...] = (x_ref[...] + _dot(mix_ref[...], w_ref[0, :MIX_W, :])
                  + _dot(mem_ref[...], w_ref[0, MIX_W:, :]))


def _out_proj(x, mix, mem, w_layers, layer, tm):
    t = x.shape[0]
    return pl.pallas_call(
        _out_proj_body,
        grid=(t // tm,),
        in_specs=[
            pl.BlockSpec((tm, D_MODEL), lambda i: (i, 0)),
            pl.BlockSpec((tm, MIX_W), lambda i: (i, 0)),
            pl.BlockSpec((tm, MEM_W), lambda i: (i, 0)),
            pl.BlockSpec((1, D_MODEL, D_MODEL), lambda i: (layer, 0, 0)),
        ],
        out_specs=pl.BlockSpec((tm, D_MODEL), lambda i: (i, 0)),
        out_shape=jax.ShapeDtypeStruct((t, D_MODEL), F32),
        compiler_params=_cparams("arbitrary", vmem=VMEM_LIMIT),
        name="out_proj",
    )(x, mix, mem, w_layers)


def _silu(g):
    return g / (1.0 + jnp.exp(-g))


def _ffn_body(x_ref, g_ref, wgu_ref, wd_ref, o_ref):
    x = x_ref[...]
    h = _rms_rows(x, g_ref[...])
    o_ref[...] = x
    for c in range(N_FF_CHUNKS):
        gate = slice(c * FF_CHUNK, (c + 1) * FF_CHUNK)
        up = slice(D_FF + c * FF_CHUNK, D_FF + (c + 1) * FF_CHUNK)
        a = _silu(_dot(h, wgu_ref[:, gate])) * _dot(h, wgu_ref[:, up])
        o_ref[...] += _dot(a, wd_ref[gate, :])


def _ffn(x, g, w_gu, w_down, tm):
    t = x.shape[0]
    resident = lambda shape: pl.BlockSpec(shape, lambda i: (0, 0), pipeline_mode=pl.Buffered(1))
    return pl.pallas_call(
        _ffn_body,
        grid=(t // tm,),
        in_specs=[
            pl.BlockSpec((tm, D_MODEL), lambda i: (i, 0)),
            pl.BlockSpec((1, D_MODEL), lambda i: (0, 0)),
            resident((D_MODEL, 2 * D_FF)),
            resident((D_FF, D_MODEL)),
        ],
        out_specs=pl.BlockSpec((tm, D_MODEL), lambda i: (i, 0)),
        out_shape=jax.ShapeDtypeStruct((t, D_MODEL), F32),
        compiler_params=_cparams("arbitrary", vmem=VMEM_LIMIT),
        name="ffn",
    )(x, g, w_gu, w_down)


_FL_ROWS = 2 * SUBLANES
_AUG0 = HEAD_DIM


def _aug_pieces(c_col):
    hi, mid, lo = _split3(c_col)
    bc = lambda a: jnp.broadcast_to(a.astype(F32), (c_col.shape[0], LANES))
    return bc(hi), bc(mid), bc(lo)


def _fox_proj_prompt_body(x_ref, g_ref, wq_ref, wk_ref, wv_ref, wqm_ref, wf_ref, bf_ref, qg_ref, kg_ref,
                          qp_ref, kp_ref, kt_ref, vt_ref, vtb_ref, lft_ref, qm_ref, carry_ref, *, tm, per_seq):
    @pl.when(pl.program_id(0) % per_seq == 0)
    def _():
        carry_ref[...] = jnp.zeros(carry_ref.shape, F32)

    h = _rms_rows(x_ref[...], g_ref[...])
    q_pieces = _head_rms_rows_t(_dot_nt(wq_ref[...], h), qg_ref[...], FOX_HEADS)
    k_pieces = _head_rms_rows_t(_dot_nt(wk_ref[...], h), kg_ref[...], FOX_HEADS)
    for h_i, piece in enumerate(k_pieces):
        kt_ref[0, h_i * HEAD_DIM:(h_i + 1) * HEAD_DIM, :] = piece
    vt = _dot_nt(wv_ref[...], h)
    vt_ref[0] = vt
    vtb_ref[0] = vt.astype(BF16)
    q = jnp.concatenate(q_pieces, axis=0).T * (SCALE * LOG2E)
    k = jnp.concatenate(k_pieces, axis=0).T
    qm_ref[...] = _dot(h, wqm_ref[...])
    lane = lax.broadcasted_iota(jnp.int32, (tm, LANES), 1)
    lf = jnp.where(lane < FOX_HEADS, _log_sigmoid(_dot(h, wf_ref[...]) + bf_ref[...]), 0.0)
    lft_ref[...] = lf.T[:_FL_ROWS, :]
    r = lax.broadcasted_iota(jnp.int32, (tm, tm), 0)
    c = lax.broadcasted_iota(jnp.int32, (tm, tm), 1)
    lower = (c <= r).astype(BF16)
    hi, mid, lo = _split3(lf)
    csum = _dot(lower, hi) + _dot(lower, mid) + _dot(lower, lo) + carry_ref[0:1, :]
    carry_ref[...] = jnp.broadcast_to(csum[tm - 1:tm, :], carry_ref.shape)
    c2 = csum * LOG2E
    one = jnp.ones((tm, LANES), F32)
    zero = jnp.zeros((tm, LANES), F32)
    for hd in range(FOX_HEADS):
        chi, cmid, clo = _aug_pieces(c2[:, hd:hd + 1])
        aug_q = jnp.where(lane == _AUG0, chi, jnp.where(lane == _AUG0 + 1, cmid, jnp.where(
            lane == _AUG0 + 2, clo, jnp.where(lane < _AUG0 + 6, one, zero))))
        aug_k = jnp.where(lane < _AUG0 + 3, one, jnp.where(lane == _AUG0 + 3, -chi, jnp.where(
            lane == _AUG0 + 4, -cmid, jnp.where(lane == _AUG0 + 5, -clo, zero))))
        col = slice((hd // 2) * LANES, (hd // 2 + 1) * LANES)
        qc, kc = q[:, col], k[:, col]
        if hd % 2:
            qc = pltpu.roll(qc, HEAD_DIM, 1)
            kc = pltpu.roll(kc, HEAD_DIM, 1)
        qp_ref[0, hd] = jnp.where(lane < HEAD_DIM, qc, aug_q).astype(BF16)
        kp_ref[0, hd] = jnp.where(lane < HEAD_DIM, kc, aug_k).astype(BF16)


def _fox_proj_prompt(x, g, wq_t, wk_t, wv_t, wqm, wf, bf_row, qg_col, kg_col, tm, n_seq):
    t = x.shape[0]
    seq = t // n_seq
    per_seq = seq // tm

    def const(shape):
        return pl.BlockSpec(shape, lambda i: tuple(0 for _ in shape))

    heads_spec = pl.BlockSpec((1, FOX_HEADS, tm, LANES), lambda i: (i // per_seq, 0, i % per_seq, 0))
    t_spec = pl.BlockSpec((1, MIX_W, tm), lambda i: (i // per_seq, 0, i % per_seq))
    return pl.pallas_call(
        functools.partial(_fox_proj_prompt_body, tm=tm, per_seq=per_seq),
        grid=(t // tm,),
        in_specs=[
            pl.BlockSpec((tm, D_MODEL), lambda i: (i, 0)),
            const((1, D_MODEL)),
            const((MIX_W, D_MODEL)), const((MIX_W, D_MODEL)), const((MIX_W, D_MODEL)),
            const((D_MODEL, MEM_W)), const((D_MODEL, LANES)),
            const((1, LANES)), const((HEAD_DIM, 1)), const((HEAD_DIM, 1)),
        ],
        out_specs=[heads_spec, heads_spec, t_spec, t_spec, t_spec,
                   pl.BlockSpec((_FL_ROWS, tm), lambda i: (0, i)),
                   pl.BlockSpec((tm, MEM_W), lambda i: (i, 0))],
        out_shape=[
            jax.ShapeDtypeStruct((n_seq, FOX_HEADS, seq, LANES), BF16),
            jax.ShapeDtypeStruct((n_seq, FOX_HEADS, seq, LANES), BF16),
            jax.ShapeDtypeStruct((n_seq, MIX_W, seq), F32),
            jax.ShapeDtypeStruct((n_seq, MIX_W, seq), F32),
            jax.ShapeDtypeStruct((n_seq, MIX_W, seq), BF16),
            jax.ShapeDtypeStruct((_FL_ROWS, t), F32),
            jax.ShapeDtypeStruct((t, MEM_W), F32),
        ],
        scratch_shapes=[pltpu.VMEM((SUBLANES, LANES), F32)],
        compiler_params=_cparams("arbitrary", vmem=VMEM_LIMIT),
        name="fox_proj_prompt",
    )(x, g, wq_t, wk_t, wv_t, wqm, wf, bf_row, qg_col, kg_col)


def _fox_proj_sample_body(x_ref, g_ref, wq_ref, wk_ref, wv_ref, wqm_ref, wf_ref, bf_ref, qg_ref, kg_ref,
                          q_ref, k_ref, v_ref, lft_ref, qm_ref):
    h = _rms_rows(x_ref[...], g_ref[...])
    q_ref[...] = _head_rms_lanes(_dot(h, wq_ref[...]), qg_ref[...])
    k_ref[...] = _head_rms_lanes(_dot(h, wk_ref[...]), kg_ref[...])
    v_ref[...] = _dot(h, wv_ref[...])
    qm_ref[...] = _dot(h, wqm_ref[...])
    lane = lax.broadcasted_iota(jnp.int32, (x_ref.shape[0], LANES), 1)
    lf = jnp.where(lane < FOX_HEADS, _log_sigmoid(_dot(h, wf_ref[...]) + bf_ref[...]), 0.0)
    lft_ref[...] = lf.T[:_FL_ROWS, :]


def _fox_proj_sample(x, g, wq, wk, wv, wqm, wf, bf_row, qg_t, kg_t):
    t = x.shape[0]
    rows = jax.ShapeDtypeStruct((t, MIX_W), F32)
    return pl.pallas_call(
        _fox_proj_sample_body,
        out_shape=[rows, rows, rows, jax.ShapeDtypeStruct((_FL_ROWS, t), F32),
                   jax.ShapeDtypeStruct((t, MEM_W), F32)],
        compiler_params=_cparams(vmem=VMEM_LIMIT),
        name="fox_proj_sample",
    )(x, g, wq, wk, wv, wqm, wf, bf_row, qg_t, kg_t)


def _fox_flash_body(qi_ref, ki_ref, qp_ref, kp_ref, vt_ref, o_ref, m_ref, l_ref, acc_ref, *, blk):
    p = pl.program_id(2)
    qi = qi_ref[p]
    ki = ki_ref[p]

    @pl.when(ki == 0)
    def _():
        m_ref[...] = jnp.full(m_ref.shape, -jnp.inf, F32)
        l_ref[...] = jnp.zeros(l_ref.shape, F32)
        acc_ref[...] = jnp.zeros(acc_ref.shape, F32)

    def step(diagonal):
        for hh in range(FLASH_HEADS):
            pair = hh // 2
            vt = vt_ref[0, pair * LANES:(pair + 1) * LANES, :]
            st = _dot_nt(kp_ref[0, hh], qp_ref[0, hh])
            if diagonal:
                key = lax.broadcasted_iota(jnp.int32, st.shape, 0)
                qry = lax.broadcasted_iota(jnp.int32, st.shape, 1)
                st = jnp.where(key <= qry, st, -jnp.inf)
            m_prev = m_ref[hh]
            m_new = jnp.maximum(m_prev, jnp.max(st, axis=0, keepdims=True))
            alpha = jnp.exp2(m_prev - m_new)
            pt = jnp.exp2(st - m_new)
            l_ref[hh] = alpha * l_ref[hh] + jnp.sum(pt, axis=0, keepdims=True)
            acc_ref[hh] = alpha * acc_ref[hh] + _dot(vt, pt.astype(BF16))
            m_ref[hh] = m_new

    @pl.when(ki < qi)
    def _():
        step(False)

    @pl.when(ki == qi)
    def _():
        step(True)
        row = lax.broadcasted_iota(jnp.int32, (LANES, blk), 0)
        for pair in range(FLASH_HEADS // 2):
            a, c = 2 * pair, 2 * pair + 1
            o_ref[0, pair * LANES:(pair + 1) * LANES, :] = jnp.where(
                row < HEAD_DIM, acc_ref[a] / l_ref[a], acc_ref[c] / l_ref[c])


def _fox_flash(qp, kp, vt, blk):
    b, _, s, _ = qp.shape
    n = s // blk
    pairs = [(qi, ki) for qi in range(n) for ki in range(qi + 1)]
    qi_tab = jnp.asarray([p[0] for p in pairs], jnp.int32)
    ki_tab = jnp.asarray([p[1] for p in pairs], jnp.int32)
    fh = FLASH_HEADS
    ch = fh // 2 * LANES
    grid_spec = pltpu.PrefetchScalarGridSpec(
        num_scalar_prefetch=2,
        grid=(b, FOX_HEADS // fh, len(pairs)),
        in_specs=[
            pl.BlockSpec((1, fh, blk, LANES), lambda bi, hg, p, qt, kt: (bi, hg, qt[p], 0)),
            pl.BlockSpec((1, fh, blk, LANES), lambda bi, hg, p, qt, kt: (bi, hg, kt[p], 0)),
            pl.BlockSpec((1, ch, blk), lambda bi, hg, p, qt, kt: (bi, hg, kt[p])),
        ],
        out_specs=pl.BlockSpec((1, ch, blk), lambda bi, hg, p, qt, kt: (bi, hg, qt[p])),
        scratch_shapes=[pltpu.VMEM((fh, 1, blk), F32), pltpu.VMEM((fh, 1, blk), F32),
                        pltpu.VMEM((fh, LANES, blk), F32)],
    )
    return pl.pallas_call(
        functools.partial(_fox_flash_body, blk=blk),
        grid_spec=grid_spec,
        out_shape=jax.ShapeDtypeStruct((b, MIX_W, s), F32),
        compiler_params=_cparams("arbitrary", "arbitrary", "arbitrary", vmem=VMEM_LIMIT),
        name="fox_flash",
    )(qi_tab, ki_tab, qp, kp, vt)


_QROWS = FOX_HEADS * SAMPLE_ROWS
_CHUNK = PAGES_PER_STEP * PAGE_SIZE


def _expand_heads(x):
    n = x.shape[1]
    return jnp.broadcast_to(x[:FOX_HEADS, None, :], (FOX_HEADS, SAMPLE_ROWS, n)).reshape(_QROWS, n)


def _page_copies(pt_ref, kt_hbm, vt_hbm, lf_hbm, kt_buf, vt_buf, lf_buf, sem, g, slot, steps, n_pages):
    bi = g // steps
    first = n_pages - (g % steps + 1) * PAGES_PER_STEP
    copies = []
    for i in range(PAGES_PER_STEP):
        page = pt_ref[bi, first + i]
        lanes = pl.ds(i * PAGE_SIZE, PAGE_SIZE)
        copies.append(pltpu.make_async_copy(kt_hbm.at[page], kt_buf.at[slot, :, lanes], sem.at[slot, 0]))
        copies.append(pltpu.make_async_copy(vt_hbm.at[page], vt_buf.at[slot, :, lanes], sem.at[slot, 1]))
        copies.append(pltpu.make_async_copy(lf_hbm.at[:, page], lf_buf.at[slot, i, pl.ds(0, FOX_HEADS)],
                                            sem.at[slot, 2]))
    return copies


def _fox_decode_body(pt_ref, kt_hbm, vt_hbm, lf_hbm, q_ref, kn_ref, vn_ref, lfn_ref, o_ref,
                     kt_buf, vt_buf, lf_buf, sem, qbd_ref, m_ref, l_ref, acc_ref, carry_ref, crow_ref,
                     *, n_new, steps, n_pages):
    g = pl.program_id(0)
    n_steps = pl.num_programs(0)
    j = g % steps
    slot = g % DECODE_SLOTS
    copies = functools.partial(_page_copies, pt_ref, kt_hbm, vt_hbm, lf_hbm, kt_buf, vt_buf, lf_buf, sem,
                               steps=steps, n_pages=n_pages)
    row_l = lax.broadcasted_iota(jnp.int32, (_QROWS, 1), 0) % SAMPLE_ROWS

    @pl.when(g == 0)
    def _():
        lf_buf[...] = jnp.zeros(lf_buf.shape, F32)
        for ahead in range(DECODE_SLOTS - 1):
            for cp in copies(g=ahead, slot=ahead):
                cp.start()

    @pl.when(g + DECODE_SLOTS - 1 < n_steps)
    def _():
        for cp in copies(g=g + DECODE_SLOTS - 1, slot=(g + DECODE_SLOTS - 1) % DECODE_SLOTS):
            cp.start()

    @pl.when(j == 0)
    def _():
        q = q_ref[0] * SCALE
        lh = _lane_head(q.shape, 1)
        for h in range(FOX_HEADS):
            qbd_ref[h * SAMPLE_ROWS:(h + 1) * SAMPLE_ROWS, :] = jnp.where(lh == h, q, 0.0)
        lfn = lfn_ref[0]
        lane = lax.broadcasted_iota(jnp.int32, lfn.shape, 1)
        c = jnp.zeros(lfn.shape, F32)
        for m in range(n_new):
            cm = jnp.sum(jnp.where(lane <= m, lfn, 0.0), axis=1, keepdims=True)
            c = jnp.where(lane == m, cm, c)
        c_q = _expand_heads(c)
        col = lax.broadcasted_iota(jnp.int32, c_q.shape, 1)
        crow = jnp.sum(jnp.where(col == row_l, c_q, 0.0), axis=1, keepdims=True)
        crow_ref[...] = crow
        s = _dot_nt(qbd_ref[...], kn_ref[0]) + (crow - c_q)
        valid = (col < n_new) & ((col <= row_l) | (row_l >= n_new))
        s = jnp.where(valid, s, -jnp.inf)
        m0 = jnp.max(s, axis=-1, keepdims=True)
        p = jnp.exp(s - m0)
        m_ref[...] = m0
        l_ref[...] = jnp.sum(p, axis=-1, keepdims=True)
        acc_ref[...] = _dot(p, vn_ref[0])
        carry_ref[...] = jnp.zeros(carry_ref.shape, F32)

    for cp in copies(g=g, slot=slot):
        cp.wait()

    r = lax.broadcasted_iota(jnp.int32, (PAGE_SIZE, PAGE_SIZE), 0)
    cc = lax.broadcasted_iota(jnp.int32, (PAGE_SIZE, PAGE_SIZE), 1)
    later = (r > cc).astype(BF16)
    lf3 = lf_buf[slot]
    tot = jnp.sum(lf3, axis=2, keepdims=True)
    hi, mid, lo = _split3(lf3.reshape(PAGES_PER_STEP * _FL_ROWS, PAGE_SIZE))
    d_in = (_dot(hi, later) + _dot(mid, later) + _dot(lo, later)).reshape(PAGES_PER_STEP, _FL_ROWS, PAGE_SIZE)
    after = carry_ref[:, 0:1]
    d_pages = [None] * PAGES_PER_STEP
    for i in reversed(range(PAGES_PER_STEP)):
        d_pages[i] = _expand_heads(d_in[i] + after)
        after = after + tot[i]
    carry_ref[...] = jnp.broadcast_to(after, carry_ref.shape)
    d = jnp.concatenate(d_pages, axis=1)

    vt = vt_buf[slot]
    s = _dot(qbd_ref[...], kt_buf[slot]) + crow_ref[...] + d
    m_prev = m_ref[...]
    m_new = jnp.maximum(m_prev, jnp.max(s, axis=-1, keepdims=True))
    alpha = jnp.exp(m_prev - m_new)
    p = jnp.exp(s - m_new)
    l_ref[...] = alpha * l_ref[...] + jnp.sum(p, axis=-1, keepdims=True)
    acc_ref[...] = alpha * acc_ref[...] + _dot_nt(p, vt)
    m_ref[...] = m_new

    @pl.when(j == steps - 1)
    def _():
        res = acc_ref[...] / l_ref[...]
        lh = _lane_head((SAMPLE_ROWS, MIX_W), 1)
        out = jnp.zeros((SAMPLE_ROWS, MIX_W), F32)
        for h in range(FOX_HEADS):
            out = jnp.where(lh == h, res[h * SAMPLE_ROWS:(h + 1) * SAMPLE_ROWS, :], out)
        o_ref[0] = out


def _fox_decode(page_table, kt_pages, vt_pages, lf_heads, q8, kn8, vn8, lfn, n_new):
    b, n_pages = page_table.shape
    steps = n_pages // PAGES_PER_STEP
    per_b = lambda g, pt: (g // steps, 0, 0)
    any_spec = pl.BlockSpec(memory_space=pl.ANY)
    grid_spec = pltpu.PrefetchScalarGridSpec(
        num_scalar_prefetch=1,
        grid=(b * steps,),
        in_specs=[any_spec, any_spec, any_spec,
                  pl.BlockSpec((1, SAMPLE_ROWS, MIX_W), per_b),
                  pl.BlockSpec((1, SAMPLE_ROWS, MIX_W), per_b),
                  pl.BlockSpec((1, SAMPLE_ROWS, MIX_W), per_b),
                  pl.BlockSpec((1, _FL_ROWS, SAMPLE_ROWS), per_b)],
        out_specs=pl.BlockSpec((1, SAMPLE_ROWS, MIX_W), per_b),
        scratch_shapes=[
            pltpu.VMEM((DECODE_SLOTS, MIX_W, _CHUNK), F32),
            pltpu.VMEM((DECODE_SLOTS, MIX_W, _CHUNK), F32),
            pltpu.VMEM((DECODE_SLOTS, PAGES_PER_STEP, _FL_ROWS, PAGE_SIZE), F32),
            pltpu.SemaphoreType.DMA((DECODE_SLOTS, 3)),
            pltpu.VMEM((_QROWS, MIX_W), F32),
            pltpu.VMEM((_QROWS, 1), F32), pltpu.VMEM((_QROWS, 1), F32),
            pltpu.VMEM((_QROWS, MIX_W), F32),
            pltpu.VMEM((_FL_ROWS, LANES), F32),
            pltpu.VMEM((_QROWS, 1), F32),
        ],
    )
    return pl.pallas_call(
        functools.partial(_fox_decode_body, n_new=n_new, steps=steps, n_pages=n_pages),
        grid_spec=grid_spec,
        out_shape=jax.ShapeDtypeStruct((b, SAMPLE_ROWS, MIX_W), F32),
        compiler_params=_cparams("arbitrary", vmem=VMEM_LIMIT),
        name="fox_decode",
    )(page_table, kt_pages, vt_pages, lf_heads, q8, kn8, vn8, lfn)


def _router_body(x_ref, g_ref, wrt_ref, h_ref, r_ref, cnt_ref, carry_ref, *, tm):
    _route_tile(x_ref[...], g_ref, wrt_ref, h_ref, r_ref, cnt_ref, carry_ref, tm)


def _out_proj_router_body(x_ref, mixt_ref, mem_ref, wo_ref, g_ref, wrt_ref,
                          x1_ref, h_ref, r_ref, cnt_ref, carry_ref, *, tm):
    x1 = (x_ref[...] + _dot(mixt_ref[0].T, wo_ref[0, :MIX_W, :])
          + _dot(mem_ref[...], wo_ref[0, MIX_W:, :]))
    x1_ref[...] = x1
    _route_tile(x1, g_ref, wrt_ref, h_ref, r_ref, cnt_ref, carry_ref, tm)


def _route_tile(x, g_ref, wrt_ref, h_ref, r_ref, cnt_ref, carry_ref, tm):
    @pl.when(pl.program_id(0) == 0)
    def _():
        carry_ref[...] = jnp.zeros(carry_ref.shape, F32)

    h = _rms_rows(x, g_ref[...])
    h_ref[...] = h
    h_hi = h.astype(BF16)
    h_lo = (h - h_hi.astype(F32)).astype(BF16)
    w = wrt_ref[...]
    w_hi = w.astype(BF16)
    w_lo = (w - w_hi.astype(F32)).astype(BF16)
    lg = _dot_nt(w_hi, h_hi) + _dot_nt(w_hi, h_lo) + _dot_nt(w_lo, h_hi)
    idx = lax.broadcasted_iota(jnp.int32, lg.shape, 0)
    m1 = jnp.max(lg, axis=0, keepdims=True)
    i1 = jnp.min(jnp.where(lg == m1, idx, N_EXPERTS), axis=0, keepdims=True)
    sel1 = idx == i1
    lg2 = jnp.where(sel1, -jnp.inf, lg)
    m2 = jnp.max(lg2, axis=0, keepdims=True)
    i2 = jnp.min(jnp.where(lg2 == m2, idx, N_EXPERTS), axis=0, keepdims=True)
    sel2 = idx == i2
    e = jnp.exp(m2 - m1)
    g1 = 1.0 / (1.0 + e)
    g2 = e / (1.0 + e)
    assign = jnp.where(sel1 | sel2, 1.0, 0.0)
    r = lax.broadcasted_iota(jnp.int32, (tm, tm), 0)
    c = lax.broadcasted_iota(jnp.int32, (tm, tm), 1)
    before = (r < c).astype(BF16)
    rank = _dot(assign.astype(BF16), before) + carry_ref[:, 0:1]
    r1 = jnp.sum(jnp.where(sel1, rank, 0.0), axis=0, keepdims=True)
    r2 = jnp.sum(jnp.where(sel2, rank, 0.0), axis=0, keepdims=True)
    carry = carry_ref[...] + jnp.sum(assign, axis=1, keepdims=True)
    carry_ref[...] = carry
    cnt_ref[...] = carry
    rows = [i1.astype(F32), i2.astype(F32), r1, r2, g1, g2]
    out = jnp.zeros(lg.shape, F32)
    for k, v in enumerate(rows):
        out = jnp.where(idx == k, v, out)
    r_ref[...] = out


def _router(x, g, wrt, tm):
    t = x.shape[0]
    return pl.pallas_call(
        functools.partial(_router_body, tm=tm),
        grid=(t // tm,),
        in_specs=[
            pl.BlockSpec((tm, D_MODEL), lambda i: (i, 0)),
            pl.BlockSpec((1, D_MODEL), lambda i: (0, 0)),
            pl.BlockSpec((N_EXPERTS, D_MODEL), lambda i: (0, 0)),
        ],
        out_specs=[
            pl.BlockSpec((tm, D_MODEL), lambda i: (i, 0)),
            pl.BlockSpec((N_EXPERTS, tm), lambda i: (0, i)),
            pl.BlockSpec((N_EXPERTS, LANES), lambda i: (0, 0)),
        ],
        out_shape=[
            jax.ShapeDtypeStruct((t, D_MODEL), F32),
            jax.ShapeDtypeStruct((N_EXPERTS, t), F32),
            jax.ShapeDtypeStruct((N_EXPERTS, LANES), F32),
        ],
        scratch_shapes=[pltpu.VMEM((N_EXPERTS, LANES), F32)],
        compiler_params=_cparams("arbitrary", vmem=VMEM_LIMIT),
        name="router",
    )(x, g, wrt)


def _out_proj_router(x, mix_t, mem, w_layers, layer, g, wrt, tm):
    t = x.shape[0]
    per_seq = mix_t.shape[2] // tm
    return pl.pallas_call(
        functools.partial(_out_proj_router_body, tm=tm),
        grid=(t // tm,),
        in_specs=[
            pl.BlockSpec((tm, D_MODEL), lambda i: (i, 0)),
            pl.BlockSpec((1, MIX_W, tm), lambda i: (i // per_seq, 0, i % per_seq)),
            pl.BlockSpec((tm, MEM_W), lambda i: (i, 0)),
            pl.BlockSpec((1, D_MODEL, D_MODEL), lambda i: (layer, 0, 0)),
            pl.BlockSpec((1, D_MODEL), lambda i: (0, 0)),
            pl.BlockSpec((N_EXPERTS, D_MODEL), lambda i: (0, 0)),
        ],
        out_specs=[
            pl.BlockSpec((tm, D_MODEL), lambda i: (i, 0)),
            pl.BlockSpec((tm, D_MODEL), lambda i: (i, 0)),
            pl.BlockSpec((N_EXPERTS, tm), lambda i: (0, i)),
            pl.BlockSpec((N_EXPERTS, LANES), lambda i: (0, 0)),
        ],
        out_shape=[
            jax.ShapeDtypeStruct((t, D_MODEL), F32),
            jax.ShapeDtypeStruct((t, D_MODEL), F32),
            jax.ShapeDtypeStruct((N_EXPERTS, t), F32),
            jax.ShapeDtypeStruct((N_EXPERTS, LANES), F32),
        ],
        scratch_shapes=[pltpu.VMEM((N_EXPERTS, LANES), F32)],
        compiler_params=_cparams("arbitrary", vmem=VMEM_LIMIT),
        name="out_proj_router",
    )(x, mix_t, mem, w_layers, g, wrt)


_SCATTER_TM = 512


def _row_copy(src_ref, src_row, dst_ref, dst_row, sem):
    return pltpu.make_async_copy(src_ref.at[pl.ds(src_row, 1)], dst_ref.at[pl.ds(dst_row, 1)], sem)


def _scatter_body(d1_ref, d2_ref, hp_ref, hs_ref, xs_ref, sem, *, n_prompt_tiles):
    i = pl.program_id(0)

    def run(src_ref):
        n_rows = src_ref.shape[0]

        def start(r, carry):
            _row_copy(src_ref, r, xs_ref, d1_ref[0, 0, r], sem.at[0]).start(priority=0)
            _row_copy(src_ref, r, xs_ref, d2_ref[0, 0, r], sem.at[1]).start(priority=1)
            return carry

        lax.fori_loop(0, n_rows, start, 0, unroll=DMA_UNROLL)
        for k in range(2):
            pltpu.make_async_copy(src_ref, xs_ref.at[pl.ds(0, n_rows)], sem.at[k]).wait()

    @pl.when(i < n_prompt_tiles)
    def _():
        run(hp_ref)

    @pl.when(i >= n_prompt_tiles)
    def _():
        run(hs_ref)


def _scatter_rows(d1p, d2p, d1s, d2s, h_p, h_s, n_sorted):
    tm = _SCATTER_TM
    npt = h_p.shape[0] // tm
    n_s = h_s.shape[0]
    pad = jnp.zeros((tm - n_s,), jnp.int32)
    d1 = jnp.concatenate([d1p, d1s, pad])
    d2 = jnp.concatenate([d2p, d2s, pad])
    smem = lambda: pl.BlockSpec((1, 1, tm), lambda i: (i, 0, 0), memory_space=pltpu.SMEM)
    return pl.pallas_call(
        functools.partial(_scatter_body, n_prompt_tiles=npt),
        grid=(npt + 1,),
        in_specs=[
            smem(), smem(),
            pl.BlockSpec((tm, D_MODEL), lambda i: (jnp.minimum(i, npt - 1), 0)),
            pl.BlockSpec((n_s, D_MODEL), lambda i: (0, 0)),
        ],
        out_specs=pl.BlockSpec(memory_space=pl.ANY),
        out_shape=jax.ShapeDtypeStruct((n_sorted, D_MODEL), F32),
        scratch_shapes=[pltpu.SemaphoreType.DMA((2,))],
        compiler_params=_cparams("arbitrary", vmem=VMEM_LIMIT),
        name="moe_scatter",
    )(d1.reshape(-1, 1, tm), d2.reshape(-1, 1, tm), h_p, h_s)


def _gmm_body(tile_ref, exp_ref, lo_ref, hi_ref, first_ref, x_ref, wg_ref, wu_ref, wd_ref, o_ref):
    del tile_ref, exp_ref
    v = pl.program_id(0)
    c = pl.program_id(1)

    @pl.when((first_ref[v] == 1) & (c == 0))
    def _():
        o_ref[...] = jnp.zeros(o_ref.shape, F32)

    lo = lo_ref[v]
    hi = hi_ref[v]
    whole = (lo == 0) & (hi == MOE_TM)

    def swiglu_into(rows, mask):
        x = x_ref[rows, :]
        for off, n in _MOE_SUBS:
            cols = slice(off, off + n)
            a = _silu(_dot(x, wg_ref[0, :, cols])) * _dot(x, wu_ref[0, :, cols])
            y = _dot(a, wd_ref[0, cols, :])
            o_ref[rows, :] += y if mask is None else jnp.where(mask, y, 0.0)

    @pl.when(whole)
    def _():
        swiglu_into(slice(None), None)

    @pl.when(jnp.logical_not(whole) & (hi > lo))
    def _():
        for r0 in range(0, MOE_TM, MOE_SUB_ROWS):
            @pl.when((lo < r0 + MOE_SUB_ROWS) & (hi > r0))
            def _():
                rows = r0 + lax.broadcasted_iota(jnp.int32, (MOE_SUB_ROWS, 1), 0)
                swiglu_into(slice(r0, r0 + MOE_SUB_ROWS), (rows >= lo) & (rows < hi))


_MOE_FF = D_FF // MOE_FF_SPLIT
_MOE_SUBS = tuple((off, min(MXU_N, _MOE_FF - off)) for off in range(0, _MOE_FF, MXU_N))


def _gmm(meta, xs, w_gu, w_down):
    tile, expert, lo, hi, first = meta
    n_visits = tile.shape[0]
    grid_spec = pltpu.PrefetchScalarGridSpec(
        num_scalar_prefetch=5,
        grid=(n_visits, MOE_FF_SPLIT),
        in_specs=[
            pl.BlockSpec((MOE_TM, D_MODEL), lambda v, c, t, e, *_: (t[v], 0)),
            pl.BlockSpec((1, D_MODEL, _MOE_FF), lambda v, c, t, e, *_: (e[v], 0, c)),
            pl.BlockSpec((1, D_MODEL, _MOE_FF), lambda v, c, t, e, *_: (e[v], 0, MOE_FF_SPLIT + c)),
            pl.BlockSpec((1, _MOE_FF, D_MODEL), lambda v, c, t, e, *_: (e[v], c, 0)),
        ],
        out_specs=pl.BlockSpec((MOE_TM, D_MODEL), lambda v, c, t, e, *_: (t[v], 0)),
    )
    return pl.pallas_call(
        _gmm_body,
        grid_spec=grid_spec,
        out_shape=jax.ShapeDtypeStruct(xs.shape, F32),
        compiler_params=_cparams("arbitrary", "arbitrary", vmem=VMEM_LIMIT),
        name="moe_gmm",
    )(tile, expert, lo, hi, first, xs, w_gu, w_gu, w_down)


def _gmm_meta(counts, n_sorted):
    n_tiles = n_sorted // MOE_TM
    n_visits = n_tiles + N_EXPERTS - 1
    ends = jnp.cumsum(counts)
    starts = ends - counts
    first_tile = starts // MOE_TM
    last_tile = jnp.maximum(ends - 1, 0) // MOE_TM
    nv = jnp.where(counts > 0, last_tile - first_tile + 1, 0)
    cv = jnp.cumsum(nv)
    v = jnp.arange(n_visits, dtype=jnp.int32)
    total = cv[-1]
    valid = v < total
    vc = jnp.minimum(v, total - 1)
    expert = jnp.sum((cv[None, :] <= vc[:, None]).astype(jnp.int32), axis=1)
    tile = first_tile[expert] + (vc - (cv[expert] - nv[expert]))
    lo = jnp.maximum(starts[expert], tile * MOE_TM) - tile * MOE_TM
    hi = jnp.minimum(ends[expert], (tile + 1) * MOE_TM) - tile * MOE_TM
    lo = jnp.where(valid, lo, 0)
    hi = jnp.where(valid, hi, 0)
    prev_tile = jnp.concatenate([jnp.full((1,), -1, jnp.int32), tile[:-1]])
    first = (valid & (tile != prev_tile)).astype(jnp.int32)
    as_i32 = lambda a: a.astype(jnp.int32)
    return as_i32(tile), as_i32(expert), as_i32(lo), as_i32(hi), first


def _combine_body(d1_ref, d2_ref, x_ref, g_ref, os_ref, y_ref, buf_ref, sem, *, tm, n_tiles):
    i = pl.program_id(0)
    slot = i % 2

    def issue(tile, slot_):
        def body(r, carry):
            _row_copy(os_ref, d1_ref[tile, r], buf_ref.at[slot_, 0], r, sem.at[slot_, 0]).start(priority=0)
            _row_copy(os_ref, d2_ref[tile, r], buf_ref.at[slot_, 1], r, sem.at[slot_, 1]).start(priority=1)
            return carry
        lax.fori_loop(0, tm, body, 0, unroll=DMA_UNROLL)

    @pl.when(i == 0)
    def _():
        issue(0, 0)

    @pl.when(i + 1 < n_tiles)
    def _():
        issue(i + 1, 1 - slot)

    for k in range(2):
        pltpu.make_async_copy(os_ref.at[pl.ds(0, tm)], buf_ref.at[slot, k], sem.at[slot, k]).wait()

    g = g_ref[...]
    y_ref[...] = x_ref[...] + g[:, 0:1] * buf_ref[slot, 0] + g[:, 1:2] * buf_ref[slot, 1]


def _combine(d1, d2, x, gates, o_sorted, tm):
    t = x.shape[0]
    n_tiles = t // tm
    grid_spec = pltpu.PrefetchScalarGridSpec(
        num_scalar_prefetch=2,
        grid=(n_tiles,),
        in_specs=[
            pl.BlockSpec((tm, D_MODEL), lambda i, *_: (i, 0)),
            pl.BlockSpec((tm, 2), lambda i, *_: (i, 0)),
            pl.BlockSpec(memory_space=pl.ANY),
        ],
        out_specs=pl.BlockSpec((tm, D_MODEL), lambda i, *_: (i, 0)),
        scratch_shapes=[pltpu.VMEM((2, 2, tm, D_MODEL), F32), pltpu.SemaphoreType.DMA((2, 2))],
    )
    return pl.pallas_call(
        functools.partial(_combine_body, tm=tm, n_tiles=n_tiles),
        grid_spec=grid_spec,
        out_shape=jax.ShapeDtypeStruct((t, D_MODEL), F32),
        compiler_params=_cparams("arbitrary", vmem=VMEM_LIMIT),
        name="moe_combine",
    )(d1.reshape(n_tiles, tm), d2.reshape(n_tiles, tm), x, gates, o_sorted)


def _block_diag(w_grp):
    g, n, _ = w_grp.shape
    out = jnp.zeros((g * n, g * n), w_grp.dtype)
    for i in range(g):
        out = out.at[i * n:(i + 1) * n, i * n:(i + 1) * n].set(w_grp[i])
    return out


def kernel(x_prompt, x_sample, cache_mem_k, cache_mem_v, state_pool, cache_fox_k, cache_fox_v, cache_fox_logf, page_table, mem_prompt, norm_mix_g, norm_ffn_g, norm_mem_g, w_mem_kv, mem_q_norm_g, mem_k_norm_g, w_out, w_in_pool, w_pool_group, pool_scale, w_in_fox, b_forget, fox_q_norm_g, fox_k_norm_g, w_ffn_gu, w_ffn_down, w_router, w_exp_gu, w_exp_down):
    b, s, d = x_prompt.shape
    bs, ls, _ = x_sample.shape
    tp, ts = b * s, bs * ls
    tm_p, tm_s = PROMPT_TM, ts

    xp = x_prompt.reshape(tp, d)
    xs = x_sample.reshape(ts, d)

    mem_kt_p, mem_vt_p = _memkv(mem_prompt, norm_mem_g, w_mem_kv, mem_k_norm_g)
    mem_kt_s = cache_mem_k.transpose(0, 1, 3, 4, 2).reshape(2, bs, MEM_W, MEM_LEN)
    mem_vt_s = cache_mem_v.transpose(0, 1, 3, 4, 2).reshape(2, bs, MEM_W, MEM_LEN)

    def mem_attend_sample(qm_s, layer):
        qs8 = jnp.pad(qm_s.reshape(bs, ls, MEM_W), ((0, 0), (0, SAMPLE_ROWS - ls), (0, 0)))
        mo_s = _mem_attend(qs8, mem_kt_s, mem_vt_s, layer, mem_q_norm_g[layer], SAMPLE_ROWS, MEM_SEQS_PER_STEP)
        return mo_s[:, :ls].reshape(ts, MEM_W)

    g_mix0 = norm_mix_g[0][None, :]
    w_bd = _block_diag(w_pool_group[0])
    pscale = pool_scale[0][None, :]
    xp, up_tail = _pool_layer_prompt(x_prompt, g_mix0, w_in_pool[0], w_bd, pscale, mem_kt_p[0], mem_vt_p[0],
                                     mem_q_norm_g[0], w_out[0], tm_p)
    xp = xp.reshape(tp, d)
    pool_p = up_tail[:, _POOL_HALO - POOL_STATE:][None]
    us, qms = _pool_proj(xs, g_mix0, w_in_pool[0], tm_s)
    us_t = us.reshape(bs, ls, MIX_W).transpose(1, 0, 2)
    mix_s_t, new_state_t = _pool_sample(state_pool[0].transpose(1, 0, 2), us_t, w_bd, pscale)
    mix_s = mix_s_t.transpose(1, 0, 2).reshape(ts, MIX_W)
    pool_s = new_state_t.transpose(1, 0, 2)[None]
    xs = _out_proj(xs, mix_s, mem_attend_sample(qms, 0), w_out, 0, tm_s)
    g_ffn0 = norm_ffn_g[0][None, :]
    xp = _ffn(xp, g_ffn0, w_ffn_gu[0], w_ffn_down[0], tm_p)
    xs = _ffn(xs, g_ffn0, w_ffn_gu[0], w_ffn_down[0], tm_s)

    w_in = w_in_fox[0]
    wq = w_in[:, :MIX_W]
    wk = w_in[:, MIX_W:2 * MIX_W]
    wv = w_in[:, 2 * MIX_W:3 * MIX_W]
    wf = jnp.pad(w_in[:, 3 * MIX_W:3 * MIX_W + FOX_HEADS], ((0, 0), (0, LANES - FOX_HEADS)))
    wqm = w_in[:, 3 * MIX_W + FOX_HEADS:]
    bf_row = jnp.pad(b_forget[0], (0, LANES - FOX_HEADS))[None, :]
    qg_t = jnp.tile(fox_q_norm_g[0], FOX_HEADS)[None, :]
    kg_t = jnp.tile(fox_k_norm_g[0], FOX_HEADS)[None, :]
    g_mix1 = norm_mix_g[1][None, :]
    qpp, kpp, ktp, vtp, vtb, lftp, qmp = _fox_proj_prompt(
        xp, g_mix1, wq.T, wk.T, wv.T, wqm, wf, bf_row, fox_q_norm_g[0][:, None], fox_k_norm_g[0][:, None],
        tm=tm_p, n_seq=b)
    qs, kns, vns, lfts, qms = _fox_proj_sample(xs, g_mix1, wq, wk, wv, wqm, wf, bf_row, qg_t, kg_t)

    mix_pt = _fox_flash(qpp, kpp, vtb, FLASH_BLK)

    pad_rows = ((0, 0), (0, SAMPLE_ROWS - ls), (0, 0))
    q8 = jnp.pad(qs.reshape(bs, ls, MIX_W), pad_rows)
    kn8 = jnp.pad(kns.reshape(bs, ls, MIX_W), pad_rows)
    vn8 = jnp.pad(vns.reshape(bs, ls, MIX_W), pad_rows)
    lfn = jnp.pad(lfts.reshape(_FL_ROWS, bs, ls).transpose(1, 0, 2),
                  ((0, 0), (0, 0), (0, SAMPLE_ROWS - ls)))
    n_phys = cache_fox_k.shape[1]
    kt_pages = cache_fox_k[0].transpose(0, 2, 3, 1).reshape(n_phys, MIX_W, PAGE_SIZE)
    vt_pages = cache_fox_v[0].transpose(0, 2, 3, 1).reshape(n_phys, MIX_W, PAGE_SIZE)
    lf_heads = cache_fox_logf[0].transpose(2, 0, 1)
    mix_s = _fox_decode(page_table, kt_pages, vt_pages, lf_heads, q8, kn8, vn8, lfn, ls)
    mix_s = mix_s[:, :ls].reshape(ts, MIX_W)

    mo_p = _mem_attend(qmp.reshape(b, s, MEM_W), mem_kt_p, mem_vt_p, 1, mem_q_norm_g[1],
                       tm_p, 1).reshape(tp, MEM_W)
    xs = _out_proj(xs, mix_s, mem_attend_sample(qms, 1), w_out, 1, tm_s)

    g_ffn1 = norm_ffn_g[1][None, :]
    wrt = w_router[0].T
    xp, hp, rp, cnt_p = _out_proj_router(xp, mix_pt, mo_p, w_out, 1, g_ffn1, wrt, tm_p)
    hs, rs, cnt_s = _router(xs, g_ffn1, wrt, tm_s)
    cnt_p = cnt_p[:, 0].astype(jnp.int32)
    cnt_s = cnt_s[:, 0].astype(jnp.int32)
    counts = cnt_p + cnt_s
    offsets = jnp.cumsum(counts) - counts
    n_sorted = 2 * (tp + ts)

    def dests(r, base):
        i1, i2 = r[0].astype(jnp.int32), r[1].astype(jnp.int32)
        return base[i1] + r[2].astype(jnp.int32), base[i2] + r[3].astype(jnp.int32)

    d1p, d2p = dests(rp, offsets)
    d1s, d2s = dests(rs, offsets + cnt_p)
    x_sorted = _scatter_rows(d1p, d2p, d1s, d2s, hp, hs, n_sorted)
    o_sorted = _gmm(_gmm_meta(counts, n_sorted), x_sorted, w_exp_gu[0], w_exp_down[0])
    yp = _combine(d1p, d2p, xp, rp[4:6].T, o_sorted, tm_p)
    ys = _combine(d1s, d2s, xs, rs[4:6].T, o_sorted, ts)

    def heads_t(a_t, n_b, n_h):
        return a_t.reshape(n_b, n_h, HEAD_DIM, a_t.shape[-1]).transpose(0, 3, 1, 2)

    mem_k_p = jnp.stack([heads_t(mem_kt_p[i], b, MEM_HEADS) for i in range(2)])
    mem_v_p = jnp.stack([heads_t(mem_vt_p[i], b, MEM_HEADS) for i in range(2)])
    fk_p = heads_t(ktp, b, FOX_HEADS)[None]
    fv_p = heads_t(vtp, b, FOX_HEADS)[None]
    fl_p = lftp[:FOX_HEADS].reshape(FOX_HEADS, b, s).transpose(1, 2, 0)[None]
    fk_s = kns.reshape(1, bs, ls, FOX_HEADS, HEAD_DIM)
    fv_s = vns.reshape(1, bs, ls, FOX_HEADS, HEAD_DIM)
    fl_s = lfts[:FOX_HEADS].reshape(FOX_HEADS, bs, ls).transpose(1, 2, 0)[None]
    return (yp.reshape(b, s, d), ys.reshape(bs, ls, d), mem_k_p, mem_v_p, pool_p, pool_s,
            fk_p, fv_p, fl_p, fk_s, fv_s, fl_s)
```

```python
import functools
import math

import jax
import jax.numpy as jnp
from jax import lax
from jax.experimental import pallas as pl
from jax.experimental.pallas import tpu as pltpu

F32 = jnp.float32
BF16 = jnp.bfloat16

D_MODEL = 1024
HEAD_DIM = 64
MEM_LEN = 256
MEM_HEADS = 4
MEM_W = MEM_HEADS * HEAD_DIM
MIX_W = D_MODEL - MEM_W
FOX_HEADS = MIX_W // HEAD_DIM
POOL_WINDOWS = (2, 4, 8, 16)
POOL_GW = MIX_W // len(POOL_WINDOWS)
POOL_STATE = max(POOL_WINDOWS) - 1
D_FF = 2816
N_EXPERTS = 8
PAGE_SIZE = 128
EPS = 1e-6
SCALE = HEAD_DIM ** -0.5
LOG2E = math.log2(math.e)

LANES = 128
SUBLANES = 8
MXU_N = 256
VMEM_LIMIT = 56 * 1024 * 1024

FF_CHUNK = MXU_N
N_FF_CHUNKS = D_FF // FF_CHUNK
MOE_TM = 1032
PAGES_PER_STEP = 8
SAMPLE_ROWS = 8
DMA_UNROLL = 8
PROMPT_TM = 512
FLASH_BLK = 512
MEM_SEQS_PER_STEP = 4
DECODE_SLOTS = 3
FLASH_HEADS = 12
MOE_FF_SPLIT = 2
MOE_SUB_ROWS = 344


def _cparams(*sem, vmem=None):
    return pltpu.CompilerParams(dimension_semantics=sem, vmem_limit_bytes=vmem)


def _dot(a, b):
    return jnp.dot(a, b, preferred_element_type=F32)


def _dot_nt(a, b):
    return lax.dot_general(a, b, (((1,), (1,)), ((), ())), preferred_element_type=F32)


def _rms_rows(x, g):
    return x * lax.rsqrt(jnp.mean(x * x, axis=-1, keepdims=True) + EPS) * g


def _split3(x):
    hi = x.astype(BF16)
    r = x - hi.astype(F32)
    mid = r.astype(BF16)
    lo = (r - mid.astype(F32)).astype(BF16)
    return hi, mid, lo


def _head_sumsq_lanes(x):
    r = lax.broadcasted_iota(jnp.int32, (LANES, LANES), 0) // HEAD_DIM
    c = lax.broadcasted_iota(jnp.int32, (LANES, LANES), 1) // HEAD_DIM
    ones_bd = (r == c).astype(BF16)
    xx = x * x
    hi = xx.astype(BF16)
    lo = (xx - hi.astype(F32)).astype(BF16)
    parts = []
    for j in range(x.shape[1] // LANES):
        sl = slice(j * LANES, (j + 1) * LANES)
        parts.append(_dot(hi[:, sl], ones_bd) + _dot(lo[:, sl], ones_bd))
    return jnp.concatenate(parts, axis=1)


def _head_rms_lanes(x, g_tiled):
    ssq = _head_sumsq_lanes(x)
    return x * lax.rsqrt(ssq * (1.0 / HEAD_DIM) + EPS) * g_tiled


def _head_rms_rows_t(xt, g_col, n_heads):
    outs = []
    for h in range(n_heads):
        blk = xt[h * HEAD_DIM:(h + 1) * HEAD_DIM, :]
        ms = jnp.mean(blk * blk, axis=0, keepdims=True)
        outs.append(blk * lax.rsqrt(ms + EPS) * g_col)
    return outs


def _log_sigmoid(x):
    return jnp.minimum(x, 0.0) - jnp.log1p(jnp.exp(-jnp.abs(x)))


def _lane_head(shape, axis):
    return lax.broadcasted_iota(jnp.int32, shape, axis) // HEAD_DIM


def _memkv_body(mem_ref, gm_ref, w_ref, kg_ref, kt_ref, vt_ref):
    h = _rms_rows(mem_ref[0], gm_ref[0])
    z = _dot(h, w_ref[0])
    kt = z[:, :MEM_W].T
    vt_ref[0, 0] = z[:, MEM_W:].T
    pieces = _head_rms_rows_t(kt, kg_ref[0], MEM_HEADS)
    for h_i, p in enumerate(pieces):
        kt_ref[0, 0, h_i * HEAD_DIM:(h_i + 1) * HEAD_DIM, :] = p


def _memkv(mem, g_mem, w_kv, kn_g):
    depth, batch = w_kv.shape[0], mem.shape[0]
    out = jax.ShapeDtypeStruct((depth, batch, MEM_W, MEM_LEN), F32)
    return pl.pallas_call(
        _memkv_body,
        grid=(depth, batch),
        in_specs=[
            pl.BlockSpec((1, MEM_LEN, D_MODEL), lambda i, b: (b, 0, 0)),
            pl.BlockSpec((1, 1, D_MODEL), lambda i, b: (i, 0, 0)),
            pl.BlockSpec((1, D_MODEL, 2 * MEM_W), lambda i, b: (i, 0, 0)),
            pl.BlockSpec((1, HEAD_DIM, 1), lambda i, b: (i, 0, 0)),
        ],
        out_specs=[pl.BlockSpec((1, 1, MEM_W, MEM_LEN), lambda i, b: (i, b, 0, 0))] * 2,
        out_shape=[out, out],
        compiler_params=_cparams("arbitrary", "arbitrary"),
        name="memkv",
    )(mem, g_mem[:, None, :], w_kv, kn_g[:, :, None])


def _pool_proj_body(x_ref, g_ref, w_ref, u_ref, qm_ref):
    h = _rms_rows(x_ref[...], g_ref[...])
    u_ref[...] = _dot(h, w_ref[:, :MIX_W])
    qm_ref[...] = _dot(h, w_ref[:, MIX_W:])


def _pool_proj(x, g, w, tm):
    t = x.shape[0]
    return pl.pallas_call(
        _pool_proj_body,
        grid=(t // tm,),
        in_specs=[
            pl.BlockSpec((tm, D_MODEL), lambda i: (i, 0)),
            pl.BlockSpec((1, D_MODEL), lambda i: (0, 0)),
            pl.BlockSpec((D_MODEL, D_MODEL), lambda i: (0, 0)),
        ],
        out_specs=[pl.BlockSpec((tm, MIX_W), lambda i: (i, 0)),
                   pl.BlockSpec((tm, MEM_W), lambda i: (i, 0))],
        out_shape=[jax.ShapeDtypeStruct((t, MIX_W), F32), jax.ShapeDtypeStruct((t, MEM_W), F32)],
        compiler_params=_cparams("arbitrary", vmem=VMEM_LIMIT),
        name="pool_proj",
    )(x, g, w)


def _window_of_lane(shape):
    lane = lax.broadcasted_iota(jnp.int32, shape, len(shape) - 1)
    return jnp.where(lane < POOL_GW, POOL_WINDOWS[0],
                     jnp.where(lane < 2 * POOL_GW, POOL_WINDOWS[1],
                               jnp.where(lane < 3 * POOL_GW, POOL_WINDOWS[2], POOL_WINDOWS[3])))


def _pool_select(s2, s4, s8, s16, shape):
    lane = lax.broadcasted_iota(jnp.int32, shape, len(shape) - 1)
    return jnp.where(lane < POOL_GW, s2,
                     jnp.where(lane < 2 * POOL_GW, s4, jnp.where(lane < 3 * POOL_GW, s8, s16)))


_POOL_PAD = SUBLANES
_POOL_HALO = 2 * SUBLANES
_POOL_BASE = _POOL_PAD + _POOL_HALO


def _pool_mix_tile(u, li, tl, e_ref, s2_ref, s4_ref, s8_ref, w_bd, scale):
    n = _POOL_HALO + tl

    @pl.when(li == 0)
    def _():
        e_ref[0:_POOL_BASE, :] = jnp.zeros((_POOL_BASE, MIX_W), F32)
        s2_ref[0:_POOL_PAD, :] = jnp.zeros((_POOL_PAD, MIX_W), F32)
        s4_ref[0:_POOL_PAD, :] = jnp.zeros((_POOL_PAD, MIX_W), F32)
        s8_ref[0:_POOL_PAD, :] = jnp.zeros((_POOL_PAD, MIX_W), F32)

    e_ref[_POOL_BASE:_POOL_BASE + tl, :] = u
    s2 = e_ref[_POOL_PAD:_POOL_PAD + n, :] + e_ref[_POOL_PAD - 1:_POOL_PAD - 1 + n, :]
    s2_ref[_POOL_PAD:_POOL_PAD + n, :] = s2
    s4 = s2 + s2_ref[_POOL_PAD - 2:_POOL_PAD - 2 + n, :]
    s4_ref[_POOL_PAD:_POOL_PAD + n, :] = s4
    s8 = s4 + s4_ref[_POOL_PAD - 4:_POOL_PAD - 4 + n, :]
    s8_ref[_POOL_PAD:_POOL_PAD + n, :] = s8
    s16 = s8[_POOL_HALO:, :] + s8_ref[_POOL_BASE - 8:_POOL_BASE - 8 + tl, :]
    shape = (tl, MIX_W)
    ssel = _pool_select(s2[_POOL_HALO:, :], s4[_POOL_HALO:, :], s8[_POOL_HALO:, :], s16, shape)
    pos = li * tl + lax.broadcasted_iota(jnp.int32, shape, 0)
    cnt = jnp.minimum(_window_of_lane(shape), pos + 1).astype(F32)
    diff = ssel / cnt - u
    mix = _dot(diff, w_bd) * scale
    e_ref[_POOL_PAD:_POOL_BASE, :] = e_ref[_POOL_PAD + tl:_POOL_BASE + tl, :]
    return mix


def _pool_layer_prompt_body(x_ref, g_ref, win_ref, wbd_ref, sc_ref, kt_ref, vt_ref, qg_ref, wo_ref,
                            o_ref, st_ref, e_ref, s2_ref, s4_ref, s8_ref, *, tl):
    x = x_ref[0]
    h = _rms_rows(x, g_ref[...])
    u = _dot(h, win_ref[:, :MIX_W])
    qm = _dot(h, win_ref[:, MIX_W:])
    st_ref[0] = u[tl - _POOL_HALO:, :]
    mix = _pool_mix_tile(u, pl.program_id(1), tl, e_ref, s2_ref, s4_ref, s8_ref, wbd_ref[...], sc_ref[...])
    mem = _mem_attend_tile(qm, kt_ref[0], vt_ref[0], qg_ref[...])
    o_ref[0] = x + _dot(mix, wo_ref[:MIX_W, :]) + _dot(mem, wo_ref[MIX_W:, :])


def _pool_layer_prompt(x, g, w_in, w_bd, scale, mem_kt, mem_vt, qn_g, w_out, tl):
    b, s, _ = x.shape
    rows = _POOL_BASE + tl
    const = lambda shape: pl.BlockSpec(shape, lambda bi, li: tuple(0 for _ in shape))
    per_b = lambda shape: pl.BlockSpec(shape, lambda bi, li: (bi, 0, 0))
    return pl.pallas_call(
        functools.partial(_pool_layer_prompt_body, tl=tl),
        grid=(b, s // tl),
        in_specs=[
            pl.BlockSpec((1, tl, D_MODEL), lambda bi, li: (bi, li, 0)),
            const((1, D_MODEL)), const((D_MODEL, D_MODEL)), const((MIX_W, MIX_W)), const((1, MIX_W)),
            per_b((1, MEM_W, MEM_LEN)), per_b((1, MEM_W, MEM_LEN)), const((1, MEM_W)),
            const((D_MODEL, D_MODEL)),
        ],
        out_specs=[pl.BlockSpec((1, tl, D_MODEL), lambda bi, li: (bi, li, 0)),
                   per_b((1, _POOL_HALO, MIX_W))],
        out_shape=[jax.ShapeDtypeStruct((b, s, D_MODEL), F32),
                   jax.ShapeDtypeStruct((b, _POOL_HALO, MIX_W), F32)],
        scratch_shapes=[pltpu.VMEM((rows, MIX_W), F32)] * 4,
        compiler_params=_cparams("arbitrary", "arbitrary", vmem=VMEM_LIMIT),
        name="pool_layer_prompt",
    )(x, g, w_in, w_bd, scale, mem_kt, mem_vt, jnp.tile(qn_g, MEM_HEADS)[None, :], w_out)


def _pool_sample_body(st_ref, u_ref, w_ref, sc_ref, o_ref, ns_ref, *, n_new):
    def ext(j):
        return st_ref[j] if j < POOL_STATE else u_ref[j - POOL_STATE]

    for l in range(n_new):
        r = POOL_STATE + l
        s2 = ext(r) + ext(r - 1)
        s4 = s2 + ext(r - 2) + ext(r - 3)
        s8 = s4
        for j in range(4, 8):
            s8 = s8 + ext(r - j)
        s16 = s8
        for j in range(8, 16):
            s16 = s16 + ext(r - j)
        shape = s2.shape
        cnt = _window_of_lane(shape).astype(F32)
        diff = _pool_select(s2, s4, s8, s16, shape) / cnt - ext(r)
        o_ref[l] = _dot(diff, w_ref[...]) * sc_ref[...]
    for j in range(POOL_STATE):
        ns_ref[j] = ext(j + n_new)


def _pool_sample(state_t, u_t, w_bd, scale):
    n_new, b, _ = u_t.shape
    return pl.pallas_call(
        functools.partial(_pool_sample_body, n_new=n_new),
        out_shape=[jax.ShapeDtypeStruct((n_new, b, MIX_W), F32),
                   jax.ShapeDtypeStruct((POOL_STATE, b, MIX_W), F32)],
        name="pool_sample",
    )(state_t, u_t, w_bd, scale)


def _mem_attend_tile(qm, kt, vt, g_tiled):
    q = _head_rms_lanes(qm, g_tiled) * SCALE
    lh = _lane_head(q.shape, 1)
    out = jnp.zeros(q.shape, F32)
    for h in range(MEM_HEADS):
        s = _dot(jnp.where(lh == h, q, 0.0), kt)
        e = jnp.exp(s - jnp.max(s, axis=-1, keepdims=True))
        p = e / jnp.sum(e, axis=-1, keepdims=True)
        out = jnp.where(lh == h, _dot_nt(p, vt), out)
    return out


def _mem_attend_body(q_ref, kt_ref, vt_ref, g_ref, o_ref):
    for i in range(q_ref.shape[0]):
        o_ref[i] = _mem_attend_tile(q_ref[i], kt_ref[0, i], vt_ref[0, i], g_ref[...])


def _mem_attend(qm, kt_layers, vt_layers, layer, qn_g, tl, bb):
    b, l, _ = qm.shape
    kv_spec = pl.BlockSpec((1, bb, MEM_W, MEM_LEN), lambda bi, li: (layer, bi, 0, 0))
    return pl.pallas_call(
        _mem_attend_body,
        grid=(b // bb, l // tl),
        in_specs=[
            pl.BlockSpec((bb, tl, MEM_W), lambda bi, li: (bi, li, 0)),
            kv_spec, kv_spec,
            pl.BlockSpec((1, MEM_W), lambda bi, li: (0, 0)),
        ],
        out_specs=pl.BlockSpec((bb, tl, MEM_W), lambda bi, li: (bi, li, 0)),
        out_shape=jax.ShapeDtypeStruct((b, l, MEM_W), F32),
        compiler_params=_cparams("arbitrary", "arbitrary"),
        name="mem_attend",
    )(qm, kt_layers, vt_layers, jnp.tile(qn_g, MEM_HEADS)[None, :])


def _out_proj_body(x_ref, mix_ref, mem_ref, w_ref, o_ref):
    o_ref[...] = (x_ref[...] + _dot(mix_ref[...], w_ref[0, :MIX_W, :])
                  + _dot(mem_ref[...], w_ref[0, MIX_W:, :]))


def _out_proj(x, mix, mem, w_layers, layer, tm):
    t = x.shape[0]
    return pl.pallas_call(
        _out_proj_body,
        grid=(t // tm,),
        in_specs=[
            pl.BlockSpec((tm, D_MODEL), lambda i: (i, 0)),
            pl.BlockSpec((tm, MIX_W), lambda i: (i, 0)),
            pl.BlockSpec((tm, MEM_W), lambda i: (i, 0)),
            pl.BlockSpec((1, D_MODEL, D_MODEL), lambda i: (layer, 0, 0)),
        ],
        out_specs=pl.BlockSpec((tm, D_MODEL), lambda i: (i, 0)),
        out_shape=jax.ShapeDtypeStruct((t, D_MODEL), F32),
        compiler_params=_cparams("arbitrary", vmem=VMEM_LIMIT),
        name="out_proj",
    )(x, mix, mem, w_layers)


def _silu(g):
    return g / (1.0 + jnp.exp(-g))


def _ffn_body(x_ref, g_ref, wgu_ref, wd_ref, o_ref):
    x = x_ref[...]
    h = _rms_rows(x, g_ref[...])
    o_ref[...] = x
    for c in range(N_FF_CHUNKS):
        gate = slice(c * FF_CHUNK, (c + 1) * FF_CHUNK)
        up = slice(D_FF + c * FF_CHUNK, D_FF + (c + 1) * FF_CHUNK)
        a = _silu(_dot(h, wgu_ref[:, gate])) * _dot(h, wgu_ref[:, up])
        o_ref[...] += _dot(a, wd_ref[gate, :])


def _ffn(x, g, w_gu, w_down, tm):
    t = x.shape[0]
    resident = lambda shape: pl.BlockSpec(shape, lambda i: (0, 0), pipeline_mode=pl.Buffered(1))
    return pl.pallas_call(
        _ffn_body,
        grid=(t // tm,),
        in_specs=[
            pl.BlockSpec((tm, D_MODEL), lambda i: (i, 0)),
            pl.BlockSpec((1, D_MODEL), lambda i: (0, 0)),
            resident((D_MODEL, 2 * D_FF)),
            resident((D_FF, D_MODEL)),
        ],
        out_specs=pl.BlockSpec((tm, D_MODEL), lambda i: (i, 0)),
        out_shape=jax.ShapeDtypeStruct((t, D_MODEL), F32),
        compiler_params=_cparams("arbitrary", vmem=VMEM_LIMIT),
        name="ffn",
    )(x, g, w_gu, w_down)


_FL_ROWS = 2 * SUBLANES
_AUG0 = HEAD_DIM


def _aug_pieces(c_col):
    hi, mid, lo = _split3(c_col)
    bc = lambda a: jnp.broadcast_to(a.astype(F32), (c_col.shape[0], LANES))
    return bc(hi), bc(mid), bc(lo)


def _fox_proj_prompt_body(x_ref, g_ref, wq_ref, wk_ref, wv_ref, wqm_ref, wf_ref, bf_ref, qg_ref, kg_ref,
                          qp_ref, kp_ref, kt_ref, vt_ref, vtb_ref, lft_ref, qm_ref, carry_ref, *, tm, per_seq):
    @pl.when(pl.program_id(0) % per_seq == 0)
    def _():
        carry_ref[...] = jnp.zeros(carry_ref.shape, F32)

    h = _rms_rows(x_ref[...], g_ref[...])
    q_pieces = _head_rms_rows_t(_dot_nt(wq_ref[...], h), qg_ref[...], FOX_HEADS)
    k_pieces = _head_rms_rows_t(_dot_nt(wk_ref[...], h), kg_ref[...], FOX_HEADS)
    for h_i, piece in enumerate(k_pieces):
        kt_ref[0, h_i * HEAD_DIM:(h_i + 1) * HEAD_DIM, :] = piece
    vt = _dot_nt(wv_ref[...], h)
    vt_ref[0] = vt
    vtb_ref[0] = vt.astype(BF16)
    q = jnp.concatenate(q_pieces, axis=0).T * (SCALE * LOG2E)
    k = jnp.concatenate(k_pieces, axis=0).T
    qm_ref[...] = _dot(h, wqm_ref[...])
    lane = lax.broadcasted_iota(jnp.int32, (tm, LANES), 1)
    lf = jnp.where(lane < FOX_HEADS, _log_sigmoid(_dot(h, wf_ref[...]) + bf_ref[...]), 0.0)
    lft_ref[...] = lf.T[:_FL_ROWS, :]
    r = lax.broadcasted_iota(jnp.int32, (tm, tm), 0)
    c = lax.broadcasted_iota(jnp.int32, (tm, tm), 1)
    lower = (c <= r).astype(BF16)
    hi, mid, lo = _split3(lf)
    csum = _dot(lower, hi) + _dot(lower, mid) + _dot(lower, lo) + carry_ref[0:1, :]
    carry_ref[...] = jnp.broadcast_to(csum[tm - 1:tm, :], carry_ref.shape)
    c2 = csum * LOG2E
    one = jnp.ones((tm, LANES), F32)
    zero = jnp.zeros((tm, LANES), F32)
    for hd in range(FOX_HEADS):
        chi, cmid, clo = _aug_pieces(c2[:, hd:hd + 1])
        aug_q = jnp.where(lane == _AUG0, chi, jnp.where(lane == _AUG0 + 1, cmid, jnp.where(
            lane == _AUG0 + 2, clo, jnp.where(lane < _AUG0 + 6, one, zero))))
        aug_k = jnp.where(lane < _AUG0 + 3, one, jnp.where(lane == _AUG0 + 3, -chi, jnp.where(
            lane == _AUG0 + 4, -cmid, jnp.where(lane == _AUG0 + 5, -clo, zero))))
        col = slice((hd // 2) * LANES, (hd // 2 + 1) * LANES)
        qc, kc = q[:, col], k[:, col]
        if hd % 2:
            qc = pltpu.roll(qc, HEAD_DIM, 1)
            kc = pltpu.roll(kc, HEAD_DIM, 1)
        qp_ref[0, hd] = jnp.where(lane < HEAD_DIM, qc, aug_q).astype(BF16)
        kp_ref[0, hd] = jnp.where(lane < HEAD_DIM, kc, aug_k).astype(BF16)


def _fox_proj_prompt(x, g, wq_t, wk_t, wv_t, wqm, wf, bf_row, qg_col, kg_col, tm, n_seq):
    t = x.shape[0]
    seq = t // n_seq
    per_seq = seq // tm

    def const(shape):
        return pl.BlockSpec(shape, lambda i: tuple(0 for _ in shape))

    heads_spec = pl.BlockSpec((1, FOX_HEADS, tm, LANES), lambda i: (i // per_seq, 0, i % per_seq, 0))
    t_spec = pl.BlockSpec((1, MIX_W, tm), lambda i: (i // per_seq, 0, i % per_seq))
    return pl.pallas_call(
        functools.partial(_fox_proj_prompt_body, tm=tm, per_seq=per_seq),
        grid=(t // tm,),
        in_specs=[
            pl.BlockSpec((tm, D_MODEL), lambda i: (i, 0)),
            const((1, D_MODEL)),
            const((MIX_W, D_MODEL)), const((MIX_W, D_MODEL)), const((MIX_W, D_MODEL)),
            const((D_MODEL, MEM_W)), const((D_MODEL, LANES)),
            const((1, LANES)), const((HEAD_DIM, 1)), const((HEAD_DIM, 1)),
        ],
        out_specs=[heads_spec, heads_spec, t_spec, t_spec, t_spec,
                   pl.BlockSpec((_FL_ROWS, tm), lambda i: (0, i)),
                   pl.BlockSpec((tm, MEM_W), lambda i: (i, 0))],
        out_shape=[
            jax.ShapeDtypeStruct((n_seq, FOX_HEADS, seq, LANES), BF16),
            jax.ShapeDtypeStruct((n_seq, FOX_HEADS, seq, LANES), BF16),
            jax.ShapeDtypeStruct((n_seq, MIX_W, seq), F32),
            jax.ShapeDtypeStruct((n_seq, MIX_W, seq), F32),
            jax.ShapeDtypeStruct((n_seq, MIX_W, seq), BF16),
            jax.ShapeDtypeStruct((_FL_ROWS, t), F32),
            jax.ShapeDtypeStruct((t, MEM_W), F32),
        ],
        scratch_shapes=[pltpu.VMEM((SUBLANES, LANES), F32)],
        compiler_params=_cparams("arbitrary", vmem=VMEM_LIMIT),
        name="fox_proj_prompt",
    )(x, g, wq_t, wk_t, wv_t, wqm, wf, bf_row, qg_col, kg_col)


def _fox_proj_sample_body(x_ref, g_ref, wq_ref, wk_ref, wv_ref, wqm_ref, wf_ref, bf_ref, qg_ref, kg_ref,
                          q_ref, k_ref, v_ref, lft_ref, qm_ref):
    h = _rms_rows(x_ref[...], g_ref[...])
    q_ref[...] = _head_rms_lanes(_dot(h, wq_ref[...]), qg_ref[...])
    k_ref[...] = _head_rms_lanes(_dot(h, wk_ref[...]), kg_ref[...])
    v_ref[...] = _dot(h, wv_ref[...])
    qm_ref[...] = _dot(h, wqm_ref[...])
    lane = lax.broadcasted_iota(jnp.int32, (x_ref.shape[0], LANES), 1)
    lf = jnp.where(lane < FOX_HEADS, _log_sigmoid(_dot(h, wf_ref[...]) + bf_ref[...]), 0.0)
    lft_ref[...] = lf.T[:_FL_ROWS, :]


def _fox_proj_sample(x, g, wq, wk, wv, wqm, wf, bf_row, qg_t, kg_t):
    t = x.shape[0]
    rows = jax.ShapeDtypeStruct((t, MIX_W), F32)
    return pl.pallas_call(
        _fox_proj_sample_body,
        out_shape=[rows, rows, rows, jax.ShapeDtypeStruct((_FL_ROWS, t), F32),
                   jax.ShapeDtypeStruct((t, MEM_W), F32)],
        compiler_params=_cparams(vmem=VMEM_LIMIT),
        name="fox_proj_sample",
    )(x, g, wq, wk, wv, wqm, wf, bf_row, qg_t, kg_t)


def _fox_flash_body(qi_ref, ki_ref, qp_ref, kp_ref, vt_ref, o_ref, m_ref, l_ref, acc_ref, *, blk):
    p = pl.program_id(2)
    qi = qi_ref[p]
    ki = ki_ref[p]

    @pl.when(ki == 0)
    def _():
        m_ref[...] = jnp.full(m_ref.shape, -jnp.inf, F32)
        l_ref[...] = jnp.zeros(l_ref.shape, F32)
        acc_ref[...] = jnp.zeros(acc_ref.shape, F32)

    def step(diagonal):
        for hh in range(FLASH_HEADS):
            vt = vt_ref[0, hh * HEAD_DIM:(hh + 1) * HEAD_DIM, :]
            st = _dot_nt(kp_ref[0, hh], qp_ref[0, hh])
            if diagonal:
                key = lax.broadcasted_iota(jnp.int32, st.shape, 0)
                qry = lax.broadcasted_iota(jnp.int32, st.shape, 1)
                st = jnp.where(key <= qry, st, -jnp.inf)
            m_prev = m_ref[hh]
            m_new = jnp.maximum(m_prev, jnp.max(st, axis=0, keepdims=True))
            alpha = jnp.exp2(m_prev - m_new)
            pt = jnp.exp2(st - m_new)
            l_ref[hh] = alpha * l_ref[hh] + jnp.sum(pt, axis=0, keepdims=True)
            acc_ref[hh] = alpha * acc_ref[hh] + _dot(vt, pt.astype(BF16))
            m_ref[hh] = m_new

    @pl.when(ki < qi)
    def _():
        step(False)

    @pl.when(ki == qi)
    def _():
        step(True)
        for hh in range(FLASH_HEADS):
            o_ref[0, hh * HEAD_DIM:(hh + 1) * HEAD_DIM, :] = acc_ref[hh] / l_ref[hh]


def _fox_flash(qp, kp, vt, blk):
    b, _, s, _ = qp.shape
    n = s // blk
    pairs = [(qi, ki) for qi in range(n) for ki in range(qi + 1)]
    qi_tab = jnp.asarray([p[0] for p in pairs], jnp.int32)
    ki_tab = jnp.asarray([p[1] for p in pairs], jnp.int32)
    fh = FLASH_HEADS
    ch = fh * HEAD_DIM
    grid_spec = pltpu.PrefetchScalarGridSpec(
        num_scalar_prefetch=2,
        grid=(b, FOX_HEADS // fh, len(pairs)),
        in_specs=[
            pl.BlockSpec((1, fh, blk, LANES), lambda bi, hg, p, qt, kt: (bi, hg, qt[p], 0)),
            pl.BlockSpec((1, fh, blk, LANES), lambda bi, hg, p, qt, kt: (bi, hg, kt[p], 0)),
            pl.BlockSpec((1, ch, blk), lambda bi, hg, p, qt, kt: (bi, hg, kt[p])),
        ],
        out_specs=pl.BlockSpec((1, ch, blk), lambda bi, hg, p, qt, kt: (bi, hg, qt[p])),
        scratch_shapes=[pltpu.VMEM((fh, 1, blk), F32), pltpu.VMEM((fh, 1, blk), F32),
                        pltpu.VMEM((fh, HEAD_DIM, blk), F32)],
    )
    return pl.pallas_call(
        functools.partial(_fox_flash_body, blk=blk),
        grid_spec=grid_spec,
        out_shape=jax.ShapeDtypeStruct((b, MIX_W, s), F32),
        compiler_params=_cparams("arbitrary", "arbitrary", "arbitrary", vmem=VMEM_LIMIT),
        name="fox_flash",
    )(qi_tab, ki_tab, qp, kp, vt)


_QROWS = FOX_HEADS * SAMPLE_ROWS
_CHUNK = PAGES_PER_STEP * PAGE_SIZE


def _expand_heads(x):
    n = x.shape[1]
    return jnp.broadcast_to(x[:FOX_HEADS, None, :], (FOX_HEADS, SAMPLE_ROWS, n)).reshape(_QROWS, n)


def _page_copies(pt_ref, kt_hbm, vt_hbm, lf_hbm, kt_buf, vt_buf, lf_buf, sem, g, slot, steps, n_pages):
    bi = g // steps
    first = n_pages - (g % steps + 1) * PAGES_PER_STEP
    copies = []
    for i in range(PAGES_PER_STEP):
        page = pt_ref[bi, first + i]
        lanes = pl.ds(i * PAGE_SIZE, PAGE_SIZE)
        copies.append(pltpu.make_async_copy(kt_hbm.at[page], kt_buf.at[slot, :, lanes], sem.at[slot, 0]))
        copies.append(pltpu.make_async_copy(vt_hbm.at[page], vt_buf.at[slot, :, lanes], sem.at[slot, 1]))
        copies.append(pltpu.make_async_copy(lf_hbm.at[:, page], lf_buf.at[slot, i, pl.ds(0, FOX_HEADS)],
                                            sem.at[slot, 2]))
    return copies


def _fox_decode_body(pt_ref, kt_hbm, vt_hbm, lf_hbm, q_ref, kn_ref, vn_ref, lfn_ref, o_ref,
                     kt_buf, vt_buf, lf_buf, sem, qbd_ref, m_ref, l_ref, acc_ref, carry_ref, crow_ref,
                     *, n_new, steps, n_pages):
    g = pl.program_id(0)
    n_steps = pl.num_programs(0)
    j = g % steps
    slot = g % DECODE_SLOTS
    copies = functools.partial(_page_copies, pt_ref, kt_hbm, vt_hbm, lf_hbm, kt_buf, vt_buf, lf_buf, sem,
                               steps=steps, n_pages=n_pages)
    row_l = lax.broadcasted_iota(jnp.int32, (_QROWS, 1), 0) % SAMPLE_ROWS

    @pl.when(g == 0)
    def _():
        lf_buf[...] = jnp.zeros(lf_buf.shape, F32)
        for ahead in range(DECODE_SLOTS - 1):
            for cp in copies(g=ahead, slot=ahead):
                cp.start()

    @pl.when(g + DECODE_SLOTS - 1 < n_steps)
    def _():
        for cp in copies(g=g + DECODE_SLOTS - 1, slot=(g + DECODE_SLOTS - 1) % DECODE_SLOTS):
            cp.start()

    @pl.when(j == 0)
    def _():
        q = q_ref[0] * SCALE
        lh = _lane_head(q.shape, 1)
        for h in range(FOX_HEADS):
            qbd_ref[h * SAMPLE_ROWS:(h + 1) * SAMPLE_ROWS, :] = jnp.where(lh == h, q, 0.0)
        lfn = lfn_ref[0]
        lane = lax.broadcasted_iota(jnp.int32, lfn.shape, 1)
        c = jnp.zeros(lfn.shape, F32)
        for m in range(n_new):
            cm = jnp.sum(jnp.where(lane <= m, lfn, 0.0), axis=1, keepdims=True)
            c = jnp.where(lane == m, cm, c)
        c_q = _expand_heads(c)
        col = lax.broadcasted_iota(jnp.int32, c_q.shape, 1)
        crow = jnp.sum(jnp.where(col == row_l, c_q, 0.0), axis=1, keepdims=True)
        crow_ref[...] = crow
        s = _dot_nt(qbd_ref[...], kn_ref[0]) + (crow - c_q)
        valid = (col < n_new) & ((col <= row_l) | (row_l >= n_new))
        s = jnp.where(valid, s, -jnp.inf)
        m0 = jnp.max(s, axis=-1, keepdims=True)
        p = jnp.exp(s - m0)
        m_ref[...] = m0
        l_ref[...] = jnp.sum(p, axis=-1, keepdims=True)
        acc_ref[...] = _dot(p, vn_ref[0])
        carry_ref[...] = jnp.zeros(carry_ref.shape, F32)

    for cp in copies(g=g, slot=slot):
        cp.wait()

    r = lax.broadcasted_iota(jnp.int32, (PAGE_SIZE, PAGE_SIZE), 0)
    cc = lax.broadcasted_iota(jnp.int32, (PAGE_SIZE, PAGE_SIZE), 1)
    later = (r > cc).astype(BF16)
    lf3 = lf_buf[slot]
    tot = jnp.sum(lf3, axis=2, keepdims=True)
    hi, mid, lo = _split3(lf3.reshape(PAGES_PER_STEP * _FL_ROWS, PAGE_SIZE))
    d_in = (_dot(hi, later) + _dot(mid, later) + _dot(lo, later)).reshape(PAGES_PER_STEP, _FL_ROWS, PAGE_SIZE)
    after = carry_ref[:, 0:1]
    d_pages = [None] * PAGES_PER_STEP
    for i in reversed(range(PAGES_PER_STEP)):
        d_pages[i] = _expand_heads(d_in[i] + after)
        after = after + tot[i]
    carry_ref[...] = jnp.broadcast_to(after, carry_ref.shape)
    d = jnp.concatenate(d_pages, axis=1)

    vt = vt_buf[slot]
    s = _dot(qbd_ref[...], kt_buf[slot]) + crow_ref[...] + d
    m_prev = m_ref[...]
    m_new = jnp.maximum(m_prev, jnp.max(s, axis=-1, keepdims=True))
    alpha = jnp.exp(m_prev - m_new)
    p = jnp.exp(s - m_new)
    l_ref[...] = alpha * l_ref[...] + jnp.sum(p, axis=-1, keepdims=True)
    acc_ref[...] = alpha * acc_ref[...] + _dot_nt(p, vt)
    m_ref[...] = m_new

    @pl.when(j == steps - 1)
    def _():
        res = acc_ref[...] / l_ref[...]
        lh = _lane_head((SAMPLE_ROWS, MIX_W), 1)
        out = jnp.zeros((SAMPLE_ROWS, MIX_W), F32)
        for h in range(FOX_HEADS):
            out = jnp.where(lh == h, res[h * SAMPLE_ROWS:(h + 1) * SAMPLE_ROWS, :], out)
        o_ref[0] = out


def _fox_decode(page_table, kt_pages, vt_pages, lf_heads, q8, kn8, vn8, lfn, n_new):
    b, n_pages = page_table.shape
    steps = n_pages // PAGES_PER_STEP
    per_b = lambda g, pt: (g // steps, 0, 0)
    any_spec = pl.BlockSpec(memory_space=pl.ANY)
    grid_spec = pltpu.PrefetchScalarGridSpec(
        num_scalar_prefetch=1,
        grid=(b * steps,),
        in_specs=[any_spec, any_spec, any_spec,
                  pl.BlockSpec((1, SAMPLE_ROWS, MIX_W), per_b),
                  pl.BlockSpec((1, SAMPLE_ROWS, MIX_W), per_b),
                  pl.BlockSpec((1, SAMPLE_ROWS, MIX_W), per_b),
                  pl.BlockSpec((1, _FL_ROWS, SAMPLE_ROWS), per_b)],
        out_specs=pl.BlockSpec((1, SAMPLE_ROWS, MIX_W), per_b),
        scratch_shapes=[
            pltpu.VMEM((DECODE_SLOTS, MIX_W, _CHUNK), F32),
            pltpu.VMEM((DECODE_SLOTS, MIX_W, _CHUNK), F32),
            pltpu.VMEM((DECODE_SLOTS, PAGES_PER_STEP, _FL_ROWS, PAGE_SIZE), F32),
            pltpu.SemaphoreType.DMA((DECODE_SLOTS, 3)),
            pltpu.VMEM((_QROWS, MIX_W), F32),
            pltpu.VMEM((_QROWS, 1), F32), pltpu.VMEM((_QROWS, 1), F32),
            pltpu.VMEM((_QROWS, MIX_W), F32),
            pltpu.VMEM((_FL_ROWS, LANES), F32),
            pltpu.VMEM((_QROWS, 1), F32),
        ],
    )
    return pl.pallas_call(
        functools.partial(_fox_decode_body, n_new=n_new, steps=steps, n_pages=n_pages),
        grid_spec=grid_spec,
        out_shape=jax.ShapeDtypeStruct((b, SAMPLE_ROWS, MIX_W), F32),
        compiler_params=_cparams("arbitrary", vmem=VMEM_LIMIT),
        name="fox_decode",
    )(page_table, kt_pages, vt_pages, lf_heads, q8, kn8, vn8, lfn)


def _router_body(x_ref, g_ref, wrt_ref, h_ref, r_ref, cnt_ref, carry_ref, *, tm):
    _route_tile(x_ref[...], g_ref, wrt_ref, h_ref, r_ref, cnt_ref, carry_ref, tm)


def _out_proj_router_body(x_ref, mixt_ref, mem_ref, wo_ref, g_ref, wrt_ref,
                          x1_ref, h_ref, r_ref, cnt_ref, carry_ref, *, tm):
    x1 = (x_ref[...] + _dot(mixt_ref[0].T, wo_ref[0, :MIX_W, :])
          + _dot(mem_ref[...], wo_ref[0, MIX_W:, :]))
    x1_ref[...] = x1
    _route_tile(x1, g_ref, wrt_ref, h_ref, r_ref, cnt_ref, carry_ref, tm)


def _route_tile(x, g_ref, wrt_ref, h_ref, r_ref, cnt_ref, carry_ref, tm):
    @pl.when(pl.program_id(0) == 0)
    def _():
        carry_ref[...] = jnp.zeros(carry_ref.shape, F32)

    h = _rms_rows(x, g_ref[...])
    h_ref[...] = h
    h_hi = h.astype(BF16)
    h_lo = (h - h_hi.astype(F32)).astype(BF16)
    w = wrt_ref[...]
    w_hi = w.astype(BF16)
    w_lo = (w - w_hi.astype(F32)).astype(BF16)
    lg = _dot_nt(w_hi, h_hi) + _dot_nt(w_hi, h_lo) + _dot_nt(w_lo, h_hi)
    idx = lax.broadcasted_iota(jnp.int32, lg.shape, 0)
    m1 = jnp.max(lg, axis=0, keepdims=True)
    i1 = jnp.min(jnp.where(lg == m1, idx, N_EXPERTS), axis=0, keepdims=True)
    sel1 = idx == i1
    lg2 = jnp.where(sel1, -jnp.inf, lg)
    m2 = jnp.max(lg2, axis=0, keepdims=True)
    i2 = jnp.min(jnp.where(lg2 == m2, idx, N_EXPERTS), axis=0, keepdims=True)
    sel2 = idx == i2
    e = jnp.exp(m2 - m1)
    g1 = 1.0 / (1.0 + e)
    g2 = e / (1.0 + e)
    assign = jnp.where(sel1 | sel2, 1.0, 0.0)
    r = lax.broadcasted_iota(jnp.int32, (tm, tm), 0)
    c = lax.broadcasted_iota(jnp.int32, (tm, tm), 1)
    before = (r < c).astype(BF16)
    rank = _dot(assign.astype(BF16), before) + carry_ref[:, 0:1]
    r1 = jnp.sum(jnp.where(sel1, rank, 0.0), axis=0, keepdims=True)
    r2 = jnp.sum(jnp.where(sel2, rank, 0.0), axis=0, keepdims=True)
    carry = carry_ref[...] + jnp.sum(assign, axis=1, keepdims=True)
    carry_ref[...] = carry
    cnt_ref[...] = carry
    rows = [i1.astype(F32), i2.astype(F32), r1, r2, g1, g2]
    out = jnp.zeros(lg.shape, F32)
    for k, v in enumerate(rows):
        out = jnp.where(idx == k, v, out)
    r_ref[...] = out


def _router(x, g, wrt, tm):
    t = x.shape[0]
    return pl.pallas_call(
        functools.partial(_router_body, tm=tm),
        grid=(t // tm,),
        in_specs=[
            pl.BlockSpec((tm, D_MODEL), lambda i: (i, 0)),
            pl.BlockSpec((1, D_MODEL), lambda i: (0, 0)),
            pl.BlockSpec((N_EXPERTS, D_MODEL), lambda i: (0, 0)),
        ],
        out_specs=[
            pl.BlockSpec((tm, D_MODEL), lambda i: (i, 0)),
            pl.BlockSpec((N_EXPERTS, tm), lambda i: (0, i)),
            pl.BlockSpec((N_EXPERTS, LANES), lambda i: (0, 0)),
        ],
        out_shape=[
            jax.ShapeDtypeStruct((t, D_MODEL), F32),
            jax.ShapeDtypeStruct((N_EXPERTS, t), F32),
            jax.ShapeDtypeStruct((N_EXPERTS, LANES), F32),
        ],
        scratch_shapes=[pltpu.VMEM((N_EXPERTS, LANES), F32)],
        compiler_params=_cparams("arbitrary", vmem=VMEM_LIMIT),
        name="router",
    )(x, g, wrt)


def _out_proj_router(x, mix_t, mem, w_layers, layer, g, wrt, tm):
    t = x.shape[0]
    per_seq = mix_t.shape[2] // tm
    return pl.pallas_call(
        functools.partial(_out_proj_router_body, tm=tm),
        grid=(t // tm,),
        in_specs=[
            pl.BlockSpec((tm, D_MODEL), lambda i: (i, 0)),
            pl.BlockSpec((1, MIX_W, tm), lambda i: (i // per_seq, 0, i % per_seq)),
            pl.BlockSpec((tm, MEM_W), lambda i: (i, 0)),
            pl.BlockSpec((1, D_MODEL, D_MODEL), lambda i: (layer, 0, 0)),
            pl.BlockSpec((1, D_MODEL), lambda i: (0, 0)),
            pl.BlockSpec((N_EXPERTS, D_MODEL), lambda i: (0, 0)),
        ],
        out_specs=[
            pl.BlockSpec((tm, D_MODEL), lambda i: (i, 0)),
            pl.BlockSpec((tm, D_MODEL), lambda i: (i, 0)),
            pl.BlockSpec((N_EXPERTS, tm), lambda i: (0, i)),
            pl.BlockSpec((N_EXPERTS, LANES), lambda i: (0, 0)),
        ],
        out_shape=[
            jax.ShapeDtypeStruct((t, D_MODEL), F32),
            jax.ShapeDtypeStruct((t, D_MODEL), F32),
            jax.ShapeDtypeStruct((N_EXPERTS, t), F32),
            jax.ShapeDtypeStruct((N_EXPERTS, LANES), F32),
        ],
        scratch_shapes=[pltpu.VMEM((N_EXPERTS, LANES), F32)],
        compiler_params=_cparams("arbitrary", vmem=VMEM_LIMIT),
        name="out_proj_router",
    )(x, mix_t, mem, w_layers, g, wrt)


_SCATTER_TM = 512


def _row_copy(src_ref, src_row, dst_ref, dst_row, sem):
    return pltpu.make_async_copy(src_ref.at[pl.ds(src_row, 1)], dst_ref.at[pl.ds(dst_row, 1)], sem)


def _scatter_body(d1_ref, d2_ref, hp_ref, hs_ref, xs_ref, sem, *, n_prompt_tiles):
    i = pl.program_id(0)

    def run(src_ref):
        n_rows = src_ref.shape[0]

        def start(r, carry):
            _row_copy(src_ref, r, xs_ref, d1_ref[0, 0, r], sem.at[0]).start(priority=0)
            _row_copy(src_ref, r, xs_ref, d2_ref[0, 0, r], sem.at[1]).start(priority=1)
            return carry

        lax.fori_loop(0, n_rows, start, 0, unroll=DMA_UNROLL)
        for k in range(2):
            pltpu.make_async_copy(src_ref, xs_ref.at[pl.ds(0, n_rows)], sem.at[k]).wait()

    @pl.when(i < n_prompt_tiles)
    def _():
        run(hp_ref)

    @pl.when(i >= n_prompt_tiles)
    def _():
        run(hs_ref)


def _scatter_rows(d1p, d2p, d1s, d2s, h_p, h_s, n_sorted):
    tm = _SCATTER_TM
    npt = h_p.shape[0] // tm
    n_s = h_s.shape[0]
    pad = jnp.zeros((tm - n_s,), jnp.int32)
    d1 = jnp.concatenate([d1p, d1s, pad])
    d2 = jnp.concatenate([d2p, d2s, pad])
    smem = lambda: pl.BlockSpec((1, 1, tm), lambda i: (i, 0, 0), memory_space=pltpu.SMEM)
    return pl.pallas_call(
        functools.partial(_scatter_body, n_prompt_tiles=npt),
        grid=(npt + 1,),
        in_specs=[
            smem(), smem(),
            pl.BlockSpec((tm, D_MODEL), lambda i: (jnp.minimum(i, npt - 1), 0)),
            pl.BlockSpec((n_s, D_MODEL), lambda i: (0, 0)),
        ],
        out_specs=pl.BlockSpec(memory_space=pl.ANY),
        out_shape=jax.ShapeDtypeStruct((n_sorted, D_MODEL), F32),
        scratch_shapes=[pltpu.SemaphoreType.DMA((2,))],
        compiler_params=_cparams("arbitrary", vmem=VMEM_LIMIT),
        name="moe_scatter",
    )(d1.reshape(-1, 1, tm), d2.reshape(-1, 1, tm), h_p, h_s)


def _gmm_body(tile_ref, exp_ref, lo_ref, hi_ref, first_ref, x_ref, wg_ref, wu_ref, wd_ref, o_ref):
    del tile_ref, exp_ref
    v = pl.program_id(0)
    c = pl.program_id(1)

    @pl.when((first_ref[v] == 1) & (c == 0))
    def _():
        o_ref[...] = jnp.zeros(o_ref.shape, F32)

    lo = lo_ref[v]
    hi = hi_ref[v]
    whole = (lo == 0) & (hi == MOE_TM)

    def swiglu_into(rows, mask):
        x = x_ref[rows, :]
        for off, n in _MOE_SUBS:
            cols = slice(off, off + n)
            a = _silu(_dot(x, wg_ref[0, :, cols])) * _dot(x, wu_ref[0, :, cols])
            y = _dot(a, wd_ref[0, cols, :])
            o_ref[rows, :] += y if mask is None else jnp.where(mask, y, 0.0)

    @pl.when(whole)
    def _():
        swiglu_into(slice(None), None)

    @pl.when(jnp.logical_not(whole) & (hi > lo))
    def _():
        for r0 in range(0, MOE_TM, MOE_SUB_ROWS):
            @pl.when((lo < r0 + MOE_SUB_ROWS) & (hi > r0))
            def _():
                rows = r0 + lax.broadcasted_iota(jnp.int32, (MOE_SUB_ROWS, 1), 0)
                swiglu_into(slice(r0, r0 + MOE_SUB_ROWS), (rows >= lo) & (rows < hi))


_MOE_FF = D_FF // MOE_FF_SPLIT
_MOE_SUBS = tuple((off, min(MXU_N, _MOE_FF - off)) for off in range(0, _MOE_FF, MXU_N))


def _gmm(meta, xs, w_gu, w_down):
    tile, expert, lo, hi, first = meta
    n_visits = tile.shape[0]
    grid_spec = pltpu.PrefetchScalarGridSpec(
        num_scalar_prefetch=5,
        grid=(n_visits, MOE_FF_SPLIT),
        in_specs=[
            pl.BlockSpec((MOE_TM, D_MODEL), lambda v, c, t, e, *_: (t[v], 0)),
            pl.BlockSpec((1, D_MODEL, _MOE_FF), lambda v, c, t, e, *_: (e[v], 0, c)),
            pl.BlockSpec((1, D_MODEL, _MOE_FF), lambda v, c, t, e, *_: (e[v], 0, MOE_FF_SPLIT + c)),
            pl.BlockSpec((1, _MOE_FF, D_MODEL), lambda v, c, t, e, *_: (e[v], c, 0)),
        ],
        out_specs=pl.BlockSpec((MOE_TM, D_MODEL), lambda v, c, t, e, *_: (t[v], 0)),
    )
    return pl.pallas_call(
        _gmm_body,
        grid_spec=grid_spec,
        out_shape=jax.ShapeDtypeStruct(xs.shape, F32),
        compiler_params=_cparams("arbitrary", "arbitrary", vmem=VMEM_LIMIT),
        name="moe_gmm",
    )(tile, expert, lo, hi, first, xs, w_gu, w_gu, w_down)


def _gmm_meta(counts, n_sorted):
    n_tiles = n_sorted // MOE_TM
    n_visits = n_tiles + N_EXPERTS - 1
    ends = jnp.cumsum(counts)
    starts = ends - counts
    first_tile = starts // MOE_TM
    last_tile = jnp.maximum(ends - 1, 0) // MOE_TM
    nv = jnp.where(counts > 0, last_tile - first_tile + 1, 0)
    cv = jnp.cumsum(nv)
    v = jnp.arange(n_visits, dtype=jnp.int32)
    total = cv[-1]
    valid = v < total
    vc = jnp.minimum(v, total - 1)
    expert = jnp.sum((cv[None, :] <= vc[:, None]).astype(jnp.int32), axis=1)
    sel = (expert[:, None] == jnp.arange(N_EXPERTS, dtype=jnp.int32)[None, :]).astype(jnp.int32)
    of_expert = lambda a: jnp.sum(sel * a[None, :].astype(jnp.int32), axis=1)
    tile = of_expert(first_tile) + (vc - (of_expert(cv) - of_expert(nv)))
    lo = jnp.maximum(of_expert(starts), tile * MOE_TM) - tile * MOE_TM
    hi = jnp.minimum(of_expert(ends), (tile + 1) * MOE_TM) - tile * MOE_TM
    lo = jnp.where(valid, lo, 0)
    hi = jnp.where(valid, hi, 0)
    prev_tile = jnp.concatenate([jnp.full((1,), -1, jnp.int32), tile[:-1]])
    first = (valid & (tile != prev_tile)).astype(jnp.int32)
    as_i32 = lambda a: a.astype(jnp.int32)
    return as_i32(tile), as_i32(expert), as_i32(lo), as_i32(hi), first


def _combine_body(d1_ref, d2_ref, x_ref, g_ref, os_ref, y_ref, buf_ref, sem, *, tm, n_tiles):
    i = pl.program_id(0)
    slot = i % 2

    def issue(tile, slot_):
        def body(r, carry):
            _row_copy(os_ref, d1_ref[tile, r], buf_ref.at[slot_, 0], r, sem.at[slot_, 0]).start(priority=0)
            _row_copy(os_ref, d2_ref[tile, r], buf_ref.at[slot_, 1], r, sem.at[slot_, 1]).start(priority=1)
            return carry
        lax.fori_loop(0, tm, body, 0, unroll=DMA_UNROLL)

    @pl.when(i == 0)
    def _():
        issue(0, 0)

    @pl.when(i + 1 < n_tiles)
    def _():
        issue(i + 1, 1 - slot)

    for k in range(2):
        pltpu.make_async_copy(os_ref.at[pl.ds(0, tm)], buf_ref.at[slot, k], sem.at[slot, k]).wait()

    g = g_ref[...]
    y_ref[...] = x_ref[...] + g[:, 0:1] * buf_ref[slot, 0] + g[:, 1:2] * buf_ref[slot, 1]


def _combine(d1, d2, x, gates, o_sorted, tm):
    t = x.shape[0]
    n_tiles = t // tm
    grid_spec = pltpu.PrefetchScalarGridSpec(
        num_scalar_prefetch=2,
        grid=(n_tiles,),
        in_specs=[
            pl.BlockSpec((tm, D_MODEL), lambda i, *_: (i, 0)),
            pl.BlockSpec((tm, 2), lambda i, *_: (i, 0)),
            pl.BlockSpec(memory_space=pl.ANY),
        ],
        out_specs=pl.BlockSpec((tm, D_MODEL), lambda i, *_: (i, 0)),
        scratch_shapes=[pltpu.VMEM((2, 2, tm, D_MODEL), F32), pltpu.SemaphoreType.DMA((2, 2))],
    )
    return pl.pallas_call(
        functools.partial(_combine_body, tm=tm, n_tiles=n_tiles),
        grid_spec=grid_spec,
        out_shape=jax.ShapeDtypeStruct((t, D_MODEL), F32),
        compiler_params=_cparams("arbitrary", vmem=VMEM_LIMIT),
        name="moe_combine",
    )(d1.reshape(n_tiles, tm), d2.reshape(n_tiles, tm), x, gates, o_sorted)


def _block_diag(w_grp):
    g, n, _ = w_grp.shape
    out = jnp.zeros((g * n, g * n), w_grp.dtype)
    for i in range(g):
        out = out.at[i * n:(i + 1) * n, i * n:(i + 1) * n].set(w_grp[i])
    return out


def kernel(x_prompt, x_sample, cache_mem_k, cache_mem_v, state_pool, cache_fox_k, cache_fox_v, cache_fox_logf, page_table, mem_prompt, norm_mix_g, norm_ffn_g, norm_mem_g, w_mem_kv, mem_q_norm_g, mem_k_norm_g, w_out, w_in_pool, w_pool_group, pool_scale, w_in_fox, b_forget, fox_q_norm_g, fox_k_norm_g, w_ffn_gu, w_ffn_down, w_router, w_exp_gu, w_exp_down):
    b, s, d = x_prompt.shape
    bs, ls, _ = x_sample.shape
    tp, ts = b * s, bs * ls
    tm_p, tm_s = PROMPT_TM, ts

    xp = x_prompt.reshape(tp, d)
    xs = x_sample.reshape(ts, d)

    mem_kt_p, mem_vt_p = _memkv(mem_prompt, norm_mem_g, w_mem_kv, mem_k_norm_g)
    mem_kt_s = cache_mem_k.transpose(0, 1, 3, 4, 2).reshape(2, bs, MEM_W, MEM_LEN)
    mem_vt_s = cache_mem_v.transpose(0, 1, 3, 4, 2).reshape(2, bs, MEM_W, MEM_LEN)

    def mem_attend_sample(qm_s, layer):
        qs8 = jnp.pad(qm_s.reshape(bs, ls, MEM_W), ((0, 0), (0, SAMPLE_ROWS - ls), (0, 0)))
        mo_s = _mem_attend(qs8, mem_kt_s, mem_vt_s, layer, mem_q_norm_g[layer], SAMPLE_ROWS, MEM_SEQS_PER_STEP)
        return mo_s[:, :ls].reshape(ts, MEM_W)

    g_mix0 = norm_mix_g[0][None, :]
    w_bd = _block_diag(w_pool_group[0])
    pscale = pool_scale[0][None, :]
    xp, up_tail = _pool_layer_prompt(x_prompt, g_mix0, w_in_pool[0], w_bd, pscale, mem_kt_p[0], mem_vt_p[0],
                                     mem_q_norm_g[0], w_out[0], tm_p)
    xp = xp.reshape(tp, d)
    pool_p = up_tail[:, _POOL_HALO - POOL_STATE:][None]
    us, qms = _pool_proj(xs, g_mix0, w_in_pool[0], tm_s)
    us_t = us.reshape(bs, ls, MIX_W).transpose(1, 0, 2)
    mix_s_t, new_state_t = _pool_sample(state_pool[0].transpose(1, 0, 2), us_t, w_bd, pscale)
    mix_s = mix_s_t.transpose(1, 0, 2).reshape(ts, MIX_W)
    pool_s = new_state_t.transpose(1, 0, 2)[None]
    xs = _out_proj(xs, mix_s, mem_attend_sample(qms, 0), w_out, 0, tm_s)
    g_ffn0 = norm_ffn_g[0][None, :]
    xp = _ffn(xp, g_ffn0, w_ffn_gu[0], w_ffn_down[0], tm_p)
    xs = _ffn(xs, g_ffn0, w_ffn_gu[0], w_ffn_down[0], tm_s)

    w_in = w_in_fox[0]
    wq = w_in[:, :MIX_W]
    wk = w_in[:, MIX_W:2 * MIX_W]
    wv = w_in[:, 2 * MIX_W:3 * MIX_W]
    wf = jnp.pad(w_in[:, 3 * MIX_W:3 * MIX_W + FOX_HEADS], ((0, 0), (0, LANES - FOX_HEADS)))
    wqm = w_in[:, 3 * MIX_W + FOX_HEADS:]
    bf_row = jnp.pad(b_forget[0], (0, LANES - FOX_HEADS))[None, :]
    qg_t = jnp.tile(fox_q_norm_g[0], FOX_HEADS)[None, :]
    kg_t = jnp.tile(fox_k_norm_g[0], FOX_HEADS)[None, :]
    g_mix1 = norm_mix_g[1][None, :]
    qpp, kpp, ktp, vtp, vtb, lftp, qmp = _fox_proj_prompt(
        xp, g_mix1, wq.T, wk.T, wv.T, wqm, wf, bf_row, fox_q_norm_g[0][:, None], fox_k_norm_g[0][:, None],
        tm=tm_p, n_seq=b)
    qs, kns, vns, lfts, qms = _fox_proj_sample(xs, g_mix1, wq, wk, wv, wqm, wf, bf_row, qg_t, kg_t)

    mix_pt = _fox_flash(qpp, kpp, vtb, FLASH_BLK)

    pad_rows = ((0, 0), (0, SAMPLE_ROWS - ls), (0, 0))
    q8 = jnp.pad(qs.reshape(bs, ls, MIX_W), pad_rows)
    kn8 = jnp.pad(kns.reshape(bs, ls, MIX_W), pad_rows)
    vn8 = jnp.pad(vns.reshape(bs, ls, MIX_W), pad_rows)
    lfn = jnp.pad(lfts.reshape(_FL_ROWS, bs, ls).transpose(1, 0, 2),
                  ((0, 0), (0, 0), (0, SAMPLE_ROWS - ls)))
    n_phys = cache_fox_k.shape[1]
    kt_pages = cache_fox_k[0].transpose(0, 2, 3, 1).reshape(n_phys, MIX_W, PAGE_SIZE)
    vt_pages = cache_fox_v[0].transpose(0, 2, 3, 1).reshape(n_phys, MIX_W, PAGE_SIZE)
    lf_heads = cache_fox_logf[0].transpose(2, 0, 1)
    mix_s = _fox_decode(page_table, kt_pages, vt_pages, lf_heads, q8, kn8, vn8, lfn, ls)
    mix_s = mix_s[:, :ls].reshape(ts, MIX_W)

    mo_p = _mem_attend(qmp.reshape(b, s, MEM_W), mem_kt_p, mem_vt_p, 1, mem_q_norm_g[1],
                       tm_p, 1).reshape(tp, MEM_W)
    xs = _out_proj(xs, mix_s, mem_attend_sample(qms, 1), w_out, 1, tm_s)

    g_ffn1 = norm_ffn_g[1][None, :]
    wrt = w_router[0].T
    xp, hp, rp, cnt_p = _out_proj_router(xp, mix_pt, mo_p, w_out, 1, g_ffn1, wrt, tm_p)
    hs, rs, cnt_s = _router(xs, g_ffn1, wrt, tm_s)
    cnt_p = cnt_p[:, 0].astype(jnp.int32)
    cnt_s = cnt_s[:, 0].astype(jnp.int32)
    counts = cnt_p + cnt_s
    offsets = jnp.cumsum(counts) - counts
    n_sorted = 2 * (tp + ts)

    def dests(r, base):
        i1, i2 = r[0].astype(jnp.int32), r[1].astype(jnp.int32)
        return base[i1] + r[2].astype(jnp.int32), base[i2] + r[3].astype(jnp.int32)

    d1p, d2p = dests(rp, offsets)
    d1s, d2s = dests(rs, offsets + cnt_p)
    x_sorted = _scatter_rows(d1p, d2p, d1s, d2s, hp, hs, n_sorted)
    o_sorted = _gmm(_gmm_meta(counts, n_sorted), x_sorted, w_exp_gu[0], w_exp_down[0])
    yp = _combine(d1p, d2p, xp, rp[4:6].T, o_sorted, tm_p)
    ys = _combine(d1s, d2s, xs, rs[4:6].T, o_sorted, ts)

    def heads_t(a_t, n_b, n_h):
        return a_t.reshape(n_b, n_h, HEAD_DIM, a_t.shape[-1]).transpose(0, 3, 1, 2)

    mem_k_p = jnp.stack([heads_t(mem_kt_p[i], b, MEM_HEADS) for i in range(2)])
    mem_v_p = jnp.stack([heads_t(mem_vt_p[i], b, MEM_HEADS) for i in range(2)])
    fk_p = heads_t(ktp, b, FOX_HEADS)[None]
    fv_p = heads_t(vtp, b, FOX_HEADS)[None]
    fl_p = lftp[:FOX_HEADS].reshape(FOX_HEADS, b, s).transpose(1, 2, 0)[None]
    fk_s = kns.reshape(1, bs, ls, FOX_HEADS, HEAD_DIM)
    fv_s = vns.reshape(1, bs, ls, FOX_HEADS, HEAD_DIM)
    fl_s = lfts[:FOX_HEADS].reshape(FOX_HEADS, bs, ls).transpose(1, 2, 0)[None]
    return (yp.reshape(b, s, d), ys.reshape(bs, ls, d), mem_k_p, mem_v_p, pool_p, pool_s,
            fk_p, fv_p, fl_p, fk_s, fv_s, fl_s)
```

```python
import functools
import math

import jax
import jax.numpy as jnp
from jax import lax
from jax.experimental import pallas as pl
from jax.experimental.pallas import tpu as pltpu

F32 = jnp.float32
BF16 = jnp.bfloat16

D_MODEL = 1024
HEAD_DIM = 64
MEM_LEN = 256
MEM_HEADS = 4
MEM_W = MEM_HEADS * HEAD_DIM
MIX_W = D_MODEL - MEM_W
FOX_HEADS = MIX_W // HEAD_DIM
POOL_WINDOWS = (2, 4, 8, 16)
POOL_GW = MIX_W // len(POOL_WINDOWS)
POOL_STATE = max(POOL_WINDOWS) - 1
D_FF = 2816
N_EXPERTS = 8
PAGE_SIZE = 128
EPS = 1e-6
SCALE = HEAD_DIM ** -0.5
LOG2E = math.log2(math.e)

LANES = 128
SUBLANES = 8
MXU_N = 256
VMEM_LIMIT = 56 * 1024 * 1024

FF_CHUNK = MXU_N
N_FF_CHUNKS = D_FF // FF_CHUNK
MOE_TM = 1032
PAGES_PER_STEP = 8
SAMPLE_ROWS = 8
DMA_UNROLL = 8
PROMPT_TM = 512
FLASH_BLK = 1024
MEM_SEQS_PER_STEP = 4
DECODE_SLOTS = 3
FLASH_HEADS = 12
MOE_FF_SPLIT = 2
MOE_SUB_ROWS = 344


def _cparams(*sem, vmem=None):
    return pltpu.CompilerParams(dimension_semantics=sem, vmem_limit_bytes=vmem)


def _dot(a, b):
    return jnp.dot(a, b, preferred_element_type=F32)


def _dot_nt(a, b):
    return lax.dot_general(a, b, (((1,), (1,)), ((), ())), preferred_element_type=F32)


def _rms_rows(x, g):
    return x * lax.rsqrt(jnp.mean(x * x, axis=-1, keepdims=True) + EPS) * g


def _split3(x):
    hi = x.astype(BF16)
    r = x - hi.astype(F32)
    mid = r.astype(BF16)
    lo = (r - mid.astype(F32)).astype(BF16)
    return hi, mid, lo


def _head_sumsq_lanes(x):
    r = lax.broadcasted_iota(jnp.int32, (LANES, LANES), 0) // HEAD_DIM
    c = lax.broadcasted_iota(jnp.int32, (LANES, LANES), 1) // HEAD_DIM
    ones_bd = (r == c).astype(BF16)
    xx = x * x
    hi = xx.astype(BF16)
    lo = (xx - hi.astype(F32)).astype(BF16)
    parts = []
    for j in range(x.shape[1] // LANES):
        sl = slice(j * LANES, (j + 1) * LANES)
        parts.append(_dot(hi[:, sl], ones_bd) + _dot(lo[:, sl], ones_bd))
    return jnp.concatenate(parts, axis=1)


def _head_rms_lanes(x, g_tiled):
    ssq = _head_sumsq_lanes(x)
    return x * lax.rsqrt(ssq * (1.0 / HEAD_DIM) + EPS) * g_tiled


def _head_rms_rows_t(xt, g_col, n_heads):
    outs = []
    for h in range(n_heads):
        blk = xt[h * HEAD_DIM:(h + 1) * HEAD_DIM, :]
        ms = jnp.mean(blk * blk, axis=0, keepdims=True)
        outs.append(blk * lax.rsqrt(ms + EPS) * g_col)
    return outs


def _log_sigmoid(x):
    return jnp.minimum(x, 0.0) - jnp.log1p(jnp.exp(-jnp.abs(x)))


def _lane_head(shape, axis):
    return lax.broadcasted_iota(jnp.int32, shape, axis) // HEAD_DIM


def _memkv_body(mem_ref, gm_ref, w_ref, kg_ref, kt_ref, vt_ref):
    h = _rms_rows(mem_ref[0], gm_ref[0])
    z = _dot(h, w_ref[0])
    kt = z[:, :MEM_W].T
    vt_ref[0, 0] = z[:, MEM_W:].T
    pieces = _head_rms_rows_t(kt, kg_ref[0], MEM_HEADS)
    for h_i, p in enumerate(pieces):
        kt_ref[0, 0, h_i * HEAD_DIM:(h_i + 1) * HEAD_DIM, :] = p


def _memkv(mem, g_mem, w_kv, kn_g):
    depth, batch = w_kv.shape[0], mem.shape[0]
    out = jax.ShapeDtypeStruct((depth, batch, MEM_W, MEM_LEN), F32)
    return pl.pallas_call(
        _memkv_body,
        grid=(depth, batch),
        in_specs=[
            pl.BlockSpec((1, MEM_LEN, D_MODEL), lambda i, b: (b, 0, 0)),
            pl.BlockSpec((1, 1, D_MODEL), lambda i, b: (i, 0, 0)),
            pl.BlockSpec((1, D_MODEL, 2 * MEM_W), lambda i, b: (i, 0, 0)),
            pl.BlockSpec((1, HEAD_DIM, 1), lambda i, b: (i, 0, 0)),
        ],
        out_specs=[pl.BlockSpec((1, 1, MEM_W, MEM_LEN), lambda i, b: (i, b, 0, 0))] * 2,
        out_shape=[out, out],
        compiler_params=_cparams("arbitrary", "arbitrary"),
        name="memkv",
    )(mem, g_mem[:, None, :], w_kv, kn_g[:, :, None])


def _pool_proj_body(x_ref, g_ref, w_ref, u_ref, qm_ref):
    h = _rms_rows(x_ref[...], g_ref[...])
    u_ref[...] = _dot(h, w_ref[:, :MIX_W])
    qm_ref[...] = _dot(h, w_ref[:, MIX_W:])


def _pool_proj(x, g, w, tm):
    t = x.shape[0]
    return pl.pallas_call(
        _pool_proj_body,
        grid=(t // tm,),
        in_specs=[
            pl.BlockSpec((tm, D_MODEL), lambda i: (i, 0)),
            pl.BlockSpec((1, D_MODEL), lambda i: (0, 0)),
            pl.BlockSpec((D_MODEL, D_MODEL), lambda i: (0, 0)),
        ],
        out_specs=[pl.BlockSpec((tm, MIX_W), lambda i: (i, 0)),
                   pl.BlockSpec((tm, MEM_W), lambda i: (i, 0))],
        out_shape=[jax.ShapeDtypeStruct((t, MIX_W), F32), jax.ShapeDtypeStruct((t, MEM_W), F32)],
        compiler_params=_cparams("arbitrary", vmem=VMEM_LIMIT),
        name="pool_proj",
    )(x, g, w)


def _window_of_lane(shape):
    lane = lax.broadcasted_iota(jnp.int32, shape, len(shape) - 1)
    return jnp.where(lane < POOL_GW, POOL_WINDOWS[0],
                     jnp.where(lane < 2 * POOL_GW, POOL_WINDOWS[1],
                               jnp.where(lane < 3 * POOL_GW, POOL_WINDOWS[2], POOL_WINDOWS[3])))


def _pool_select(s2, s4, s8, s16, shape):
    lane = lax.broadcasted_iota(jnp.int32, shape, len(shape) - 1)
    return jnp.where(lane < POOL_GW, s2,
                     jnp.where(lane < 2 * POOL_GW, s4, jnp.where(lane < 3 * POOL_GW, s8, s16)))


_POOL_PAD = SUBLANES
_POOL_HALO = 2 * SUBLANES
_POOL_BASE = _POOL_PAD + _POOL_HALO


def _pool_mix_tile(u, li, tl, e_ref, s2_ref, s4_ref, s8_ref, w_bd, scale):
    n = _POOL_HALO + tl

    @pl.when(li == 0)
    def _():
        e_ref[0:_POOL_BASE, :] = jnp.zeros((_POOL_BASE, MIX_W), F32)
        s2_ref[0:_POOL_PAD, :] = jnp.zeros((_POOL_PAD, MIX_W), F32)
        s4_ref[0:_POOL_PAD, :] = jnp.zeros((_POOL_PAD, MIX_W), F32)
        s8_ref[0:_POOL_PAD, :] = jnp.zeros((_POOL_PAD, MIX_W), F32)

    e_ref[_POOL_BASE:_POOL_BASE + tl, :] = u
    s2 = e_ref[_POOL_PAD:_POOL_PAD + n, :] + e_ref[_POOL_PAD - 1:_POOL_PAD - 1 + n, :]
    s2_ref[_POOL_PAD:_POOL_PAD + n, :] = s2
    s4 = s2 + s2_ref[_POOL_PAD - 2:_POOL_PAD - 2 + n, :]
    s4_ref[_POOL_PAD:_POOL_PAD + n, :] = s4
    s8 = s4 + s4_ref[_POOL_PAD - 4:_POOL_PAD - 4 + n, :]
    s8_ref[_POOL_PAD:_POOL_PAD + n, :] = s8
    s16 = s8[_POOL_HALO:, :] + s8_ref[_POOL_BASE - 8:_POOL_BASE - 8 + tl, :]
    shape = (tl, MIX_W)
    ssel = _pool_select(s2[_POOL_HALO:, :], s4[_POOL_HALO:, :], s8[_POOL_HALO:, :], s16, shape)
    pos = li * tl + lax.broadcasted_iota(jnp.int32, shape, 0)
    cnt = jnp.minimum(_window_of_lane(shape), pos + 1).astype(F32)
    diff = ssel / cnt - u
    mix = _dot(diff, w_bd) * scale
    e_ref[_POOL_PAD:_POOL_BASE, :] = e_ref[_POOL_PAD + tl:_POOL_BASE + tl, :]
    return mix


def _pool_layer_prompt_body(x_ref, g_ref, win_ref, wbd_ref, sc_ref, kt_ref, vt_ref, qg_ref, wo_ref,
                            o_ref, st_ref, e_ref, s2_ref, s4_ref, s8_ref, *, tl):
    x = x_ref[0]
    h = _rms_rows(x, g_ref[...])
    u = _dot(h, win_ref[:, :MIX_W])
    qm = _dot(h, win_ref[:, MIX_W:])
    st_ref[0] = u[tl - _POOL_HALO:, :]
    mix = _pool_mix_tile(u, pl.program_id(1), tl, e_ref, s2_ref, s4_ref, s8_ref, wbd_ref[...], sc_ref[...])
    mem = _mem_attend_tile(qm, kt_ref[0], vt_ref[0], qg_ref[...])
    o_ref[0] = x + _dot(mix, wo_ref[:MIX_W, :]) + _dot(mem, wo_ref[MIX_W:, :])


def _pool_layer_prompt(x, g, w_in, w_bd, scale, mem_kt, mem_vt, qn_g, w_out, tl):
    b, s, _ = x.shape
    rows = _POOL_BASE + tl
    const = lambda shape: pl.BlockSpec(shape, lambda bi, li: tuple(0 for _ in shape))
    per_b = lambda shape: pl.BlockSpec(shape, lambda bi, li: (bi, 0, 0))
    return pl.pallas_call(
        functools.partial(_pool_layer_prompt_body, tl=tl),
        grid=(b, s // tl),
        in_specs=[
            pl.BlockSpec((1, tl, D_MODEL), lambda bi, li: (bi, li, 0)),
            const((1, D_MODEL)), const((D_MODEL, D_MODEL)), const((MIX_W, MIX_W)), const((1, MIX_W)),
            per_b((1, MEM_W, MEM_LEN)), per_b((1, MEM_W, MEM_LEN)), const((1, MEM_W)),
            const((D_MODEL, D_MODEL)),
        ],
        out_specs=[pl.BlockSpec((1, tl, D_MODEL), lambda bi, li: (bi, li, 0)),
                   per_b((1, _POOL_HALO, MIX_W))],
        out_shape=[jax.ShapeDtypeStruct((b, s, D_MODEL), F32),
                   jax.ShapeDtypeStruct((b, _POOL_HALO, MIX_W), F32)],
        scratch_shapes=[pltpu.VMEM((rows, MIX_W), F32)] * 4,
        compiler_params=_cparams("arbitrary", "arbitrary", vmem=VMEM_LIMIT),
        name="pool_layer_prompt",
    )(x, g, w_in, w_bd, scale, mem_kt, mem_vt, jnp.tile(qn_g, MEM_HEADS)[None, :], w_out)


def _pool_sample_body(st_ref, u_ref, w_ref, sc_ref, o_ref, ns_ref, *, n_new):
    def ext(j):
        return st_ref[j] if j < POOL_STATE else u_ref[j - POOL_STATE]

    for l in range(n_new):
        r = POOL_STATE + l
        s2 = ext(r) + ext(r - 1)
        s4 = s2 + ext(r - 2) + ext(r - 3)
        s8 = s4
        for j in range(4, 8):
            s8 = s8 + ext(r - j)
        s16 = s8
        for j in range(8, 16):
            s16 = s16 + ext(r - j)
        shape = s2.shape
        cnt = _window_of_lane(shape).astype(F32)
        diff = _pool_select(s2, s4, s8, s16, shape) / cnt - ext(r)
        o_ref[l] = _dot(diff, w_ref[...]) * sc_ref[...]
    for j in range(POOL_STATE):
        ns_ref[j] = ext(j + n_new)


def _pool_sample(state_t, u_t, w_bd, scale):
    n_new, b, _ = u_t.shape
    return pl.pallas_call(
        functools.partial(_pool_sample_body, n_new=n_new),
        out_shape=[jax.ShapeDtypeStruct((n_new, b, MIX_W), F32),
                   jax.ShapeDtypeStruct((POOL_STATE, b, MIX_W), F32)],
        name="pool_sample",
    )(state_t, u_t, w_bd, scale)


def _mem_attend_tile(qm, kt, vt, g_tiled):
    q = _head_rms_lanes(qm, g_tiled) * SCALE
    lh = _lane_head(q.shape, 1)
    out = jnp.zeros(q.shape, F32)
    for h in range(MEM_HEADS):
        s = _dot(jnp.where(lh == h, q, 0.0), kt)
        e = jnp.exp(s - jnp.max(s, axis=-1, keepdims=True))
        p = e / jnp.sum(e, axis=-1, keepdims=True)
        out = jnp.where(lh == h, _dot_nt(p, vt), out)
    return out


def _mem_attend_body(q_ref, kt_ref, vt_ref, g_ref, o_ref):
    for i in range(q_ref.shape[0]):
        o_ref[i] = _mem_attend_tile(q_ref[i], kt_ref[0, i], vt_ref[0, i], g_ref[...])


def _mem_attend(qm, kt_layers, vt_layers, layer, qn_g, tl, bb):
    b, l, _ = qm.shape
    kv_spec = pl.BlockSpec((1, bb, MEM_W, MEM_LEN), lambda bi, li: (layer, bi, 0, 0))
    return pl.pallas_call(
        _mem_attend_body,
        grid=(b // bb, l // tl),
        in_specs=[
            pl.BlockSpec((bb, tl, MEM_W), lambda bi, li: (bi, li, 0)),
            kv_spec, kv_spec,
            pl.BlockSpec((1, MEM_W), lambda bi, li: (0, 0)),
        ],
        out_specs=pl.BlockSpec((bb, tl, MEM_W), lambda bi, li: (bi, li, 0)),
        out_shape=jax.ShapeDtypeStruct((b, l, MEM_W), F32),
        compiler_params=_cparams("arbitrary", "arbitrary"),
        name="mem_attend",
    )(qm, kt_layers, vt_layers, jnp.tile(qn_g, MEM_HEADS)[None, :])


def _out_proj_body(x_ref, mix_ref, mem_ref, w_ref, o_ref):
    o_ref[...] = (x_ref[...] + _dot(mix_ref[...], w_ref[0, :MIX_W, :])
                  + _dot(mem_ref[...], w_ref[0, MIX_W:, :]))


def _out_proj(x, mix, mem, w_layers, layer, tm):
    t = x.shape[0]
    return pl.pallas_call(
        _out_proj_body,
        grid=(t // tm,),
        in_specs=[
            pl.BlockSpec((tm, D_MODEL), lambda i: (i, 0)),
            pl.BlockSpec((tm, MIX_W), lambda i: (i, 0)),
            pl.BlockSpec((tm, MEM_W), lambda i: (i, 0)),
            pl.BlockSpec((1, D_MODEL, D_MODEL), lambda i: (layer, 0, 0)),
        ],
        out_specs=pl.BlockSpec((tm, D_MODEL), lambda i: (i, 0)),
        out_shape=jax.ShapeDtypeStruct((t, D_MODEL), F32),
        compiler_params=_cparams("arbitrary", vmem=VMEM_LIMIT),
        name="out_proj",
    )(x, mix, mem, w_layers)


def _silu(g):
    return g / (1.0 + jnp.exp(-g))


def _ffn_body(x_ref, g_ref, wgu_ref, wd_ref, o_ref):
    x = x_ref[...]
    h = _rms_rows(x, g_ref[...])
    o_ref[...] = x
    for c in range(N_FF_CHUNKS):
        gate = slice(c * FF_CHUNK, (c + 1) * FF_CHUNK)
        up = slice(D_FF + c * FF_CHUNK, D_FF + (c + 1) * FF_CHUNK)
        a = _silu(_dot(h, wgu_ref[:, gate])) * _dot(h, wgu_ref[:, up])
        o_ref[...] += _dot(a, wd_ref[gate, :])


def _ffn(x, g, w_gu, w_down, tm):
    t = x.shape[0]
    resident = lambda shape: pl.BlockSpec(shape, lambda i: (0, 0), pipeline_mode=pl.Buffered(1))
    return pl.pallas_call(
        _ffn_body,
        grid=(t // tm,),
        in_specs=[
            pl.BlockSpec((tm, D_MODEL), lambda i: (i, 0)),
            pl.BlockSpec((1, D_MODEL), lambda i: (0, 0)),
            resident((D_MODEL, 2 * D_FF)),
            resident((D_FF, D_MODEL)),
        ],
        out_specs=pl.BlockSpec((tm, D_MODEL), lambda i: (i, 0)),
        out_shape=jax.ShapeDtypeStruct((t, D_MODEL), F32),
        compiler_params=_cparams("arbitrary", vmem=VMEM_LIMIT),
        name="ffn",
    )(x, g, w_gu, w_down)


_FL_ROWS = 2 * SUBLANES
_AUG0 = HEAD_DIM


def _aug_pieces(c_col):
    hi, mid, lo = _split3(c_col)
    bc = lambda a: jnp.broadcast_to(a.astype(F32), (c_col.shape[0], LANES))
    return bc(hi), bc(mid), bc(lo)


def _fox_proj_prompt_body(x_ref, g_ref, wq_ref, wk_ref, wv_ref, wqm_ref, wf_ref, bf_ref, qg_ref, kg_ref,
                          qp_ref, kp_ref, kt_ref, vt_ref, vtb_ref, lft_ref, qm_ref, carry_ref, *, tm, per_seq):
    @pl.when(pl.program_id(0) % per_seq == 0)
    def _():
        carry_ref[...] = jnp.zeros(carry_ref.shape, F32)

    h = _rms_rows(x_ref[...], g_ref[...])
    q_pieces = _head_rms_rows_t(_dot_nt(wq_ref[...], h), qg_ref[...], FOX_HEADS)
    k_pieces = _head_rms_rows_t(_dot_nt(wk_ref[...], h), kg_ref[...], FOX_HEADS)
    for h_i, piece in enumerate(k_pieces):
        kt_ref[0, h_i * HEAD_DIM:(h_i + 1) * HEAD_DIM, :] = piece
    vt = _dot_nt(wv_ref[...], h)
    vt_ref[0] = vt
    vtb_ref[0] = vt.astype(BF16)
    q = jnp.concatenate(q_pieces, axis=0).T * (SCALE * LOG2E)
    k = jnp.concatenate(k_pieces, axis=0).T
    qm_ref[...] = _dot(h, wqm_ref[...])
    lane = lax.broadcasted_iota(jnp.int32, (tm, LANES), 1)
    lf = jnp.where(lane < FOX_HEADS, _log_sigmoid(_dot(h, wf_ref[...]) + bf_ref[...]), 0.0)
    lft_ref[...] = lf.T[:_FL_ROWS, :]
    r = lax.broadcasted_iota(jnp.int32, (tm, tm), 0)
    c = lax.broadcasted_iota(jnp.int32, (tm, tm), 1)
    lower = (c <= r).astype(BF16)
    hi, mid, lo = _split3(lf)
    csum = _dot(lower, hi) + _dot(lower, mid) + _dot(lower, lo) + carry_ref[0:1, :]
    carry_ref[...] = jnp.broadcast_to(csum[tm - 1:tm, :], carry_ref.shape)
    c2 = csum * LOG2E
    one = jnp.ones((tm, LANES), F32)
    zero = jnp.zeros((tm, LANES), F32)
    for hd in range(FOX_HEADS):
        chi, cmid, clo = _aug_pieces(c2[:, hd:hd + 1])
        aug_q = jnp.where(lane == _AUG0, chi, jnp.where(lane == _AUG0 + 1, cmid, jnp.where(
            lane == _AUG0 + 2, clo, jnp.where(lane < _AUG0 + 6, one, zero))))
        aug_k = jnp.where(lane < _AUG0 + 3, one, jnp.where(lane == _AUG0 + 3, -chi, jnp.where(
            lane == _AUG0 + 4, -cmid, jnp.where(lane == _AUG0 + 5, -clo, zero))))
        col = slice((hd // 2) * LANES, (hd // 2 + 1) * LANES)
        qc, kc = q[:, col], k[:, col]
        if hd % 2:
            qc = pltpu.roll(qc, HEAD_DIM, 1)
            kc = pltpu.roll(kc, HEAD_DIM, 1)
        qp_ref[0, hd] = jnp.where(lane < HEAD_DIM, qc, aug_q).astype(BF16)
        kp_ref[0, hd] = jnp.where(lane < HEAD_DIM, kc, aug_k).astype(BF16)


def _fox_proj_prompt(x, g, wq_t, wk_t, wv_t, wqm, wf, bf_row, qg_col, kg_col, tm, n_seq):
    t = x.shape[0]
    seq = t // n_seq
    per_seq = seq // tm

    def const(shape):
        return pl.BlockSpec(shape, lambda i: tuple(0 for _ in shape))

    heads_spec = pl.BlockSpec((1, FOX_HEADS, tm, LANES), lambda i: (i // per_seq, 0, i % per_seq, 0))
    t_spec = pl.BlockSpec((1, MIX_W, tm), lambda i: (i // per_seq, 0, i % per_seq))
    return pl.pallas_call(
        functools.partial(_fox_proj_prompt_body, tm=tm, per_seq=per_seq),
        grid=(t // tm,),
        in_specs=[
            pl.BlockSpec((tm, D_MODEL), lambda i: (i, 0)),
            const((1, D_MODEL)),
            const((MIX_W, D_MODEL)), const((MIX_W, D_MODEL)), const((MIX_W, D_MODEL)),
            const((D_MODEL, MEM_W)), const((D_MODEL, LANES)),
            const((1, LANES)), const((HEAD_DIM, 1)), const((HEAD_DIM, 1)),
        ],
        out_specs=[heads_spec, heads_spec, t_spec, t_spec, t_spec,
                   pl.BlockSpec((_FL_ROWS, tm), lambda i: (0, i)),
                   pl.BlockSpec((tm, MEM_W), lambda i: (i, 0))],
        out_shape=[
            jax.ShapeDtypeStruct((n_seq, FOX_HEADS, seq, LANES), BF16),
            jax.ShapeDtypeStruct((n_seq, FOX_HEADS, seq, LANES), BF16),
            jax.ShapeDtypeStruct((n_seq, MIX_W, seq), F32),
            jax.ShapeDtypeStruct((n_seq, MIX_W, seq), F32),
            jax.ShapeDtypeStruct((n_seq, MIX_W, seq), BF16),
            jax.ShapeDtypeStruct((_FL_ROWS, t), F32),
            jax.ShapeDtypeStruct((t, MEM_W), F32),
        ],
        scratch_shapes=[pltpu.VMEM((SUBLANES, LANES), F32)],
        compiler_params=_cparams("arbitrary", vmem=VMEM_LIMIT),
        name="fox_proj_prompt",
    )(x, g, wq_t, wk_t, wv_t, wqm, wf, bf_row, qg_col, kg_col)


def _fox_proj_sample_body(x_ref, g_ref, wq_ref, wk_ref, wv_ref, wqm_ref, wf_ref, bf_ref, qg_ref, kg_ref,
                          q_ref, k_ref, v_ref, lft_ref, qm_ref):
    h = _rms_rows(x_ref[...], g_ref[...])
    q_ref[...] = _head_rms_lanes(_dot(h, wq_ref[...]), qg_ref[...])
    k_ref[...] = _head_rms_lanes(_dot(h, wk_ref[...]), kg_ref[...])
    v_ref[...] = _dot(h, wv_ref[...])
    qm_ref[...] = _dot(h, wqm_ref[...])
    lane = lax.broadcasted_iota(jnp.int32, (x_ref.shape[0], LANES), 1)
    lf = jnp.where(lane < FOX_HEADS, _log_sigmoid(_dot(h, wf_ref[...]) + bf_ref[...]), 0.0)
    lft_ref[...] = lf.T[:_FL_ROWS, :]


def _fox_proj_sample(x, g, wq, wk, wv, wqm, wf, bf_row, qg_t, kg_t):
    t = x.shape[0]
    rows = jax.ShapeDtypeStruct((t, MIX_W), F32)
    return pl.pallas_call(
        _fox_proj_sample_body,
        out_shape=[rows, rows, rows, jax.ShapeDtypeStruct((_FL_ROWS, t), F32),
                   jax.ShapeDtypeStruct((t, MEM_W), F32)],
        compiler_params=_cparams(vmem=VMEM_LIMIT),
        name="fox_proj_sample",
    )(x, g, wq, wk, wv, wqm, wf, bf_row, qg_t, kg_t)


def _fox_flash_body(qi_ref, ki_ref, qp_ref, kp_ref, vt_ref, o_ref, m_ref, l_ref, acc_ref, *, blk):
    p = pl.program_id(2)
    qi = qi_ref[p]
    ki = ki_ref[p]

    @pl.when(ki == 0)
    def _():
        m_ref[...] = jnp.full(m_ref.shape, -jnp.inf, F32)
        l_ref[...] = jnp.zeros(l_ref.shape, F32)
        acc_ref[...] = jnp.zeros(acc_ref.shape, F32)

    def step(diagonal):
        for hh in range(FLASH_HEADS):
            vt = vt_ref[0, hh * HEAD_DIM:(hh + 1) * HEAD_DIM, :]
            st = _dot_nt(kp_ref[0, hh], qp_ref[0, hh])
            if diagonal:
                key = lax.broadcasted_iota(jnp.int32, st.shape, 0)
                qry = lax.broadcasted_iota(jnp.int32, st.shape, 1)
                st = jnp.where(key <= qry, st, -jnp.inf)
            m_prev = m_ref[hh]
            m_new = jnp.maximum(m_prev, jnp.max(st, axis=0, keepdims=True))
            alpha = jnp.exp2(m_prev - m_new)
            pt = jnp.exp2(st - m_new)
            l_ref[hh] = alpha * l_ref[hh] + jnp.sum(pt, axis=0, keepdims=True)
            acc_ref[hh] = alpha * acc_ref[hh] + _dot(vt, pt.astype(BF16))
            m_ref[hh] = m_new

    @pl.when(ki < qi)
    def _():
        step(False)

    @pl.when(ki == qi)
    def _():
        step(True)
        for hh in range(FLASH_HEADS):
            o_ref[0, hh * HEAD_DIM:(hh + 1) * HEAD_DIM, :] = acc_ref[hh] / l_ref[hh]


def _fox_flash(qp, kp, vt, blk):
    b, _, s, _ = qp.shape
    n = s // blk
    pairs = [(qi, ki) for qi in range(n) for ki in range(qi + 1)]
    qi_tab = jnp.asarray([p[0] for p in pairs], jnp.int32)
    ki_tab = jnp.asarray([p[1] for p in pairs], jnp.int32)
    fh = FLASH_HEADS
    ch = fh * HEAD_DIM
    grid_spec = pltpu.PrefetchScalarGridSpec(
        num_scalar_prefetch=2,
        grid=(b, FOX_HEADS // fh, len(pairs)),
        in_specs=[
            pl.BlockSpec((1, fh, blk, LANES), lambda bi, hg, p, qt, kt: (bi, hg, qt[p], 0)),
            pl.BlockSpec((1, fh, blk, LANES), lambda bi, hg, p, qt, kt: (bi, hg, kt[p], 0)),
            pl.BlockSpec((1, ch, blk), lambda bi, hg, p, qt, kt: (bi, hg, kt[p])),
        ],
        out_specs=pl.BlockSpec((1, ch, blk), lambda bi, hg, p, qt, kt: (bi, hg, qt[p])),
        scratch_shapes=[pltpu.VMEM((fh, 1, blk), F32), pltpu.VMEM((fh, 1, blk), F32),
                        pltpu.VMEM((fh, HEAD_DIM, blk), F32)],
    )
    return pl.pallas_call(
        functools.partial(_fox_flash_body, blk=blk),
        grid_spec=grid_spec,
        out_shape=jax.ShapeDtypeStruct((b, MIX_W, s), F32),
        compiler_params=_cparams("arbitrary", "arbitrary", "arbitrary", vmem=VMEM_LIMIT),
        name="fox_flash",
    )(qi_tab, ki_tab, qp, kp, vt)


_QROWS = FOX_HEADS * SAMPLE_ROWS
_CHUNK = PAGES_PER_STEP * PAGE_SIZE


def _expand_heads(x):
    n = x.shape[1]
    return jnp.broadcast_to(x[:FOX_HEADS, None, :], (FOX_HEADS, SAMPLE_ROWS, n)).reshape(_QROWS, n)


def _page_copies(pt_ref, kt_hbm, vt_hbm, lf_hbm, kt_buf, vt_buf, lf_buf, sem, g, slot, steps, n_pages):
    bi = g // steps
    first = n_pages - (g % steps + 1) * PAGES_PER_STEP
    copies = []
    for i in range(PAGES_PER_STEP):
        page = pt_ref[bi, first + i]
        lanes = pl.ds(i * PAGE_SIZE, PAGE_SIZE)
        copies.append(pltpu.make_async_copy(kt_hbm.at[page], kt_buf.at[slot, :, lanes], sem.at[slot, 0]))
        copies.append(pltpu.make_async_copy(vt_hbm.at[page], vt_buf.at[slot, :, lanes], sem.at[slot, 1]))
        copies.append(pltpu.make_async_copy(lf_hbm.at[:, page], lf_buf.at[slot, i, pl.ds(0, FOX_HEADS)],
                                            sem.at[slot, 2]))
    return copies


def _fox_decode_body(pt_ref, kt_hbm, vt_hbm, lf_hbm, q_ref, kn_ref, vn_ref, lfn_ref, o_ref,
                     kt_buf, vt_buf, lf_buf, sem, qbd_ref, m_ref, l_ref, acc_ref, carry_ref, crow_ref,
                     *, n_new, steps, n_pages):
    g = pl.program_id(0)
    n_steps = pl.num_programs(0)
    j = g % steps
    slot = g % DECODE_SLOTS
    copies = functools.partial(_page_copies, pt_ref, kt_hbm, vt_hbm, lf_hbm, kt_buf, vt_buf, lf_buf, sem,
                               steps=steps, n_pages=n_pages)
    row_l = lax.broadcasted_iota(jnp.int32, (_QROWS, 1), 0) % SAMPLE_ROWS

    @pl.when(g == 0)
    def _():
        lf_buf[...] = jnp.zeros(lf_buf.shape, F32)
        for ahead in range(DECODE_SLOTS - 1):
            for cp in copies(g=ahead, slot=ahead):
                cp.start()

    @pl.when(g + DECODE_SLOTS - 1 < n_steps)
    def _():
        for cp in copies(g=g + DECODE_SLOTS - 1, slot=(g + DECODE_SLOTS - 1) % DECODE_SLOTS):
            cp.start()

    @pl.when(j == 0)
    def _():
        q = q_ref[0] * SCALE
        lh = _lane_head(q.shape, 1)
        for h in range(FOX_HEADS):
            qbd_ref[h * SAMPLE_ROWS:(h + 1) * SAMPLE_ROWS, :] = jnp.where(lh == h, q, 0.0)
        lfn = lfn_ref[0]
        lane = lax.broadcasted_iota(jnp.int32, lfn.shape, 1)
        c = jnp.zeros(lfn.shape, F32)
        for m in range(n_new):
            cm = jnp.sum(jnp.where(lane <= m, lfn, 0.0), axis=1, keepdims=True)
            c = jnp.where(lane == m, cm, c)
        c_q = _expand_heads(c)
        col = lax.broadcasted_iota(jnp.int32, c_q.shape, 1)
        crow = jnp.sum(jnp.where(col == row_l, c_q, 0.0), axis=1, keepdims=True)
        crow_ref[...] = crow
        s = _dot_nt(qbd_ref[...], kn_ref[0]) + (crow - c_q)
        valid = (col < n_new) & ((col <= row_l) | (row_l >= n_new))
        s = jnp.where(valid, s, -jnp.inf)
        m0 = jnp.max(s, axis=-1, keepdims=True)
        p = jnp.exp(s - m0)
        m_ref[...] = m0
        l_ref[...] = jnp.sum(p, axis=-1, keepdims=True)
        acc_ref[...] = _dot(p, vn_ref[0])
        carry_ref[...] = jnp.zeros(carry_ref.shape, F32)

    for cp in copies(g=g, slot=slot):
        cp.wait()

    r = lax.broadcasted_iota(jnp.int32, (PAGE_SIZE, PAGE_SIZE), 0)
    cc = lax.broadcasted_iota(jnp.int32, (PAGE_SIZE, PAGE_SIZE), 1)
    later = (r > cc).astype(BF16)
    lf3 = lf_buf[slot]
    tot = jnp.sum(lf3, axis=2, keepdims=True)
    hi, mid, lo = _split3(lf3.reshape(PAGES_PER_STEP * _FL_ROWS, PAGE_SIZE))
    d_in = (_dot(hi, later) + _dot(mid, later) + _dot(lo, later)).reshape(PAGES_PER_STEP, _FL_ROWS, PAGE_SIZE)
    after = carry_ref[:, 0:1]
    d_pages = [None] * PAGES_PER_STEP
    for i in reversed(range(PAGES_PER_STEP)):
        d_pages[i] = _expand_heads(d_in[i] + after)
        after = after + tot[i]
    carry_ref[...] = jnp.broadcast_to(after, carry_ref.shape)
    d = jnp.concatenate(d_pages, axis=1)

    vt = vt_buf[slot]
    s = _dot(qbd_ref[...], kt_buf[slot]) + crow_ref[...] + d
    m_prev = m_ref[...]
    m_new = jnp.maximum(m_prev, jnp.max(s, axis=-1, keepdims=True))
    alpha = jnp.exp(m_prev - m_new)
    p = jnp.exp(s - m_new)
    l_ref[...] = alpha * l_ref[...] + jnp.sum(p, axis=-1, keepdims=True)
    acc_ref[...] = alpha * acc_ref[...] + _dot_nt(p, vt)
    m_ref[...] = m_new

    @pl.when(j == steps - 1)
    def _():
        res = acc_ref[...] / l_ref[...]
        lh = _lane_head((SAMPLE_ROWS, MIX_W), 1)
        out = jnp.zeros((SAMPLE_ROWS, MIX_W), F32)
        for h in range(FOX_HEADS):
            out = jnp.where(lh == h, res[h * SAMPLE_ROWS:(h + 1) * SAMPLE_ROWS, :], out)
        o_ref[0] = out


def _fox_decode(page_table, kt_pages, vt_pages, lf_heads, q8, kn8, vn8, lfn, n_new):
    b, n_pages = page_table.shape
    steps = n_pages // PAGES_PER_STEP
    per_b = lambda g, pt: (g // steps, 0, 0)
    any_spec = pl.BlockSpec(memory_space=pl.ANY)
    grid_spec = pltpu.PrefetchScalarGridSpec(
        num_scalar_prefetch=1,
        grid=(b * steps,),
        in_specs=[any_spec, any_spec, any_spec,
                  pl.BlockSpec((1, SAMPLE_ROWS, MIX_W), per_b),
                  pl.BlockSpec((1, SAMPLE_ROWS, MIX_W), per_b),
                  pl.BlockSpec((1, SAMPLE_ROWS, MIX_W), per_b),
                  pl.BlockSpec((1, _FL_ROWS, SAMPLE_ROWS), per_b)],
        out_specs=pl.BlockSpec((1, SAMPLE_ROWS, MIX_W), per_b),
        scratch_shapes=[
            pltpu.VMEM((DECODE_SLOTS, MIX_W, _CHUNK), F32),
            pltpu.VMEM((DECODE_SLOTS, MIX_W, _CHUNK), F32),
            pltpu.VMEM((DECODE_SLOTS, PAGES_PER_STEP, _FL_ROWS, PAGE_SIZE), F32),
            pltpu.SemaphoreType.DMA((DECODE_SLOTS, 3)),
            pltpu.VMEM((_QROWS, MIX_W), F32),
            pltpu.VMEM((_QROWS, 1), F32), pltpu.VMEM((_QROWS, 1), F32),
            pltpu.VMEM((_QROWS, MIX_W), F32),
            pltpu.VMEM((_FL_ROWS, LANES), F32),
            pltpu.VMEM((_QROWS, 1), F32),
        ],
    )
    return pl.pallas_call(
        functools.partial(_fox_decode_body, n_new=n_new, steps=steps, n_pages=n_pages),
        grid_spec=grid_spec,
        out_shape=jax.ShapeDtypeStruct((b, SAMPLE_ROWS, MIX_W), F32),
        compiler_params=_cparams("arbitrary", vmem=VMEM_LIMIT),
        name="fox_decode",
    )(page_table, kt_pages, vt_pages, lf_heads, q8, kn8, vn8, lfn)


def _router_body(x_ref, g_ref, wrt_ref, h_ref, r_ref, cnt_ref, carry_ref, *, tm):
    _route_tile(x_ref[...], g_ref, wrt_ref, h_ref, r_ref, cnt_ref, carry_ref, tm)


def _out_proj_router_body(x_ref, mixt_ref, mem_ref, wo_ref, g_ref, wrt_ref,
                          x1_ref, h_ref, r_ref, cnt_ref, carry_ref, *, tm):
    x1 = (x_ref[...] + _dot(mixt_ref[0].T, wo_ref[0, :MIX_W, :])
          + _dot(mem_ref[...], wo_ref[0, MIX_W:, :]))
    x1_ref[...] = x1
    _route_tile(x1, g_ref, wrt_ref, h_ref, r_ref, cnt_ref, carry_ref, tm)


def _route_tile(x, g_ref, wrt_ref, h_ref, r_ref, cnt_ref, carry_ref, tm):
    @pl.when(pl.program_id(0) == 0)
    def _():
        carry_ref[...] = jnp.zeros(carry_ref.shape, F32)

    h = _rms_rows(x, g_ref[...])
    h_ref[...] = h
    h_hi = h.astype(BF16)
    h_lo = (h - h_hi.astype(F32)).astype(BF16)
    w = wrt_ref[...]
    w_hi = w.astype(BF16)
    w_lo = (w - w_hi.astype(F32)).astype(BF16)
    lg = _dot_nt(w_hi, h_hi) + _dot_nt(w_hi, h_lo) + _dot_nt(w_lo, h_hi)
    idx = lax.broadcasted_iota(jnp.int32, lg.shape, 0)
    m1 = jnp.max(lg, axis=0, keepdims=True)
    i1 = jnp.min(jnp.where(lg == m1, idx, N_EXPERTS), axis=0, keepdims=True)
    sel1 = idx == i1
    lg2 = jnp.where(sel1, -jnp.inf, lg)
    m2 = jnp.max(lg2, axis=0, keepdims=True)
    i2 = jnp.min(jnp.where(lg2 == m2, idx, N_EXPERTS), axis=0, keepdims=True)
    sel2 = idx == i2
    e = jnp.exp(m2 - m1)
    g1 = 1.0 / (1.0 + e)
    g2 = e / (1.0 + e)
    assign = jnp.where(sel1 | sel2, 1.0, 0.0)
    r = lax.broadcasted_iota(jnp.int32, (tm, tm), 0)
    c = lax.broadcasted_iota(jnp.int32, (tm, tm), 1)
    before = (r < c).astype(BF16)
    rank = _dot(assign.astype(BF16), before) + carry_ref[:, 0:1]
    r1 = jnp.sum(jnp.where(sel1, rank, 0.0), axis=0, keepdims=True)
    r2 = jnp.sum(jnp.where(sel2, rank, 0.0), axis=0, keepdims=True)
    carry = carry_ref[...] + jnp.sum(assign, axis=1, keepdims=True)
    carry_ref[...] = carry
    cnt_ref[...] = carry
    rows = [i1.astype(F32), i2.astype(F32), r1, r2, g1, g2]
    out = jnp.zeros(lg.shape, F32)
    for k, v in enumerate(rows):
        out = jnp.where(idx == k, v, out)
    r_ref[...] = out


def _router(x, g, wrt, tm):
    t = x.shape[0]
    return pl.pallas_call(
        functools.partial(_router_body, tm=tm),
        grid=(t // tm,),
        in_specs=[
            pl.BlockSpec((tm, D_MODEL), lambda i: (i, 0)),
            pl.BlockSpec((1, D_MODEL), lambda i: (0, 0)),
            pl.BlockSpec((N_EXPERTS, D_MODEL), lambda i: (0, 0)),
        ],
        out_specs=[
            pl.BlockSpec((tm, D_MODEL), lambda i: (i, 0)),
            pl.BlockSpec((N_EXPERTS, tm), lambda i: (0, i)),
            pl.BlockSpec((N_EXPERTS, LANES), lambda i: (0, 0)),
        ],
        out_shape=[
            jax.ShapeDtypeStruct((t, D_MODEL), F32),
            jax.ShapeDtypeStruct((N_EXPERTS, t), F32),
            jax.ShapeDtypeStruct((N_EXPERTS, LANES), F32),
        ],
        scratch_shapes=[pltpu.VMEM((N_EXPERTS, LANES), F32)],
        compiler_params=_cparams("arbitrary", vmem=VMEM_LIMIT),
        name="router",
    )(x, g, wrt)


def _out_proj_router(x, mix_t, mem, w_layers, layer, g, wrt, tm):
    t = x.shape[0]
    per_seq = mix_t.shape[2] // tm
    return pl.pallas_call(
        functools.partial(_out_proj_router_body, tm=tm),
        grid=(t // tm,),
        in_specs=[
            pl.BlockSpec((tm, D_MODEL), lambda i: (i, 0)),
            pl.BlockSpec((1, MIX_W, tm), lambda i: (i // per_seq, 0, i % per_seq)),
            pl.BlockSpec((tm, MEM_W), lambda i: (i, 0)),
            pl.BlockSpec((1, D_MODEL, D_MODEL), lambda i: (layer, 0, 0)),
            pl.BlockSpec((1, D_MODEL), lambda i: (0, 0)),
            pl.BlockSpec((N_EXPERTS, D_MODEL), lambda i: (0, 0)),
        ],
        out_specs=[
            pl.BlockSpec((tm, D_MODEL), lambda i: (i, 0)),
            pl.BlockSpec((tm, D_MODEL), lambda i: (i, 0)),
            pl.BlockSpec((N_EXPERTS, tm), lambda i: (0, i)),
            pl.BlockSpec((N_EXPERTS, LANES), lambda i: (0, 0)),
        ],
        out_shape=[
            jax.ShapeDtypeStruct((t, D_MODEL), F32),
            jax.ShapeDtypeStruct((t, D_MODEL), F32),
            jax.ShapeDtypeStruct((N_EXPERTS, t), F32),
            jax.ShapeDtypeStruct((N_EXPERTS, LANES), F32),
        ],
        scratch_shapes=[pltpu.VMEM((N_EXPERTS, LANES), F32)],
        compiler_params=_cparams("arbitrary", vmem=VMEM_LIMIT),
        name="out_proj_router",
    )(x, mix_t, mem, w_layers, g, wrt)


_SCATTER_TM = 512


def _row_copy(src_ref, src_row, dst_ref, dst_row, sem):
    return pltpu.make_async_copy(src_ref.at[pl.ds(src_row, 1)], dst_ref.at[pl.ds(dst_row, 1)], sem)


def _scatter_body(d1_ref, d2_ref, hp_ref, hs_ref, xs_ref, sem, *, n_prompt_tiles):
    i = pl.program_id(0)

    def run(src_ref):
        n_rows = src_ref.shape[0]

        def start(r, carry):
            _row_copy(src_ref, r, xs_ref, d1_ref[0, 0, r], sem.at[0]).start(priority=0)
            _row_copy(src_ref, r, xs_ref, d2_ref[0, 0, r], sem.at[1]).start(priority=1)
            return carry

        lax.fori_loop(0, n_rows, start, 0, unroll=DMA_UNROLL)
        for k in range(2):
            pltpu.make_async_copy(src_ref, xs_ref.at[pl.ds(0, n_rows)], sem.at[k]).wait()

    @pl.when(i < n_prompt_tiles)
    def _():
        run(hp_ref)

    @pl.when(i >= n_prompt_tiles)
    def _():
        run(hs_ref)


def _scatter_rows(d1p, d2p, d1s, d2s, h_p, h_s, n_sorted):
    tm = _SCATTER_TM
    npt = h_p.shape[0] // tm
    n_s = h_s.shape[0]
    pad = jnp.zeros((tm - n_s,), jnp.int32)
    d1 = jnp.concatenate([d1p, d1s, pad])
    d2 = jnp.concatenate([d2p, d2s, pad])
    smem = lambda: pl.BlockSpec((1, 1, tm), lambda i: (i, 0, 0), memory_space=pltpu.SMEM)
    return pl.pallas_call(
        functools.partial(_scatter_body, n_prompt_tiles=npt),
        grid=(npt + 1,),
        in_specs=[
            smem(), smem(),
            pl.BlockSpec((tm, D_MODEL), lambda i: (jnp.minimum(i, npt - 1), 0)),
            pl.BlockSpec((n_s, D_MODEL), lambda i: (0, 0)),
        ],
        out_specs=pl.BlockSpec(memory_space=pl.ANY),
        out_shape=jax.ShapeDtypeStruct((n_sorted, D_MODEL), F32),
        scratch_shapes=[pltpu.SemaphoreType.DMA((2,))],
        compiler_params=_cparams("arbitrary", vmem=VMEM_LIMIT),
        name="moe_scatter",
    )(d1.reshape(-1, 1, tm), d2.reshape(-1, 1, tm), h_p, h_s)


def _gmm_body(tile_ref, exp_ref, lo_ref, hi_ref, first_ref, x_ref, wg_ref, wu_ref, wd_ref, o_ref):
    del tile_ref, exp_ref
    v = pl.program_id(0)
    c = pl.program_id(1)

    @pl.when((first_ref[v] == 1) & (c == 0))
    def _():
        o_ref[...] = jnp.zeros(o_ref.shape, F32)

    lo = lo_ref[v]
    hi = hi_ref[v]
    whole = (lo == 0) & (hi == MOE_TM)

    def swiglu_into(rows, mask):
        x = x_ref[rows, :]
        for off, n in _MOE_SUBS:
            cols = slice(off, off + n)
            a = _silu(_dot(x, wg_ref[0, :, cols])) * _dot(x, wu_ref[0, :, cols])
            y = _dot(a, wd_ref[0, cols, :])
            o_ref[rows, :] += y if mask is None else jnp.where(mask, y, 0.0)

    @pl.when(whole)
    def _():
        swiglu_into(slice(None), None)

    @pl.when(jnp.logical_not(whole) & (hi > lo))
    def _():
        for r0 in range(0, MOE_TM, MOE_SUB_ROWS):
            @pl.when((lo < r0 + MOE_SUB_ROWS) & (hi > r0))
            def _():
                rows = r0 + lax.broadcasted_iota(jnp.int32, (MOE_SUB_ROWS, 1), 0)
                swiglu_into(slice(r0, r0 + MOE_SUB_ROWS), (rows >= lo) & (rows < hi))


_MOE_FF = D_FF // MOE_FF_SPLIT
_MOE_SUBS = tuple((off, min(MXU_N, _MOE_FF - off)) for off in range(0, _MOE_FF, MXU_N))


def _gmm(meta, xs, w_gu, w_down):
    tile, expert, lo, hi, first = meta
    n_visits = tile.shape[0]
    grid_spec = pltpu.PrefetchScalarGridSpec(
        num_scalar_prefetch=5,
        grid=(n_visits, MOE_FF_SPLIT),
        in_specs=[
            pl.BlockSpec((MOE_TM, D_MODEL), lambda v, c, t, e, *_: (t[v], 0)),
            pl.BlockSpec((1, D_MODEL, _MOE_FF), lambda v, c, t, e, *_: (e[v], 0, c)),
            pl.BlockSpec((1, D_MODEL, _MOE_FF), lambda v, c, t, e, *_: (e[v], 0, MOE_FF_SPLIT + c)),
            pl.BlockSpec((1, _MOE_FF, D_MODEL), lambda v, c, t, e, *_: (e[v], c, 0)),
        ],
        out_specs=pl.BlockSpec((MOE_TM, D_MODEL), lambda v, c, t, e, *_: (t[v], 0)),
    )
    return pl.pallas_call(
        _gmm_body,
        grid_spec=grid_spec,
        out_shape=jax.ShapeDtypeStruct(xs.shape, F32),
        compiler_params=_cparams("arbitrary", "arbitrary", vmem=VMEM_LIMIT),
        name="moe_gmm",
    )(tile, expert, lo, hi, first, xs, w_gu, w_gu, w_down)


def _gmm_meta(counts, n_sorted):
    n_tiles = n_sorted // MOE_TM
    n_visits = n_tiles + N_EXPERTS - 1
    ends = jnp.cumsum(counts)
    starts = ends - counts
    first_tile = starts // MOE_TM
    last_tile = jnp.maximum(ends - 1, 0) // MOE_TM
    nv = jnp.where(counts > 0, last_tile - first_tile + 1, 0)
    cv = jnp.cumsum(nv)
    v = jnp.arange(n_visits, dtype=jnp.int32)
    total = cv[-1]
    valid = v < total
    vc = jnp.minimum(v, total - 1)
    expert = jnp.sum((cv[None, :] <= vc[:, None]).astype(jnp.int32), axis=1)
    sel = (expert[:, None] == jnp.arange(N_EXPERTS, dtype=jnp.int32)[None, :]).astype(jnp.int32)
    of_expert = lambda a: jnp.sum(sel * a[None, :].astype(jnp.int32), axis=1)
    tile = of_expert(first_tile) + (vc - (of_expert(cv) - of_expert(nv)))
    lo = jnp.maximum(of_expert(starts), tile * MOE_TM) - tile * MOE_TM
    hi = jnp.minimum(of_expert(ends), (tile + 1) * MOE_TM) - tile * MOE_TM
    lo = jnp.where(valid, lo, 0)
    hi = jnp.where(valid, hi, 0)
    prev_tile = jnp.concatenate([jnp.full((1,), -1, jnp.int32), tile[:-1]])
    first = (valid & (tile != prev_tile)).astype(jnp.int32)
    as_i32 = lambda a: a.astype(jnp.int32)
    return as_i32(tile), as_i32(expert), as_i32(lo), as_i32(hi), first


def _combine_body(d1_ref, d2_ref, x_ref, g_ref, os_ref, y_ref, buf_ref, sem, *, tm, n_tiles):
    i = pl.program_id(0)
    slot = i % 2

    def issue(tile, slot_):
        def body(r, carry):
            _row_copy(os_ref, d1_ref[tile, r], buf_ref.at[slot_, 0], r, sem.at[slot_, 0]).start(priority=0)
            _row_copy(os_ref, d2_ref[tile, r], buf_ref.at[slot_, 1], r, sem.at[slot_, 1]).start(priority=1)
            return carry
        lax.fori_loop(0, tm, body, 0, unroll=DMA_UNROLL)

    @pl.when(i == 0)
    def _():
        issue(0, 0)

    @pl.when(i + 1 < n_tiles)
    def _():
        issue(i + 1, 1 - slot)

    for k in range(2):
        pltpu.make_async_copy(os_ref.at[pl.ds(0, tm)], buf_ref.at[slot, k], sem.at[slot, k]).wait()

    g = g_ref[...]
    y_ref[...] = x_ref[...] + g[:, 0:1] * buf_ref[slot, 0] + g[:, 1:2] * buf_ref[slot, 1]


def _combine(d1, d2, x, gates, o_sorted, tm):
    t = x.shape[0]
    n_tiles = t // tm
    grid_spec = pltpu.PrefetchScalarGridSpec(
        num_scalar_prefetch=2,
        grid=(n_tiles,),
        in_specs=[
            pl.BlockSpec((tm, D_MODEL), lambda i, *_: (i, 0)),
            pl.BlockSpec((tm, 2), lambda i, *_: (i, 0)),
            pl.BlockSpec(memory_space=pl.ANY),
        ],
        out_specs=pl.BlockSpec((tm, D_MODEL), lambda i, *_: (i, 0)),
        scratch_shapes=[pltpu.VMEM((2, 2, tm, D_MODEL), F32), pltpu.SemaphoreType.DMA((2, 2))],
    )
    return pl.pallas_call(
        functools.partial(_combine_body, tm=tm, n_tiles=n_tiles),
        grid_spec=grid_spec,
        out_shape=jax.ShapeDtypeStruct((t, D_MODEL), F32),
        compiler_params=_cparams("arbitrary", vmem=VMEM_LIMIT),
        name="moe_combine",
    )(d1.reshape(n_tiles, tm), d2.reshape(n_tiles, tm), x, gates, o_sorted)


def _block_diag(w_grp):
    g, n, _ = w_grp.shape
    out = jnp.zeros((g * n, g * n), w_grp.dtype)
    for i in range(g):
        out = out.at[i * n:(i + 1) * n, i * n:(i + 1) * n].set(w_grp[i])
    return out


def kernel(x_prompt, x_sample, cache_mem_k, cache_mem_v, state_pool, cache_fox_k, cache_fox_v, cache_fox_logf, page_table, mem_prompt, norm_mix_g, norm_ffn_g, norm_mem_g, w_mem_kv, mem_q_norm_g, mem_k_norm_g, w_out, w_in_pool, w_pool_group, pool_scale, w_in_fox, b_forget, fox_q_norm_g, fox_k_norm_g, w_ffn_gu, w_ffn_down, w_router, w_exp_gu, w_exp_down):
    b, s, d = x_prompt.shape
    bs, ls, _ = x_sample.shape
    tp, ts = b * s, bs * ls
    tm_p, tm_s = PROMPT_TM, ts

    xp = x_prompt.reshape(tp, d)
    xs = x_sample.reshape(ts, d)

    mem_kt_p, mem_vt_p = _memkv(mem_prompt, norm_mem_g, w_mem_kv, mem_k_norm_g)
    mem_kt_s = cache_mem_k.transpose(0, 1, 3, 4, 2).reshape(2, bs, MEM_W, MEM_LEN)
    mem_vt_s = cache_mem_v.transpose(0, 1, 3, 4, 2).reshape(2, bs, MEM_W, MEM_LEN)

    def mem_attend_sample(qm_s, layer):
        qs8 = jnp.pad(qm_s.reshape(bs, ls, MEM_W), ((0, 0), (0, SAMPLE_ROWS - ls), (0, 0)))
        mo_s = _mem_attend(qs8, mem_kt_s, mem_vt_s, layer, mem_q_norm_g[layer], SAMPLE_ROWS, MEM_SEQS_PER_STEP)
        return mo_s[:, :ls].reshape(ts, MEM_W)

    g_mix0 = norm_mix_g[0][None, :]
    w_bd = _block_diag(w_pool_group[0])
    pscale = pool_scale[0][None, :]
    xp, up_tail = _pool_layer_prompt(x_prompt, g_mix0, w_in_pool[0], w_bd, pscale, mem_kt_p[0], mem_vt_p[0],
                                     mem_q_norm_g[0], w_out[0], tm_p)
    xp = xp.reshape(tp, d)
    pool_p = up_tail[:, _POOL_HALO - POOL_STATE:][None]
    us, qms = _pool_proj(xs, g_mix0, w_in_pool[0], tm_s)
    us_t = us.reshape(bs, ls, MIX_W).transpose(1, 0, 2)
    mix_s_t, new_state_t = _pool_sample(state_pool[0].transpose(1, 0, 2), us_t, w_bd, pscale)
    mix_s = mix_s_t.transpose(1, 0, 2).reshape(ts, MIX_W)
    pool_s = new_state_t.transpose(1, 0, 2)[None]
    xs = _out_proj(xs, mix_s, mem_attend_sample(qms, 0), w_out, 0, tm_s)
    g_ffn0 = norm_ffn_g[0][None, :]
    xp = _ffn(xp, g_ffn0, w_ffn_gu[0], w_ffn_down[0], tm_p)
    xs = _ffn(xs, g_ffn0, w_ffn_gu[0], w_ffn_down[0], tm_s)

    w_in = w_in_fox[0]
    wq = w_in[:, :MIX_W]
    wk = w_in[:, MIX_W:2 * MIX_W]
    wv = w_in[:, 2 * MIX_W:3 * MIX_W]
    wf = jnp.pad(w_in[:, 3 * MIX_W:3 * MIX_W + FOX_HEADS], ((0, 0), (0, LANES - FOX_HEADS)))
    wqm = w_in[:, 3 * MIX_W + FOX_HEADS:]
    bf_row = jnp.pad(b_forget[0], (0, LANES - FOX_HEADS))[None, :]
    qg_t = jnp.tile(fox_q_norm_g[0], FOX_HEADS)[None, :]
    kg_t = jnp.tile(fox_k_norm_g[0], FOX_HEADS)[None, :]
    g_mix1 = norm_mix_g[1][None, :]
    qpp, kpp, ktp, vtp, vtb, lftp, qmp = _fox_proj_prompt(
        xp, g_mix1, wq.T, wk.T, wv.T, wqm, wf, bf_row, fox_q_norm_g[0][:, None], fox_k_norm_g[0][:, None],
        tm=tm_p, n_seq=b)
    qs, kns, vns, lfts, qms = _fox_proj_sample(xs, g_mix1, wq, wk, wv, wqm, wf, bf_row, qg_t, kg_t)

    mix_pt = _fox_flash(qpp, kpp, vtb, FLASH_BLK)

    pad_rows = ((0, 0), (0, SAMPLE_ROWS - ls), (0, 0))
    q8 = jnp.pad(qs.reshape(bs, ls, MIX_W), pad_rows)
    kn8 = jnp.pad(kns.reshape(bs, ls, MIX_W), pad_rows)
    vn8 = jnp.pad(vns.reshape(bs, ls, MIX_W), pad_rows)
    lfn = jnp.pad(lfts.reshape(_FL_ROWS, bs, ls).transpose(1, 0, 2),
                  ((0, 0), (0, 0), (0, SAMPLE_ROWS - ls)))
    n_phys = cache_fox_k.shape[1]
    kt_pages = cache_fox_k[0].transpose(0, 2, 3, 1).reshape(n_phys, MIX_W, PAGE_SIZE)
    vt_pages = cache_fox_v[0].transpose(0, 2, 3, 1).reshape(n_phys, MIX_W, PAGE_SIZE)
    lf_heads = cache_fox_logf[0].transpose(2, 0, 1)
    mix_s = _fox_decode(page_table, kt_pages, vt_pages, lf_heads, q8, kn8, vn8, lfn, ls)
    mix_s = mix_s[:, :ls].reshape(ts, MIX_W)

    mo_p = _mem_attend(qmp.reshape(b, s, MEM_W), mem_kt_p, mem_vt_p, 1, mem_q_norm_g[1],
                       tm_p, 1).reshape(tp, MEM_W)
    xs = _out_proj(xs, mix_s, mem_attend_sample(qms, 1), w_out, 1, tm_s)

    g_ffn1 = norm_ffn_g[1][None, :]
    wrt = w_router[0].T
    xp, hp, rp, cnt_p = _out_proj_router(xp, mix_pt, mo_p, w_out, 1, g_ffn1, wrt, tm_p)
    hs, rs, cnt_s = _router(xs, g_ffn1, wrt, tm_s)
    cnt_p = cnt_p[:, 0].astype(jnp.int32)
    cnt_s = cnt_s[:, 0].astype(jnp.int32)
    counts = cnt_p + cnt_s
    offsets = jnp.cumsum(counts) - counts
    n_sorted = 2 * (tp + ts)

    def dests(r, base):
        i1, i2 = r[0].astype(jnp.int32), r[1].astype(jnp.int32)
        return base[i1] + r[2].astype(jnp.int32), base[i2] + r[3].astype(jnp.int32)

    d1p, d2p = dests(rp, offsets)
    d1s, d2s = dests(rs, offsets + cnt_p)
    x_sorted = _scatter_rows(d1p, d2p, d1s, d2s, hp, hs, n_sorted)
    o_sorted = _gmm(_gmm_meta(counts, n_sorted), x_sorted, w_exp_gu[0], w_exp_down[0])
    yp = _combine(d1p, d2p, xp, rp[4:6].T, o_sorted, tm_p)
    ys = _combine(d1s, d2s, xs, rs[4:6].T, o_sorted, ts)

    def heads_t(a_t, n_b, n_h):
        return a_t.reshape(n_b, n_h, HEAD_DIM, a_t.shape[-1]).transpose(0, 3, 1, 2)

    mem_k_p = jnp.stack([heads_t(mem_kt_p[i], b, MEM_HEADS) for i in range(2)])
    mem_v_p = jnp.stack([heads_t(mem_vt_p[i], b, MEM_HEADS) for i in range(2)])
    fk_p = heads_t(ktp, b, FOX_HEADS)[None]
    fv_p = heads_t(vtp, b, FOX_HEADS)[None]
    fl_p = lftp[:FOX_HEADS].reshape(FOX_HEADS, b, s).transpose(1, 2, 0)[None]
    fk_s = kns.reshape(1, bs, ls, FOX_HEADS, HEAD_DIM)
    fv_s = vns.reshape(1, bs, ls, FOX_HEADS, HEAD_DIM)
    fl_s = lfts[:FOX_HEADS].reshape(FOX_HEADS, bs, ls).transpose(1, 2, 0)[None]
    return (yp.reshape(b, s, d), ys.reshape(bs, ls, d), mem_k_p, mem_v_p, pool_p, pool_s,
            fk_p, fv_p, fl_p, fk_s, fv_s, fl_s)
```
